```python
import jax, jax.numpy as jnp
from jax import lax
import numpy as np

D_MODEL = 1024
BATCH = 32
SEQ = 2048
DEPTH = 1
DEC_BATCH = 8
DEC_SEQ = 64
PAST_LEN = 1024

CHUNK = 64
GMLP_BLOCK = 128
D_MIX = D_MODEL
W_A = D_MIX // 2
A_GROUPS = 4
A_GROUP_DIM = W_A // A_GROUPS
W_B = D_MIX - W_A
B_HEADS = 8
HEAD_DIM = W_B // B_HEADS
LEFT_CHUNKS = 8
ATT_PAST_WINDOW = LEFT_CHUNKS * CHUNK
BAND = ATT_PAST_WINDOW + CHUNK
REL_CLIP = 128
N_REL = CHUNK + REL_CLIP
D_IN = 2 * W_A + 3 * W_B
SPLIT_POINTS = (W_A, 2 * W_A, 2 * W_A + W_B, 2 * W_A + 2 * W_B)
N_EXPERTS = 256
TOP_K = 8
D_EXPERT = 256
D_SHARED = 256
ROUTED_SCALE = 2.5
MOE_BLOCK = 128
EPS = 1e-6
NEG_INF = -1e30

kernel_name = "hymba_gmlp_chunkband_moe_stream_step"


def rms_norm(x, g):
    xf = x.astype(jnp.float32)
    y = xf * lax.rsqrt(jnp.mean(xf * xf, axis=-1, keepdims=True) + EPS)
    return (y * g.astype(jnp.float32)).astype(x.dtype)


def modulate(x, g, shift, scale):
    return rms_norm(x, g) * (1 + scale) + shift


def spatial_gating(u, v, g_v, w_s, b_s):
    bsz, L, _ = v.shape
    blk = min(GMLP_BLOCK, L)
    vn = rms_norm(v.reshape(bsz, L, A_GROUPS, A_GROUP_DIM), g_v)
    pos = np.arange(blk)
    mask = (pos[None, :] // CHUNK) <= (pos[:, None] // CHUNK)
    w = jnp.where(mask, w_s[:, :blk, :blk], 0)
    vb = vn.reshape(bsz, L // blk, blk, A_GROUPS, A_GROUP_DIM)
    z = jnp.einsum("hij,bnjhc->bnihc", w, vb) + b_s[:, :blk].T[None, None, :, :, None]
    return u * z.reshape(bsz, L, W_A), vn


def _attend(q, k, v, rel, valid, rel_bias):
    idx = np.clip(rel, -(CHUNK - 1), REL_CLIP) + (CHUNK - 1)
    bias = rel_bias[:, idx]
    s = jnp.einsum("bqhd,bkhd->bhqk", q, k).astype(jnp.float32) * (HEAD_DIM ** -0.5)
    s = s + bias.astype(jnp.float32)
    if valid is not None:
        s = jnp.where(valid, s, NEG_INF)
    p = jax.nn.softmax(s, axis=-1).astype(v.dtype)
    return jnp.einsum("bhqk,bkhd->bqhd", p, v)


def band_attention_prompt(q, k, v, rel_bias):
    bsz, S, H, Dh = q.shape
    n_chunks = S // CHUNK
    pad = ((0, 0), (ATT_PAST_WINDOW, 0), (0, 0), (0, 0))
    k_pad = jnp.pad(k, pad)
    v_pad = jnp.pad(v, pad)
    rel = ATT_PAST_WINDOW + np.arange(CHUNK)[:, None] - np.arange(BAND)[None, :]

    def one_chunk(n):
        start = n * CHUNK
        qc = lax.dynamic_slice_in_dim(q, start, CHUNK, axis=1)
        kc = lax.dynamic_slice_in_dim(k_pad, start, BAND, axis=1)
        vc = lax.dynamic_slice_in_dim(v_pad, start, BAND, axis=1)
        valid = (start - ATT_PAST_WINDOW + jnp.arange(BAND)) >= 0
        return _attend(qc, kc, vc, rel, valid, rel_bias)

    o = lax.map(one_chunk, jnp.arange(n_chunks))
    return jnp.moveaxis(o, 0, 1).reshape(bsz, S, H * Dh)


def band_attention_sample(q, k_new, v_new, cache_k, cache_v, rel_bias):
    bsz, L, H, Dh = q.shape
    P = cache_k.shape[1]
    k = jnp.concatenate([cache_k, k_new], axis=1)
    v = jnp.concatenate([cache_v, v_new], axis=1)
    rel = P + np.arange(L)[:, None] - np.arange(P + L)[None, :]
    return _attend(q, k, v, rel, None, rel_bias).reshape(bsz, L, H * Dh)


def routed_experts(h, w_router, b_router, w_gate, w_up, w_down):
    T = h.shape[0]
    aff = jax.nn.sigmoid(jnp.dot(h.astype(jnp.float32), w_router.astype(jnp.float32)))
    _, idx = lax.top_k(aff + b_router.astype(jnp.float32), TOP_K)
    gw = jnp.take_along_axis(aff, idx, axis=1)
    gw = gw / jnp.sum(gw, axis=1, keepdims=True) * ROUTED_SCALE
    M = T * TOP_K
    e_flat = idx.reshape(-1)
    tok_flat = jnp.arange(M, dtype=jnp.int32) // TOP_K
    w_flat = gw.reshape(-1)
    order = jnp.argsort(e_flat)
    e_sorted = e_flat[order]
    counts = jnp.bincount(e_flat, length=N_EXPERTS)
    padded = (counts + MOE_BLOCK - 1) // MOE_BLOCK * MOE_BLOCK
    pad_end = jnp.cumsum(padded)
    pad_start = pad_end - padded
    start = jnp.cumsum(counts) - counts
    dest = pad_start[e_sorted] + jnp.arange(M, dtype=jnp.int32) - start[e_sorted]
    n_blocks = -(-M // MOE_BLOCK) + N_EXPERTS
    n_slots = n_blocks * MOE_BLOCK
    slot_tok = jnp.zeros((n_slots,), jnp.int32).at[dest].set(tok_flat[order])
    slot_w = jnp.zeros((n_slots,), jnp.float32).at[dest].set(w_flat[order])
    block_e = jnp.minimum(
        jnp.searchsorted(pad_end, jnp.arange(n_blocks) * MOE_BLOCK, side="right"), N_EXPERTS - 1)

    def body(out, blk):
        tok, wt, e = blk
        xb = h[tok]
        yb = jnp.dot(jax.nn.silu(jnp.dot(xb, w_gate[e])) * jnp.dot(xb, w_up[e]), w_down[e])
        return out.at[tok].add(yb * wt[:, None].astype(yb.dtype)), None

    out, _ = lax.scan(body, jnp.zeros_like(h),
                      (slot_tok.reshape(n_blocks, MOE_BLOCK), slot_w.reshape(n_blocks, MOE_BLOCK), block_e))
    return out


def shared_expert(h, w_g, w_u, w_d):
    return jnp.dot(jax.nn.silu(jnp.dot(h, w_g)) * jnp.dot(h, w_u), w_d)


def trunk_layer(x, c, cache_k, cache_v, w_ada, b_ada, g_mix, w_in, g_gmlp_v, w_spatial, b_spatial,
                g_q, g_k, rel_bias, g_out_a, g_out_b, w_out, g_ffn, w_router, b_router,
                w_exp_gate, w_exp_up, w_exp_down, w_sh_gate, w_sh_up, w_sh_down):
    bsz, L, _ = x.shape
    mod = jnp.dot(jax.nn.silu(c), w_ada) + b_ada
    sh1, sc1, gt1, sh2, sc2, gt2 = jnp.split(mod[:, None, :], 6, axis=-1)

    h = modulate(x, g_mix, sh1, sc1)
    proj = jnp.dot(h, w_in)
    u, v, q, k, va = jnp.split(proj, SPLIT_POINTS, axis=-1)
    a_out, v_rows = spatial_gating(jax.nn.gelu(u), jax.nn.gelu(v), g_gmlp_v, w_spatial, b_spatial)
    q = rms_norm(q.reshape(bsz, L, B_HEADS, HEAD_DIM), g_q)
    k = rms_norm(k.reshape(bsz, L, B_HEADS, HEAD_DIM), g_k)
    va = va.reshape(bsz, L, B_HEADS, HEAD_DIM)
    if cache_k is None:
        b_out = band_attention_prompt(q, k, va, rel_bias)
        keep = min(ATT_PAST_WINDOW, L)
        new_state = (k[:, L - keep:], va[:, L - keep:])
    else:
        b_out = band_attention_sample(q, k, va, cache_k, cache_v, rel_bias)
        new_state = (k, va, v_rows)
    merged = jnp.concatenate([rms_norm(a_out, g_out_a), rms_norm(b_out, g_out_b)], axis=-1)
    x = x + gt1 * jnp.dot(merged, w_out)

    h2 = modulate(x, g_ffn, sh2, sc2).reshape(bsz * L, D_MODEL)
    y = routed_experts(h2, w_router, b_router, w_exp_gate, w_exp_up, w_exp_down) \
        + shared_expert(h2, w_sh_gate, w_sh_up, w_sh_down)
    x = x + gt2 * y.reshape(bsz, L, D_MODEL)
    return x, new_state


def setup_inputs(seed: int = 0) -> dict:
    key = jax.random.key(seed)
    ks = jax.random.split(key, 32)

    def nrm(k, shape, scale):
        return jax.random.normal(k, shape, jnp.float32) * scale

    cache_len = min(ATT_PAST_WINDOW, PAST_LEN)
    D = D_MODEL
    return {
        "x_prompt": nrm(ks[0], (BATCH, SEQ, D), 1.0),
        "x_sample": nrm(ks[1], (DEC_BATCH, DEC_SEQ, D), 1.0),
        "c_prompt": nrm(ks[2], (BATCH, D), 1.0),
        "c_sample": nrm(ks[3], (DEC_BATCH, D), 1.0),
        "cache_k": nrm(ks[4], (DEPTH, DEC_BATCH, cache_len, B_HEADS, HEAD_DIM), 1.0),
        "cache_v": nrm(ks[5], (DEPTH, DEC_BATCH, cache_len, B_HEADS, HEAD_DIM), 1.0),
        "w_ada": nrm(ks[6], (DEPTH, D, 6 * D), 0.3 * D ** -0.5),
        "b_ada": nrm(ks[7], (DEPTH, 6 * D), 0.02),
        "g_mix": 1.0 + nrm(ks[8], (DEPTH, D), 0.02),
        "w_in": nrm(ks[9], (DEPTH, D, D_IN), D ** -0.5),
        "g_gmlp_v": 1.0 + nrm(ks[10], (DEPTH, A_GROUPS, A_GROUP_DIM), 0.02),
        "w_spatial": nrm(ks[11], (DEPTH, A_GROUPS, GMLP_BLOCK, GMLP_BLOCK), GMLP_BLOCK ** -0.5),
        "b_spatial": 1.0 + nrm(ks[12], (DEPTH, A_GROUPS, GMLP_BLOCK), 0.02),
        "g_q": 1.0 + nrm(ks[13], (DEPTH, HEAD_DIM), 0.02),
        "g_k": 1.0 + nrm(ks[14], (DEPTH, HEAD_DIM), 0.02),
        "rel_bias": nrm(ks[15], (DEPTH, B_HEADS, N_REL), 0.1),
        "g_out_a": 1.0 + nrm(ks[16], (DEPTH, W_A), 0.02),
        "g_out_b": 1.0 + nrm(ks[17], (DEPTH, W_B), 0.02),
        "w_out": nrm(ks[18], (DEPTH, D_MIX, D), D_MIX ** -0.5),
        "g_ffn": 1.0 + nrm(ks[19], (DEPTH, D), 0.02),
        "w_router": nrm(ks[20], (DEPTH, D, N_EXPERTS), D ** -0.5),
        "b_router": nrm(ks[21], (DEPTH, N_EXPERTS), 0.01),
        "w_exp_gate": nrm(ks[22], (DEPTH, N_EXPERTS, D, D_EXPERT), D ** -0.5),
        "w_exp_up": nrm(ks[23], (DEPTH, N_EXPERTS, D, D_EXPERT), D ** -0.5),
        "w_exp_down": nrm(ks[24], (DEPTH, N_EXPERTS, D_EXPERT, D), D_EXPERT ** -0.5),
        "w_sh_gate": nrm(ks[25], (DEPTH, D, D_SHARED), D ** -0.5),
        "w_sh_up": nrm(ks[26], (DEPTH, D, D_SHARED), D ** -0.5),
        "w_sh_down": nrm(ks[27], (DEPTH, D_SHARED, D), D_SHARED ** -0.5),
    }


def reference(x_prompt, x_sample, c_prompt, c_sample, cache_k, cache_v, w_ada, b_ada, g_mix, w_in,
              g_gmlp_v, w_spatial, b_spatial, g_q, g_k, rel_bias, g_out_a, g_out_b, w_out, g_ffn,
              w_router, b_router, w_exp_gate, w_exp_up, w_exp_down, w_sh_gate, w_sh_up, w_sh_down):
    y_p, y_s = x_prompt, x_sample
    kp, vp, ksm, vsm, gvs = [], [], [], [], []
    for l in range(DEPTH):
        lw = (w_ada[l], b_ada[l], g_mix[l], w_in[l], g_gmlp_v[l], w_spatial[l], b_spatial[l],
              g_q[l], g_k[l], rel_bias[l], g_out_a[l], g_out_b[l], w_out[l], g_ffn[l],
              w_router[l], b_router[l], w_exp_gate[l], w_exp_up[l], w_exp_down[l],
              w_sh_gate[l], w_sh_up[l], w_sh_down[l])
        y_p, (k_l, v_l) = trunk_layer(y_p, c_prompt, None, None, *lw)
        kp.append(k_l)
        vp.append(v_l)
        y_s, (k_l, v_l, g_l) = trunk_layer(y_s, c_sample, cache_k[l], cache_v[l], *lw)
        ksm.append(k_l)
        vsm.append(v_l)
        gvs.append(g_l)
    new_k_prompt = jnp.stack(kp)
    new_v_prompt = jnp.stack(vp)
    new_k_sample = jnp.stack(ksm)
    new_v_sample = jnp.stack(vsm)
    new_gmlp_v_sample = jnp.stack(gvs)
    return (y_p, y_s, new_k_prompt, new_v_prompt, new_k_sample, new_v_sample, new_gmlp_v_sample)
```

```python
import functools

import jax
import jax.numpy as jnp
import numpy as np
from jax import lax
from jax.experimental import pallas as pl
from jax.experimental.pallas import tpu as pltpu

F32 = jnp.float32
BF16 = jnp.bfloat16

D_MODEL = 1024
CHUNK = 64
GMLP_BLOCK = 128
W_A = 512
A_GROUPS = 4
A_GROUP_DIM = 128
W_B = 512
B_HEADS = 8
HEAD_DIM = 64
LEFT_CHUNKS = 8
ATT_PAST_WINDOW = LEFT_CHUNKS * CHUNK
REL_CLIP = 128
D_IN = 2 * W_A + 3 * W_B
N_EXPERTS = 256
TOP_K = 8
D_EXPERT = 256
ROUTED_SCALE = 2.5
EPS = 1e-6
NEG_INF = -1e30

RANK_BITS = 20
VMEM_LIMIT = 56 * 1024 * 1024


def _cparams(sem, vmem=VMEM_LIMIT):
    return pltpu.CompilerParams(dimension_semantics=sem, vmem_limit_bytes=vmem)


def _rms(x, axis=-1):
    return lax.rsqrt(jnp.mean(x * x, axis=axis, keepdims=True) + EPS)


def _ada_kernel(c_ref, w_ref, b_ref, o_ref):
    c = c_ref[...]
    s = c * jax.nn.sigmoid(c)
    o_ref[...] = jnp.dot(s.astype(BF16), w_ref[...].astype(BF16), preferred_element_type=F32) + b_ref[...]


def _ada(c, w_ada, b_ada):
    n, d = c.shape
    nout = w_ada.shape[1]
    tn = 1024
    return pl.pallas_call(
        _ada_kernel,
        grid=(nout // tn,),
        in_specs=[pl.BlockSpec((n, d), lambda j: (0, 0)),
                  pl.BlockSpec((d, tn), lambda j: (0, j)),
                  pl.BlockSpec((1, tn), lambda j: (0, j))],
        out_specs=pl.BlockSpec((n, tn), lambda j: (0, j)),
        out_shape=jax.ShapeDtypeStruct((n, nout), F32),
        compiler_params=_cparams(("arbitrary",)),
        name="ada",
    )(c, w_ada, b_ada.reshape(1, nout))


def _in_proj_kernel(x_ref, sh_ref, sc_ref, gmix_ref, win_ref, gv_ref, gq_ref, gk_ref, hsum_ref,
                    u_ref, vn_ref, q_ref, kp_ref, vp_ref, k32_ref, v32_ref, *, npad):
    i = pl.program_id(1)

    @pl.when(i < npad)
    def _():
        kp_ref[...] = jnp.zeros_like(kp_ref)
        vp_ref[...] = jnp.zeros_like(vp_ref)

    @pl.when(i >= npad)
    def _():
        x = x_ref[0]
        h = x * _rms(x) * gmix_ref[...]
        h = h * (1.0 + sc_ref[0]) + sh_ref[0]
        proj = jnp.dot(h.astype(BF16), win_ref[...], preferred_element_type=F32)
        u = jax.nn.gelu(proj[:, 0:W_A])
        u_ref[0] = u.astype(u_ref.dtype)
        v = jax.nn.gelu(proj[:, W_A:2 * W_A])
        for g in range(A_GROUPS):
            vg = v[:, g * A_GROUP_DIM:(g + 1) * A_GROUP_DIM]
            vn = vg * _rms(vg) * gv_ref[:, g * A_GROUP_DIM:(g + 1) * A_GROUP_DIM]
            vn_ref[0, :, g * A_GROUP_DIM:(g + 1) * A_GROUP_DIM] = vn.astype(vn_ref.dtype)
        q = proj[:, 2 * W_A:2 * W_A + W_B]
        k = proj[:, 2 * W_A + W_B:2 * W_A + 2 * W_B]
        va = proj[:, 2 * W_A + 2 * W_B:]
        q2 = jnp.dot((q * q).astype(BF16), hsum_ref[...], preferred_element_type=F32) * (1.0 / HEAD_DIM)
        k2 = jnp.dot((k * k).astype(BF16), hsum_ref[...], preferred_element_type=F32) * (1.0 / HEAD_DIM)
        qn = q * lax.rsqrt(q2 + EPS) * gq_ref[...]
        kn = k * lax.rsqrt(k2 + EPS) * gk_ref[...]
        q_ref[0] = (qn * (HEAD_DIM ** -0.5)).astype(q_ref.dtype)
        kp_ref[0] = kn.astype(kp_ref.dtype)
        vp_ref[0] = va.astype(vp_ref.dtype)
        k32_ref[0] = kn
        v32_ref[0] = va


def _in_proj(x, sh1, sc1, g_mix, w_in_bf, gv, gq, gk, hsum, *, tm, pad_rows, vn_dtype):
    b, l, d = x.shape
    assert l % tm == 0 and pad_rows % tm == 0
    npad = pad_rows // tm
    nt = l // tm
    row = lambda bi, i: (bi, jnp.maximum(i - npad, 0), 0)
    bvec = lambda bi, i: (bi, 0, 0)
    const2 = lambda bi, i: (0, 0)
    out_shapes = (
        jax.ShapeDtypeStruct((b, l, W_A), BF16),
        jax.ShapeDtypeStruct((b, l, W_A), vn_dtype),
        jax.ShapeDtypeStruct((b, l, W_B), BF16),
        jax.ShapeDtypeStruct((b, pad_rows + l, W_B), BF16),
        jax.ShapeDtypeStruct((b, pad_rows + l, W_B), BF16),
        jax.ShapeDtypeStruct((b, l, W_B), F32),
        jax.ShapeDtypeStruct((b, l, W_B), F32),
    )
    padrow = lambda bi, i: (bi, i, 0)
    return pl.pallas_call(
        functools.partial(_in_proj_kernel, npad=npad),
        grid=(b, nt + npad),
        in_specs=[pl.BlockSpec((1, tm, d), row),
                  pl.BlockSpec((1, 1, d), bvec),
                  pl.BlockSpec((1, 1, d), bvec),
                  pl.BlockSpec((1, d), const2),
                  pl.BlockSpec((d, D_IN), const2),
                  pl.BlockSpec((1, W_A), const2),
                  pl.BlockSpec((1, W_B), const2),
                  pl.BlockSpec((1, W_B), const2),
                  pl.BlockSpec((W_B, W_B), const2)],
        out_specs=(pl.BlockSpec((1, tm, W_A), row),
                   pl.BlockSpec((1, tm, W_A), row),
                   pl.BlockSpec((1, tm, W_B), row),
                   pl.BlockSpec((1, tm, W_B), padrow),
                   pl.BlockSpec((1, tm, W_B), padrow),
                   pl.BlockSpec((1, tm, W_B), row),
                   pl.BlockSpec((1, tm, W_B), row)),
        out_shape=out_shapes,
        compiler_params=_cparams(("arbitrary", "arbitrary")),
        name="in_proj",
    )(x, sh1, sc1, g_mix, w_in_bf, gv, gq, gk, hsum)


def _mixer_kernel(x_ref, u_ref, vn_ref, q_ref, kp_ref, vp_ref, bias_ref, wsp_ref, bsp_ref,
                  goa_ref, gob_ref, wout_ref, gt1_ref, gffn_ref, sh2_ref, sc2_ref, *rest,
                  sb, nsb, kw, padded, nj, nmain):
    x1_ref, h2_ref, merged_ref = rest[-3:]
    i = pl.program_id(0)
    if len(rest) == 5:
        tx1_ref, th2_ref = rest[:2]

        @pl.when(i >= nmain)
        def _():
            x1_ref[...] = tx1_ref[...]
            h2_ref[...] = th2_ref[...]

    @pl.when(i < nmain)
    def _():
        _mixer_body(x_ref, u_ref, vn_ref, q_ref, kp_ref, vp_ref, bias_ref, wsp_ref, bsp_ref,
                    goa_ref, gob_ref, wout_ref, gt1_ref, gffn_ref, sh2_ref, sc2_ref,
                    x1_ref, h2_ref, merged_ref, i % nj, sb=sb, nsb=nsb, kw=kw, padded=padded)


def _mixer_body(x_ref, u_ref, vn_ref, q_ref, kp_ref, vp_ref, bias_ref, wsp_ref, bsp_ref,
                goa_ref, gob_ref, wout_ref, gt1_ref, gffn_ref, sh2_ref, sc2_ref,
                x1_ref, h2_ref, merged_ref, j, *, sb, nsb, kw, padded):
    rows = sb * nsb
    ri = lax.broadcasted_iota(jnp.int32, (sb, sb), 0)
    ci = lax.broadcasted_iota(jnp.int32, (sb, sb), 1)
    gmask = (ci // CHUNK) <= (ri // CHUNK)
    lane = lax.broadcasted_iota(jnp.int32, (sb, 2 * HEAD_DIM), 1)
    lo_half = lane < HEAD_DIM
    kidx = lax.broadcasted_iota(jnp.int32, (sb, kw), 1)

    for s in range(nsb):
        r0 = s * sb
        vn = vn_ref[0, r0:r0 + sb, :].astype(BF16)
        u = u_ref[0, r0:r0 + sb, :].astype(F32)
        zs = []
        for g in range(A_GROUPS):
            w = jnp.where(gmask, wsp_ref[g], 0.0).astype(BF16)
            z = jnp.dot(w, vn[:, g * A_GROUP_DIM:(g + 1) * A_GROUP_DIM], preferred_element_type=F32)
            zs.append(z + bsp_ref[g])
        a = u * jnp.concatenate(zs, axis=1)
        a = a * _rms(a) * goa_ref[...]
        merged_ref[r0:r0 + sb, 0:W_A] = a.astype(BF16)

        if padded:
            blk0 = j * rows + r0
            kstart = pl.multiple_of(blk0, sb)
            kb = kp_ref[0, pl.ds(kstart, kw), :]
            vb = vp_ref[0, pl.ds(kstart, kw), :]
            valid = kidx >= (ATT_PAST_WINDOW - blk0)
        else:
            kb = kp_ref[0]
            vb = vp_ref[0]
            valid = None
        qb = q_ref[0, r0:r0 + sb, :]
        outs = []
        for hp in range(B_HEADS // 2):
            c0 = hp * 2 * HEAD_DIM
            qp = qb[:, c0:c0 + 2 * HEAD_DIM]
            kp = kb[:, c0:c0 + 2 * HEAD_DIM]
            vp = vb[:, c0:c0 + 2 * HEAD_DIM]
            o_pair = []
            for hh in range(2):
                qm = jnp.where(lo_half if hh == 0 else jnp.logical_not(lo_half), qp, jnp.zeros_like(qp))
                sc = lax.dot_general(qm, kp, (((1,), (1,)), ((), ())), preferred_element_type=F32)
                sc = sc + bias_ref[2 * hp + hh]
                if valid is not None:
                    sc = jnp.where(valid, sc, NEG_INF)
                m = jnp.max(sc, axis=-1, keepdims=True)
                p = jnp.exp(sc - m)
                l = jnp.sum(p, axis=-1, keepdims=True)
                o = jnp.dot(p.astype(BF16), vp, preferred_element_type=F32)
                o_pair.append(o / l)
            outs.append(jnp.where(lo_half, o_pair[0], o_pair[1]))
        bo = jnp.concatenate(outs, axis=1)
        bo = bo * _rms(bo) * gob_ref[...]
        merged_ref[r0:r0 + sb, W_A:W_A + W_B] = bo.astype(BF16)

    mix = jnp.dot(merged_ref[...], wout_ref[...], preferred_element_type=F32)
    x1 = x_ref[0] + gt1_ref[0] * mix
    x1_ref[...] = x1
    h2 = x1 * _rms(x1) * gffn_ref[...]
    h2_ref[...] = h2 * (1.0 + sc2_ref[0]) + sh2_ref[0]


def _mixer(x, u, vn, q, kp, vp, bias, wsp, bsp, goa, gob, wout_bf, gt1, gffn, sh2, sc2,
           *, sb, nsb, padded, tail=None):
    b, l, d = x.shape
    rows = sb * nsb
    assert l % rows == 0
    nj = l // rows
    nmain = b * nj
    ntail = 0
    if tail is not None:
        assert tail[0].shape[0] % rows == 0
        ntail = tail[0].shape[0] // rows
    total_rows = (nmain + ntail) * rows
    kw = bias.shape[2]
    kl = kp.shape[1]
    bidx = lambda i: jnp.minimum(i // nj, b - 1)
    row = lambda i: (bidx(i), i % nj, 0)
    bvec = lambda i: (bidx(i), 0, 0)
    c2 = lambda i: (0, 0)
    c3 = lambda i: (0, 0, 0)
    pool = lambda i: (i, 0)
    in_specs = [pl.BlockSpec((1, rows, d), row),
                pl.BlockSpec((1, rows, W_A), row),
                pl.BlockSpec((1, rows, W_A), row),
                pl.BlockSpec((1, rows, W_B), row),
                pl.BlockSpec((1, kl, W_B), bvec),
                pl.BlockSpec((1, kl, W_B), bvec),
                pl.BlockSpec(bias.shape, c3),
                pl.BlockSpec(wsp.shape, c3),
                pl.BlockSpec(bsp.shape, c3),
                pl.BlockSpec((1, W_A), c2),
                pl.BlockSpec((1, W_B), c2),
                pl.BlockSpec((d, d), c2),
                pl.BlockSpec((1, 1, d), bvec),
                pl.BlockSpec((1, d), c2),
                pl.BlockSpec((1, 1, d), bvec),
                pl.BlockSpec((1, 1, d), bvec)]
    args = [x, u, vn, q, kp, vp, bias, wsp, bsp, goa, gob, wout_bf, gt1, gffn, sh2, sc2]
    if tail is not None:
        tmap = lambda i: (jnp.maximum(i - nmain, 0), 0)
        in_specs += [pl.BlockSpec((rows, d), tmap), pl.BlockSpec((rows, d), tmap)]
        args += list(tail)
    return pl.pallas_call(
        functools.partial(_mixer_kernel, sb=sb, nsb=nsb, kw=kw, padded=padded, nj=nj, nmain=nmain),
        grid=(nmain + ntail,),
        in_specs=in_specs,
        out_specs=(pl.BlockSpec((rows, d), pool), pl.BlockSpec((rows, d), pool)),
        out_shape=(jax.ShapeDtypeStruct((total_rows, d), F32), jax.ShapeDtypeStruct((total_rows, d), F32)),
        scratch_shapes=[pltpu.VMEM((rows, d), BF16)],
        compiler_params=_cparams(("arbitrary",)),
        name="mixer",
    )(*args)


def _route_kernel(h2_ref, wr_ref, br_ref, pk_ref, gw_ref, cnt_ref, run_ref, *, tm):
    i = pl.program_id(0)

    @pl.when(i == 0)
    def _():
        run_ref[...] = jnp.zeros_like(run_ref)

    logits = jnp.dot(h2_ref[...].astype(BF16), wr_ref[...], preferred_element_type=F32)
    aff = jax.nn.sigmoid(logits)
    sel = aff + br_ref[...]
    lane = lax.broadcasted_iota(jnp.int32, (tm, N_EXPERTS), 1)
    idxs, affs = [], []
    onehot = jnp.zeros((tm, N_EXPERTS), F32)
    for _ in range(TOP_K):
        m = jnp.max(sel, axis=-1, keepdims=True)
        idx = jnp.min(jnp.where(sel == m, lane, N_EXPERTS), axis=-1, keepdims=True)
        oh = lane == idx
        affs.append(jnp.sum(jnp.where(oh, aff, 0.0), axis=-1, keepdims=True))
        idxs.append(idx)
        sel = jnp.where(oh, -jnp.inf, sel)
        onehot = onehot + oh.astype(F32)
    tot = affs[0]
    for a in affs[1:]:
        tot = tot + a
    ri = lax.broadcasted_iota(jnp.int32, (tm, tm), 0)
    ci = lax.broadcasted_iota(jnp.int32, (tm, tm), 1)
    tri = (ci < ri).astype(BF16)
    before = jnp.dot(tri, onehot.astype(BF16), preferred_element_type=F32) + run_ref[...]
    out_lane = lax.broadcasted_iota(jnp.int32, (tm, 128), 1)
    pk = jnp.zeros((tm, 128), jnp.int32)
    gw = jnp.zeros((tm, 128), F32)
    for k in range(TOP_K):
        rank = jnp.sum(jnp.where(lane == idxs[k], before, 0.0), axis=-1, keepdims=True).astype(jnp.int32)
        packed = idxs[k] * (1 << RANK_BITS) + rank
        pk = jnp.where(out_lane == k, packed, pk)
        gw = jnp.where(out_lane == k, affs[k] / tot * ROUTED_SCALE, gw)
    pk_ref[...] = pk
    gw_ref[...] = gw
    run = run_ref[...] + jnp.sum(onehot, axis=0, keepdims=True)
    run_ref[...] = run
    cnt_ref[...] = run.astype(jnp.int32)


def _route(h2, wr_bf, b_router, *, tm):
    t, d = h2.shape
    return pl.pallas_call(
        functools.partial(_route_kernel, tm=tm),
        grid=(t // tm,),
        in_specs=[pl.BlockSpec((tm, d), lambda i: (i, 0)),
                  pl.BlockSpec((d, N_EXPERTS), lambda i: (0, 0)),
                  pl.BlockSpec((1, N_EXPERTS), lambda i: (0, 0))],
        out_specs=(pl.BlockSpec((tm, 128), lambda i: (i, 0)),
                   pl.BlockSpec((tm, 128), lambda i: (i, 0)),
                   pl.BlockSpec((1, N_EXPERTS), lambda i: (0, 0))),
        out_shape=(jax.ShapeDtypeStruct((t, 128), jnp.int32),
                   jax.ShapeDtypeStruct((t, 128), F32),
                   jax.ShapeDtypeStruct((1, N_EXPERTS), jnp.int32)),
        scratch_shapes=[pltpu.VMEM((1, N_EXPERTS), F32)],
        compiler_params=_cparams(("arbitrary",)),
        name="route",
    )(h2, wr_bf, b_router.reshape(1, N_EXPERTS))


def _idx_copy(pk_hbm, idx_smem, sem, step, n):
    slot = step % 2
    return pltpu.make_async_copy(pk_hbm.at[pl.ds(pl.multiple_of(step * n, n), n)], idx_smem.at[slot], sem.at[slot])


def _dispatch_kernel(offs_ref, h2_ref, pk_hbm, xs_hbm, idx_smem, idx_sem, row_sem, *, tm):
    i = pl.program_id(0)
    n = tm * TOP_K

    @pl.when(i == 0)
    def _():
        _idx_copy(pk_hbm, idx_smem, idx_sem, i, n).start()

    _idx_copy(pk_hbm, idx_smem, idx_sem, i, n).wait()

    @pl.when(i + 1 < pl.num_programs(0))
    def _():
        _idx_copy(pk_hbm, idx_smem, idx_sem, i + 1, n).start()

    slot = i % 2

    def body(t, carry):
        for k in range(TOP_K):
            p = idx_smem[slot, t * TOP_K + k]
            dst = offs_ref[p >> RANK_BITS] + (p & ((1 << RANK_BITS) - 1))
            pltpu.make_async_copy(h2_ref.at[pl.ds(t, 1)], xs_hbm.at[pl.ds(dst, 1)], row_sem).start()
        return carry

    lax.fori_loop(0, tm, body, 0)
    pltpu.make_async_copy(xs_hbm.at[pl.ds(0, n)], xs_hbm.at[pl.ds(0, n)], row_sem).wait()


def _dispatch(offs, h2, pk_flat, *, tm):
    t, d = h2.shape
    n = tm * TOP_K
    grid_spec = pltpu.PrefetchScalarGridSpec(
        num_scalar_prefetch=1,
        grid=(t // tm,),
        in_specs=[pl.BlockSpec((tm, d), lambda i, offs: (i, 0)),
                  pl.BlockSpec(memory_space=pl.ANY)],
        out_specs=pl.BlockSpec(memory_space=pl.ANY),
        scratch_shapes=[pltpu.SMEM((2, n), jnp.int32),
                        pltpu.SemaphoreType.DMA((2,)),
                        pltpu.SemaphoreType.DMA(())],
    )
    return pl.pallas_call(
        functools.partial(_dispatch_kernel, tm=tm),
        grid_spec=grid_spec,
        out_shape=jax.ShapeDtypeStruct((t * TOP_K, d), F32),
        compiler_params=_cparams(("arbitrary",)),
        name="dispatch",
    )(offs, h2, pk_flat)


def _experts_kernel(vt_ref, vg_ref, offs_ref, nv_ref, xs_ref, wg_ref, wu_ref, wd_ref, ys_ref, *, tm):
    v = pl.program_id(0)

    @pl.when(v < nv_ref[0])
    def _():
        g = vg_ref[v]
        tile = vt_ref[v]
        first = jnp.logical_or(v == 0, vt_ref[jnp.maximum(v - 1, 0)] != tile)
        rowid = tile * tm + lax.broadcasted_iota(jnp.int32, (tm, 1), 0)
        mine = jnp.logical_and(rowid >= offs_ref[g], rowid < offs_ref[g + 1])
        x = xs_ref[...].astype(BF16)
        hg = jnp.dot(x, wg_ref[0].astype(BF16), preferred_element_type=F32)
        hu = jnp.dot(x, wu_ref[0].astype(BF16), preferred_element_type=F32)
        hm = (hg * jax.nn.sigmoid(hg) * hu).astype(BF16)
        y = jnp.dot(hm, wd_ref[0].astype(BF16), preferred_element_type=F32)

        @pl.when(first)
        def _():
            ys_ref[...] = jnp.where(mine, y, 0.0)

        @pl.when(jnp.logical_not(first))
        def _():
            ys_ref[...] = jnp.where(mine, y, ys_ref[...])


def _experts(vt, vg, offs, nv, xs, w_gate, w_up, w_down, *, tm):
    m, d = xs.shape
    nvis = vt.shape[0]
    de = w_gate.shape[2]
    xmap = lambda v, vt, vg, offs, nv: (vt[v], 0)
    wmap = lambda v, vt, vg, offs, nv: (vg[v], 0, 0)
    grid_spec = pltpu.PrefetchScalarGridSpec(
        num_scalar_prefetch=4,
        grid=(nvis,),
        in_specs=[pl.BlockSpec((tm, d), xmap),
                  pl.BlockSpec((1, d, de), wmap),
                  pl.BlockSpec((1, d, de), wmap),
                  pl.BlockSpec((1, de, d), wmap)],
        out_specs=pl.BlockSpec((tm, d), xmap),
    )
    return pl.pallas_call(
        functools.partial(_experts_kernel, tm=tm),
        grid_spec=grid_spec,
        out_shape=jax.ShapeDtypeStruct((m, d), F32),
        compiler_params=_cparams(("arbitrary",)),
        name="experts",
    )(vt, vg, offs, nv, xs, w_gate, w_up, w_down)


def _combine_kernel(offs_ref, x1_ref, h2_ref, gw_ref, gt2_ref, wsg_ref, wsu_ref, wsd_ref, pk_hbm, ys_hbm,
                    y_ref, idx_smem, buf_ref, idx_sem, row_sem, *, tm, step0, bpt):
    i = pl.program_id(0)
    n = tm * TOP_K

    @pl.when(i == 0)
    def _():
        _idx_copy(pk_hbm, idx_smem, idx_sem, step0 + i, n).start()

    _idx_copy(pk_hbm, idx_smem, idx_sem, step0 + i, n).wait()

    @pl.when(i + 1 < pl.num_programs(0))
    def _():
        _idx_copy(pk_hbm, idx_smem, idx_sem, step0 + i + 1, n).start()

    slot = (step0 + i) % 2

    def body(t, carry):
        for k in range(TOP_K):
            p = idx_smem[slot, t * TOP_K + k]
            src = offs_ref[p >> RANK_BITS] + (p & ((1 << RANK_BITS) - 1))
            pltpu.make_async_copy(ys_hbm.at[pl.ds(src, 1)], buf_ref.at[k, pl.ds(t, 1)], row_sem).start()
        return carry

    lax.fori_loop(0, tm, body, 0)

    h2 = h2_ref[...].astype(BF16)
    sg = jnp.dot(h2, wsg_ref[...], preferred_element_type=F32)
    su = jnp.dot(h2, wsu_ref[...], preferred_element_type=F32)
    acc = jnp.dot((sg * jax.nn.sigmoid(sg) * su).astype(BF16), wsd_ref[...], preferred_element_type=F32)

    pltpu.make_async_copy(ys_hbm.at[pl.ds(0, n)], ys_hbm.at[pl.ds(0, n)], row_sem).wait()
    gw = gw_ref[...]
    routed = gw[:, 0:1] * buf_ref[0]
    for k in range(1, TOP_K):
        routed = routed + gw[:, k:k + 1] * buf_ref[k]
    ffn = routed + acc
    x1 = x1_ref[...]
    seq = tm // bpt
    for b in range(bpt):
        r = slice(b * seq, (b + 1) * seq)
        y_ref[b] = x1[r] + gt2_ref[b] * ffn[r]


def _combine(offs, x1, h2, gw, gt2, wsg, wsu, wsd, pk_flat, ys, *, tm, batch, seq, row_offset):
    d = x1.shape[1]
    n = tm * TOP_K
    bpt = max(tm // seq, 1)
    per_b = max(seq // tm, 1)
    ob = tm // bpt
    step0 = row_offset // tm
    tok = lambda i, offs: (step0 + i, 0)
    c2 = lambda i, offs: (0, 0)
    grid_spec = pltpu.PrefetchScalarGridSpec(
        num_scalar_prefetch=1,
        grid=(batch * seq // tm,),
        in_specs=[pl.BlockSpec((tm, d), tok),
                  pl.BlockSpec((tm, d), tok),
                  pl.BlockSpec((tm, 128), tok),
                  pl.BlockSpec((bpt, 1, d), lambda i, offs: (i // per_b, 0, 0)),
                  pl.BlockSpec(wsg.shape, c2),
                  pl.BlockSpec(wsu.shape, c2),
                  pl.BlockSpec(wsd.shape, c2),
                  pl.BlockSpec(memory_space=pl.ANY),
                  pl.BlockSpec(memory_space=pl.ANY)],
        out_specs=pl.BlockSpec((bpt, ob, d), lambda i, offs: (i // per_b, i % per_b, 0)),
        scratch_shapes=[pltpu.SMEM((2, n), jnp.int32),
                        pltpu.VMEM((TOP_K, tm, d), F32),
                        pltpu.SemaphoreType.DMA((2,)),
                        pltpu.SemaphoreType.DMA(())],
    )
    return pl.pallas_call(
        functools.partial(_combine_kernel, tm=tm, step0=step0, bpt=bpt),
        grid_spec=grid_spec,
        out_shape=jax.ShapeDtypeStruct((batch, seq, d), F32),
        compiler_params=_cparams(("arbitrary",)),
        name="combine",
    )(offs, x1, h2, gw, gt2, wsg, wsu, wsd, pk_flat, ys)


def _bias_table(rel_bias, sb, kw):
    r = np.arange(sb)[:, None]
    kk = np.arange(kw)[None, :]
    rel = r + ATT_PAST_WINDOW - kk
    idx = np.clip(rel, -(CHUNK - 1), REL_CLIP) + (CHUNK - 1)
    qc, kc = r // CHUNK, kk // CHUNK
    band = (kc >= qc) & (kc <= qc + LEFT_CHUNKS)
    return jnp.where(band[None], rel_bias[:, idx].astype(F32), NEG_INF)


def _visit_plan(offs, m, tm, nvis):
    cnt = offs[1:] - offs[:-1]
    first = offs[:-1] // tm
    last = jnp.maximum(offs[1:] - 1, 0) // tm
    per = jnp.where(cnt > 0, last - first + 1, 0)
    vend = jnp.cumsum(per)
    vstart = vend - per
    total = vend[-1]
    v = jnp.arange(nvis, dtype=jnp.int32)
    g = jnp.minimum(jnp.searchsorted(vend, v, side="right"), N_EXPERTS - 1).astype(jnp.int32)
    tile = (first[g] + v - vstart[g]).astype(jnp.int32)
    live = v < total
    last_tile = m // tm - 1
    tile = jnp.where(live, tile, last_tile)
    g = jnp.where(live, g, N_EXPERTS - 1)
    return tile, g, total.reshape(1).astype(jnp.int32)


def kernel(x_prompt, x_sample, c_prompt, c_sample, cache_k, cache_v, w_ada, b_ada, g_mix, w_in, g_gmlp_v,
           w_spatial, b_spatial, g_q, g_k, rel_bias, g_out_a, g_out_b, w_out, g_ffn, w_router, b_router,
           w_exp_gate, w_exp_up, w_exp_down, w_sh_gate, w_sh_up, w_sh_down):
    depth = w_ada.shape[0]
    assert depth == 1, "single layer"
    bp, sp, d = x_prompt.shape
    bs, ss, _ = x_sample.shape
    tp, ts = bp * sp, bs * ss
    t_all = tp + ts
    l = 0

    c_all = jnp.concatenate([c_prompt, c_sample], axis=0)
    nb = c_all.shape[0]
    nb_pad = -(-nb // 8) * 8
    c_all = jnp.pad(c_all, ((0, nb_pad - nb), (0, 0)))
    mod = _ada(c_all, w_ada[l], b_ada[l])[:nb].reshape(nb, 6, 1, d)
    sh1, sc1, gt1, sh2, sc2, gt2 = [mod[:, i] for i in range(6)]

    w_in_bf = w_in[l].astype(BF16)
    w_out_bf = w_out[l].astype(BF16)
    gv = g_gmlp_v[l].reshape(1, W_A)
    gq = jnp.tile(g_q[l], B_HEADS).reshape(1, W_B)
    gk = jnp.tile(g_k[l], B_HEADS).reshape(1, W_B)
    hid = np.arange(W_B) // HEAD_DIM
    hsum = jnp.asarray(hid[:, None] == hid[None, :], BF16)
    gmix = g_mix[l].reshape(1, d)
    goa = g_out_a[l].reshape(1, W_A)
    gob = g_out_b[l].reshape(1, W_B)
    gffn = g_ffn[l].reshape(1, d)
    bias_p = _bias_table(rel_bias[l], GMLP_BLOCK, ATT_PAST_WINDOW + GMLP_BLOCK)
    bsp_p = jnp.broadcast_to(b_spatial[l][:, :, None], (A_GROUPS, GMLP_BLOCK, A_GROUP_DIM))

    cl = cache_k.shape[2]
    us, vns, qs, ks_bf, vs_bf, k32s, v32s = _in_proj(
        x_sample, sh1[bp:], sc1[bp:], gmix, w_in_bf, gv, gq, gk, hsum,
        tm=ss, pad_rows=0, vn_dtype=F32)
    kcat = jnp.concatenate([cache_k[l].reshape(bs, cl, W_B).astype(BF16), ks_bf], axis=1)
    vcat = jnp.concatenate([cache_v[l].reshape(bs, cl, W_B).astype(BF16), vs_bf], axis=1)
    bias_s = bias_p[:, :ss, ATT_PAST_WINDOW - cl:ATT_PAST_WINDOW + ss]
    x1_s, h2_s = _mixer(x_sample, us, vns, qs, kcat, vcat, bias_s, w_spatial[l][:, :ss, :ss], bsp_p[:, :ss],
                        goa, gob, w_out_bf, gt1[bp:], gffn, sh2[bp:], sc2[bp:],
                        sb=ss, nsb=1, padded=False)
    new_k_sample = k32s.reshape(1, bs, ss, B_HEADS, HEAD_DIM)
    new_v_sample = v32s.reshape(1, bs, ss, B_HEADS, HEAD_DIM)
    new_gmlp_v_sample = vns.reshape(1, bs, ss, A_GROUPS, A_GROUP_DIM)

    u, vn, q, kpad, vpad, k32, v32 = _in_proj(
        x_prompt, sh1[:bp], sc1[:bp], gmix, w_in_bf, gv, gq, gk, hsum,
        tm=512, pad_rows=ATT_PAST_WINDOW, vn_dtype=BF16)
    x1, h2 = _mixer(x_prompt, u, vn, q, kpad, vpad, bias_p, w_spatial[l], bsp_p, goa, gob, w_out_bf,
                    gt1[:bp], gffn, sh2[:bp], sc2[:bp],
                    sb=GMLP_BLOCK, nsb=2, padded=True, tail=(x1_s, h2_s))
    assert x1.shape[0] == t_all
    keep = min(ATT_PAST_WINDOW, sp)
    new_k_prompt = k32[:, sp - keep:].reshape(1, bp, keep, B_HEADS, HEAD_DIM)
    new_v_prompt = v32[:, sp - keep:].reshape(1, bp, keep, B_HEADS, HEAD_DIM)

    tmr = 256
    tme = 512
    assert t_all % tmr == 0 and (t_all * TOP_K) % tme == 0 and sp % 256 == 0 and ts % (2 * ss) == 0
    assert t_all * TOP_K < (1 << RANK_BITS)
    pk, gw, cnt = _route(h2, w_router[l].astype(BF16), b_router[l], tm=tmr)
    offs = jnp.concatenate([jnp.zeros((1,), jnp.int32), jnp.cumsum(cnt[0]).astype(jnp.int32)])
    pk_flat = pk[:, :TOP_K].reshape(-1)
    xs = _dispatch(offs, h2, pk_flat, tm=tmr)
    m = t_all * TOP_K
    nvis = m // tme + N_EXPERTS - 1
    vt, vg, nv = _visit_plan(offs, m, tme, nvis)
    ys = _experts(vt, vg, offs, nv, xs, w_exp_gate[l], w_exp_up[l], w_exp_down[l], tm=tme)

    wsg, wsu, wsd = w_sh_gate[l].astype(BF16), w_sh_up[l].astype(BF16), w_sh_down[l].astype(BF16)
    y_prompt = _combine(offs, x1, h2, gw, gt2[:bp], wsg, wsu, wsd, pk_flat, ys,
                        tm=256, batch=bp, seq=sp, row_offset=0)
    y_sample = _combine(offs, x1, h2, gw, gt2[bp:], wsg, wsu, wsd, pk_flat, ys,
                        tm=2 * ss, batch=bs, seq=ss, row_offset=tp)
    return (y_prompt, y_sample, new_k_prompt, new_v_prompt, new_k_sample, new_v_sample, new_gmlp_v_sample)
```

```python
import functools

import jax
import jax.numpy as jnp
import numpy as np
from jax import lax
from jax.experimental import pallas as pl
from jax.experimental.pallas import tpu as pltpu

F32 = jnp.float32
BF16 = jnp.bfloat16

D_MODEL = 1024
CHUNK = 64
GMLP_BLOCK = 128
W_A = 512
A_GROUPS = 4
A_GROUP_DIM = 128
W_B = 512
B_HEADS = 8
HEAD_DIM = 64
LEFT_CHUNKS = 8
ATT_PAST_WINDOW = LEFT_CHUNKS * CHUNK
REL_CLIP = 128
D_IN = 2 * W_A + 3 * W_B
N_EXPERTS = 256
TOP_K = 8
D_EXPERT = 256
ROUTED_SCALE = 2.5
EPS = 1e-6
NEG_INF = -1e30

RANK_BITS = 20
VMEM_LIMIT = 56 * 1024 * 1024


def _cparams(sem, vmem=VMEM_LIMIT):
    return pltpu.CompilerParams(dimension_semantics=sem, vmem_limit_bytes=vmem)


def _rms(x, axis=-1):
    return lax.rsqrt(jnp.mean(x * x, axis=axis, keepdims=True) + EPS)


def _ada_kernel(c_ref, w_ref, b_ref, o_ref):
    c = c_ref[...]
    s = c * jax.nn.sigmoid(c)
    o_ref[...] = jnp.dot(s.astype(BF16), w_ref[...].astype(BF16), preferred_element_type=F32) + b_ref[...]


def _ada(c, w_ada, b_ada):
    n, d = c.shape
    nout = w_ada.shape[1]
    tn = 1024
    return pl.pallas_call(
        _ada_kernel,
        grid=(nout // tn,),
        in_specs=[pl.BlockSpec((n, d), lambda j: (0, 0)),
                  pl.BlockSpec((d, tn), lambda j: (0, j)),
                  pl.BlockSpec((1, tn), lambda j: (0, j))],
        out_specs=pl.BlockSpec((n, tn), lambda j: (0, j)),
        out_shape=jax.ShapeDtypeStruct((n, nout), F32),
        compiler_params=_cparams(("arbitrary",)),
        name="ada",
    )(c, w_ada, b_ada.reshape(1, nout))


def _in_proj_kernel(x_ref, sh_ref, sc_ref, gmix_ref, win_ref, gv_ref, gq_ref, gk_ref, hsum_ref,
                    u_ref, vn_ref, q_ref, kp_ref, vp_ref, k32_ref, v32_ref, *, npad):
    i = pl.program_id(1)

    @pl.when(i < npad)
    def _():
        kp_ref[...] = jnp.zeros_like(kp_ref)
        vp_ref[...] = jnp.zeros_like(vp_ref)

    @pl.when(i >= npad)
    def _():
        x = x_ref[0]
        h = x * _rms(x) * gmix_ref[...]
        h = h * (1.0 + sc_ref[0]) + sh_ref[0]
        proj = jnp.dot(h.astype(BF16), win_ref[...], preferred_element_type=F32)
        u = jax.nn.gelu(proj[:, 0:W_A])
        u_ref[0] = u.astype(u_ref.dtype)
        v = jax.nn.gelu(proj[:, W_A:2 * W_A])
        for g in range(A_GROUPS):
            vg = v[:, g * A_GROUP_DIM:(g + 1) * A_GROUP_DIM]
            vn = vg * _rms(vg) * gv_ref[:, g * A_GROUP_DIM:(g + 1) * A_GROUP_DIM]
            vn_ref[0, :, g * A_GROUP_DIM:(g + 1) * A_GROUP_DIM] = vn.astype(vn_ref.dtype)
        q = proj[:, 2 * W_A:2 * W_A + W_B]
        k = proj[:, 2 * W_A + W_B:2 * W_A + 2 * W_B]
        va = proj[:, 2 * W_A + 2 * W_B:]
        q2 = jnp.dot((q * q).astype(BF16), hsum_ref[...], preferred_element_type=F32) * (1.0 / HEAD_DIM)
        k2 = jnp.dot((k * k).astype(BF16), hsum_ref[...], preferred_element_type=F32) * (1.0 / HEAD_DIM)
        qn = q * lax.rsqrt(q2 + EPS) * gq_ref[...]
        kn = k * lax.rsqrt(k2 + EPS) * gk_ref[...]
        q_ref[0] = (qn * (HEAD_DIM ** -0.5)).astype(q_ref.dtype)
        kp_ref[0] = kn.astype(kp_ref.dtype)
        vp_ref[0] = va.astype(vp_ref.dtype)
        k32_ref[0] = kn
        v32_ref[0] = va


def _in_proj(x, sh1, sc1, g_mix, w_in_bf, gv, gq, gk, hsum, *, tm, pad_rows, vn_dtype):
    b, l, d = x.shape
    assert l % tm == 0 and pad_rows % tm == 0
    npad = pad_rows // tm
    nt = l // tm
    row = lambda bi, i: (bi, jnp.maximum(i - npad, 0), 0)
    bvec = lambda bi, i: (bi, 0, 0)
    const2 = lambda bi, i: (0, 0)
    out_shapes = (
        jax.ShapeDtypeStruct((b, l, W_A), BF16),
        jax.ShapeDtypeStruct((b, l, W_A), vn_dtype),
        jax.ShapeDtypeStruct((b, l, W_B), BF16),
        jax.ShapeDtypeStruct((b, pad_rows + l, W_B), BF16),
        jax.ShapeDtypeStruct((b, pad_rows + l, W_B), BF16),
        jax.ShapeDtypeStruct((b, l, W_B), F32),
        jax.ShapeDtypeStruct((b, l, W_B), F32),
    )
    padrow = lambda bi, i: (bi, i, 0)
    return pl.pallas_call(
        functools.partial(_in_proj_kernel, npad=npad),
        grid=(b, nt + npad),
        in_specs=[pl.BlockSpec((1, tm, d), row),
                  pl.BlockSpec((1, 1, d), bvec),
                  pl.BlockSpec((1, 1, d), bvec),
                  pl.BlockSpec((1, d), const2),
                  pl.BlockSpec((d, D_IN), const2),
                  pl.BlockSpec((1, W_A), const2),
                  pl.BlockSpec((1, W_B), const2),
                  pl.BlockSpec((1, W_B), const2),
                  pl.BlockSpec((W_B, W_B), const2)],
        out_specs=(pl.BlockSpec((1, tm, W_A), row),
                   pl.BlockSpec((1, tm, W_A), row),
                   pl.BlockSpec((1, tm, W_B), row),
                   pl.BlockSpec((1, tm, W_B), padrow),
                   pl.BlockSpec((1, tm, W_B), padrow),
                   pl.BlockSpec((1, tm, W_B), row),
                   pl.BlockSpec((1, tm, W_B), row)),
        out_shape=out_shapes,
        compiler_params=_cparams(("arbitrary", "arbitrary")),
        name="in_proj",
    )(x, sh1, sc1, g_mix, w_in_bf, gv, gq, gk, hsum)


def _mixer_kernel(x_ref, u_ref, vn_ref, q_ref, kp_ref, vp_ref, bias_ref, wsp_ref, bsp_ref,
                  goa_ref, gob_ref, wout_ref, gt1_ref, gffn_ref, sh2_ref, sc2_ref, *rest,
                  sb, nsb, kw, padded, nj, nmain):
    x1_ref, h2_ref, merged_ref = rest[-3:]
    i = pl.program_id(0)
    if len(rest) == 5:
        tx1_ref, th2_ref = rest[:2]

        @pl.when(i >= nmain)
        def _():
            x1_ref[...] = tx1_ref[...]
            h2_ref[...] = th2_ref[...]

    @pl.when(i < nmain)
    def _():
        _mixer_body(x_ref, u_ref, vn_ref, q_ref, kp_ref, vp_ref, bias_ref, wsp_ref, bsp_ref,
                    goa_ref, gob_ref, wout_ref, gt1_ref, gffn_ref, sh2_ref, sc2_ref,
                    x1_ref, h2_ref, merged_ref, i % nj, sb=sb, nsb=nsb, kw=kw, padded=padded)


def _mixer_body(x_ref, u_ref, vn_ref, q_ref, kp_ref, vp_ref, bias_ref, wsp_ref, bsp_ref,
                goa_ref, gob_ref, wout_ref, gt1_ref, gffn_ref, sh2_ref, sc2_ref,
                x1_ref, h2_ref, merged_ref, j, *, sb, nsb, kw, padded):
    rows = sb * nsb
    ri = lax.broadcasted_iota(jnp.int32, (sb, sb), 0)
    ci = lax.broadcasted_iota(jnp.int32, (sb, sb), 1)
    gmask = (ci // CHUNK) <= (ri // CHUNK)
    lane = lax.broadcasted_iota(jnp.int32, (sb, 2 * HEAD_DIM), 1)
    lo_half = lane < HEAD_DIM
    kidx = lax.broadcasted_iota(jnp.int32, (sb, kw), 1)

    for s in range(nsb):
        r0 = s * sb
        vn = vn_ref[0, r0:r0 + sb, :].astype(BF16)
        u = u_ref[0, r0:r0 + sb, :].astype(F32)
        zs = []
        for g in range(A_GROUPS):
            w = jnp.where(gmask, wsp_ref[g], 0.0).astype(BF16)
            z = jnp.dot(w, vn[:, g * A_GROUP_DIM:(g + 1) * A_GROUP_DIM], preferred_element_type=F32)
            zs.append(z + bsp_ref[g])
        a = u * jnp.concatenate(zs, axis=1)
        a = a * _rms(a) * goa_ref[...]
        merged_ref[r0:r0 + sb, 0:W_A] = a.astype(BF16)

        if padded:
            blk0 = j * rows + r0
            kstart = pl.multiple_of(blk0, sb)
            kb = kp_ref[0, pl.ds(kstart, kw), :]
            vb = vp_ref[0, pl.ds(kstart, kw), :]
            valid = kidx >= (ATT_PAST_WINDOW - blk0)
        else:
            kb = kp_ref[0]
            vb = vp_ref[0]
            valid = None
        qb = q_ref[0, r0:r0 + sb, :]
        outs = []
        for hp in range(B_HEADS // 2):
            c0 = hp * 2 * HEAD_DIM
            qp = qb[:, c0:c0 + 2 * HEAD_DIM]
            kp = kb[:, c0:c0 + 2 * HEAD_DIM]
            vp = vb[:, c0:c0 + 2 * HEAD_DIM]
            o_pair = []
            for hh in range(2):
                qm = jnp.where(lo_half if hh == 0 else jnp.logical_not(lo_half), qp, jnp.zeros_like(qp))
                sc = lax.dot_general(qm, kp, (((1,), (1,)), ((), ())), preferred_element_type=F32)
                sc = sc + bias_ref[2 * hp + hh]
                if valid is not None:
                    sc = jnp.where(valid, sc, NEG_INF)
                m = jnp.max(sc, axis=-1, keepdims=True)
                p = jnp.exp(sc - m)
                l = jnp.sum(p, axis=-1, keepdims=True)
                o = jnp.dot(p.astype(BF16), vp, preferred_element_type=F32)
                o_pair.append(o / l)
            outs.append(jnp.where(lo_half, o_pair[0], o_pair[1]))
        bo = jnp.concatenate(outs, axis=1)
        bo = bo * _rms(bo) * gob_ref[...]
        merged_ref[r0:r0 + sb, W_A:W_A + W_B] = bo.astype(BF16)

    mix = jnp.dot(merged_ref[...], wout_ref[...], preferred_element_type=F32)
    x1 = x_ref[0] + gt1_ref[0] * mix
    x1_ref[...] = x1
    h2 = x1 * _rms(x1) * gffn_ref[...]
    h2_ref[...] = h2 * (1.0 + sc2_ref[0]) + sh2_ref[0]


def _mixer(x, u, vn, q, kp, vp, bias, wsp, bsp, goa, gob, wout_bf, gt1, gffn, sh2, sc2,
           *, sb, nsb, padded, tail=None):
    b, l, d = x.shape
    rows = sb * nsb
    assert l % rows == 0
    nj = l // rows
    nmain = b * nj
    ntail = 0
    if tail is not None:
        assert tail[0].shape[0] % rows == 0
        ntail = tail[0].shape[0] // rows
    total_rows = (nmain + ntail) * rows
    kw = bias.shape[2]
    kl = kp.shape[1]
    bidx = lambda i: jnp.minimum(i // nj, b - 1)
    row = lambda i: (bidx(i), i % nj, 0)
    bvec = lambda i: (bidx(i), 0, 0)
    c2 = lambda i: (0, 0)
    c3 = lambda i: (0, 0, 0)
    pool = lambda i: (i, 0)
    in_specs = [pl.BlockSpec((1, rows, d), row),
                pl.BlockSpec((1, rows, W_A), row),
                pl.BlockSpec((1, rows, W_A), row),
                pl.BlockSpec((1, rows, W_B), row),
                pl.BlockSpec((1, kl, W_B), bvec),
                pl.BlockSpec((1, kl, W_B), bvec),
                pl.BlockSpec(bias.shape, c3),
                pl.BlockSpec(wsp.shape, c3),
                pl.BlockSpec(bsp.shape, c3),
                pl.BlockSpec((1, W_A), c2),
                pl.BlockSpec((1, W_B), c2),
                pl.BlockSpec((d, d), c2),
                pl.BlockSpec((1, 1, d), bvec),
                pl.BlockSpec((1, d), c2),
                pl.BlockSpec((1, 1, d), bvec),
                pl.BlockSpec((1, 1, d), bvec)]
    args = [x, u, vn, q, kp, vp, bias, wsp, bsp, goa, gob, wout_bf, gt1, gffn, sh2, sc2]
    if tail is not None:
        tmap = lambda i: (jnp.maximum(i - nmain, 0), 0)
        in_specs += [pl.BlockSpec((rows, d), tmap), pl.BlockSpec((rows, d), tmap)]
        args += list(tail)
    return pl.pallas_call(
        functools.partial(_mixer_kernel, sb=sb, nsb=nsb, kw=kw, padded=padded, nj=nj, nmain=nmain),
        grid=(nmain + ntail,),
        in_specs=in_specs,
        out_specs=(pl.BlockSpec((rows, d), pool), pl.BlockSpec((rows, d), pool)),
        out_shape=(jax.ShapeDtypeStruct((total_rows, d), F32), jax.ShapeDtypeStruct((total_rows, d), F32)),
        scratch_shapes=[pltpu.VMEM((rows, d), BF16)],
        compiler_params=_cparams(("arbitrary",)),
        name="mixer",
    )(*args)


def _route_kernel(h2_ref, wr_ref, br_ref, pk_ref, gw_ref, cnt_ref, run_ref, *, tm):
    i = pl.program_id(0)

    @pl.when(i == 0)
    def _():
        run_ref[...] = jnp.zeros_like(run_ref)

    logits = jnp.dot(h2_ref[...].astype(BF16), wr_ref[...], preferred_element_type=F32)
    aff = jax.nn.sigmoid(logits)
    sel = aff + br_ref[...]
    lane = lax.broadcasted_iota(jnp.int32, (tm, N_EXPERTS), 1)
    idxs, affs = [], []
    onehot = jnp.zeros((tm, N_EXPERTS), F32)
    for _ in range(TOP_K):
        m = jnp.max(sel, axis=-1, keepdims=True)
        idx = jnp.min(jnp.where(sel == m, lane, N_EXPERTS), axis=-1, keepdims=True)
        oh = lane == idx
        affs.append(jnp.sum(jnp.where(oh, aff, 0.0), axis=-1, keepdims=True))
        idxs.append(idx)
        sel = jnp.where(oh, -jnp.inf, sel)
        onehot = onehot + oh.astype(F32)
    tot = affs[0]
    for a in affs[1:]:
        tot = tot + a
    ri = lax.broadcasted_iota(jnp.int32, (tm, tm), 0)
    ci = lax.broadcasted_iota(jnp.int32, (tm, tm), 1)
    tri = (ci < ri).astype(BF16)
    before = jnp.dot(tri, onehot.astype(BF16), preferred_element_type=F32) + run_ref[...]
    out_lane = lax.broadcasted_iota(jnp.int32, (tm, 128), 1)
    pk = jnp.zeros((tm, 128), jnp.int32)
    gw = jnp.zeros((tm, 128), F32)
    for k in range(TOP_K):
        rank = jnp.sum(jnp.where(lane == idxs[k], before, 0.0), axis=-1, keepdims=True).astype(jnp.int32)
        packed = idxs[k] * (1 << RANK_BITS) + rank
        pk = jnp.where(out_lane == k, packed, pk)
        gw = jnp.where(out_lane == k, affs[k] / tot * ROUTED_SCALE, gw)
    pk_ref[...] = pk
    gw_ref[...] = gw
    run = run_ref[...] + jnp.sum(onehot, axis=0, keepdims=True)
    run_ref[...] = run
    cnt_ref[...] = run.astype(jnp.int32)


def _route(h2, wr_bf, b_router, *, tm):
    t, d = h2.shape
    return pl.pallas_call(
        functools.partial(_route_kernel, tm=tm),
        grid=(t // tm,),
        in_specs=[pl.BlockSpec((tm, d), lambda i: (i, 0)),
                  pl.BlockSpec((d, N_EXPERTS), lambda i: (0, 0)),
                  pl.BlockSpec((1, N_EXPERTS), lambda i: (0, 0))],
        out_specs=(pl.BlockSpec((tm, 128), lambda i: (i, 0)),
                   pl.BlockSpec((tm, 128), lambda i: (i, 0)),
                   pl.BlockSpec((1, N_EXPERTS), lambda i: (0, 0))),
        out_shape=(jax.ShapeDtypeStruct((t, 128), jnp.int32),
                   jax.ShapeDtypeStruct((t, 128), F32),
                   jax.ShapeDtypeStruct((1, N_EXPERTS), jnp.int32)),
        scratch_shapes=[pltpu.VMEM((1, N_EXPERTS), F32)],
        compiler_params=_cparams(("arbitrary",)),
        name="route",
    )(h2, wr_bf, b_router.reshape(1, N_EXPERTS))


def _idx_copy(pk_hbm, idx_smem, sem, step, n):
    base = pl.multiple_of((step % 2) * n, n)
    return pltpu.make_async_copy(pk_hbm.at[pl.ds(pl.multiple_of(step * n, n), n)],
                                 idx_smem.at[pl.ds(base, n)], sem.at[step % 2])


def _idx_pipeline(pk_hbm, idx_smem, idx_sem, step, n, has_next):
    @pl.when(pl.program_id(0) == 0)
    def _():
        _idx_copy(pk_hbm, idx_smem, idx_sem, step, n).start()

    _idx_copy(pk_hbm, idx_smem, idx_sem, step, n).wait()

    @pl.when(has_next)
    def _():
        _idx_copy(pk_hbm, idx_smem, idx_sem, step + 1, n).start()

    return (step % 2) * n


def _slots_kernel(pk_ref, offs_ref, dst_ref):
    p = pk_ref[...]
    e = p >> RANK_BITS
    half = N_EXPERTS // 2
    lo = jnp.take_along_axis(jnp.broadcast_to(offs_ref[:, 0:half], p.shape), e & (half - 1), axis=1)
    hi = jnp.take_along_axis(jnp.broadcast_to(offs_ref[:, half:N_EXPERTS], p.shape), e & (half - 1), axis=1)
    dst_ref[...] = (p & ((1 << RANK_BITS) - 1)) + jnp.where(e >= half, hi, lo)


def _slots(pk, offs, *, tm):
    t = pk.shape[0]
    return pl.pallas_call(
        _slots_kernel,
        grid=(t // tm,),
        in_specs=[pl.BlockSpec((tm, 128), lambda i: (i, 0)),
                  pl.BlockSpec((1, N_EXPERTS), lambda i: (0, 0))],
        out_specs=pl.BlockSpec((tm, 128), lambda i: (i, 0)),
        out_shape=jax.ShapeDtypeStruct((t, 128), jnp.int32),
        compiler_params=_cparams(("arbitrary",)),
        name="slots",
    )(pk, offs[:N_EXPERTS].reshape(1, N_EXPERTS))


def _dispatch_kernel(h2_ref, dst_hbm, xs_hbm, idx_smem, idx_sem, row_sem, *, tm):
    i = pl.program_id(0)
    n = tm * TOP_K
    base = _idx_pipeline(dst_hbm, idx_smem, idx_sem, i, n, i + 1 < pl.num_programs(0))

    def body(t8, carry):
        t0 = pl.multiple_of(t8 * 8, 8)
        b = base + t0 * TOP_K
        for tt in range(8):
            src = h2_ref.at[pl.ds(t0 + tt, 1)]
            for k in range(TOP_K):
                pltpu.make_async_copy(src, xs_hbm.at[pl.ds(idx_smem[b + tt * TOP_K + k], 1)],
                                      row_sem).start(priority=k % 2)
        return carry

    lax.fori_loop(0, tm // 8, body, 0)
    pltpu.make_async_copy(xs_hbm.at[pl.ds(0, n)], xs_hbm.at[pl.ds(0, n)], row_sem).wait()


def _dispatch(h2, dst_flat, *, tm):
    t, d = h2.shape
    n = tm * TOP_K
    return pl.pallas_call(
        functools.partial(_dispatch_kernel, tm=tm),
        grid=(t // tm,),
        in_specs=[pl.BlockSpec((tm, d), lambda i: (i, 0)),
                  pl.BlockSpec(memory_space=pl.ANY)],
        out_specs=pl.BlockSpec(memory_space=pl.ANY),
        scratch_shapes=[pltpu.SMEM((2 * n,), jnp.int32),
                        pltpu.SemaphoreType.DMA((2,)),
                        pltpu.SemaphoreType.DMA(())],
        out_shape=jax.ShapeDtypeStruct((t * TOP_K, d), F32),
        compiler_params=_cparams(("arbitrary",)),
        name="dispatch",
    )(h2, dst_flat)


def _experts_kernel(vt_ref, vg_ref, offs_ref, nv_ref, xs_ref, wg_ref, wu_ref, wd_ref, ys_ref,
                    wgb_ref, wub_ref, wdb_ref, *, tm):
    v = pl.program_id(0)

    @pl.when(v < nv_ref[0])
    def _():
        g = vg_ref[v]
        tile = vt_ref[v]
        prev = jnp.maximum(v - 1, 0)
        first = jnp.logical_or(v == 0, vt_ref[prev] != tile)

        @pl.when(jnp.logical_or(v == 0, vg_ref[prev] != g))
        def _():
            wgb_ref[...] = wg_ref[0].astype(BF16)
            wub_ref[...] = wu_ref[0].astype(BF16)
            wdb_ref[...] = wd_ref[0].astype(BF16)

        rowid = tile * tm + lax.broadcasted_iota(jnp.int32, (tm, 1), 0)
        mine = jnp.logical_and(rowid >= offs_ref[g], rowid < offs_ref[g + 1])
        x = xs_ref[...].astype(BF16)
        hg = jnp.dot(x, wgb_ref[...], preferred_element_type=F32)
        hu = jnp.dot(x, wub_ref[...], preferred_element_type=F32)
        hm = (hg * jax.nn.sigmoid(hg) * hu).astype(BF16)
        y = jnp.dot(hm, wdb_ref[...], preferred_element_type=F32)

        @pl.when(first)
        def _():
            ys_ref[...] = jnp.where(mine, y, 0.0)

        @pl.when(jnp.logical_not(first))
        def _():
            ys_ref[...] = jnp.where(mine, y, ys_ref[...])


def _experts(vt, vg, offs, nv, xs, w_gate, w_up, w_down, *, tm):
    m, d = xs.shape
    nvis = vt.shape[0]
    de = w_gate.shape[2]
    xmap = lambda v, vt, vg, offs, nv: (vt[v], 0)
    wmap = lambda v, vt, vg, offs, nv: (vg[v], 0, 0)
    grid_spec = pltpu.PrefetchScalarGridSpec(
        num_scalar_prefetch=4,
        grid=(nvis,),
        in_specs=[pl.BlockSpec((tm, d), xmap),
                  pl.BlockSpec((1, d, de), wmap),
                  pl.BlockSpec((1, d, de), wmap),
                  pl.BlockSpec((1, de, d), wmap)],
        out_specs=pl.BlockSpec((tm, d), xmap),
        scratch_shapes=[pltpu.VMEM((d, de), BF16), pltpu.VMEM((d, de), BF16), pltpu.VMEM((de, d), BF16)],
    )
    return pl.pallas_call(
        functools.partial(_experts_kernel, tm=tm),
        grid_spec=grid_spec,
        out_shape=jax.ShapeDtypeStruct((m, d), F32),
        compiler_params=_cparams(("arbitrary",)),
        name="experts",
    )(vt, vg, offs, nv, xs, w_gate, w_up, w_down)


def _combine_kernel(x1_ref, h2_ref, gw_ref, gt2_ref, wsg_ref, wsu_ref, wsd_ref, dst_hbm, ys_hbm,
                    y_ref, idx_smem, buf_ref, idx_sem, row_sem, *, tm, step0, bpt):
    i = pl.program_id(0)
    nsteps = pl.num_programs(0)
    n = tm * TOP_K

    def gather(step):
        _idx_copy(dst_hbm, idx_smem, idx_sem, step0 + step, n).wait()
        base = ((step0 + step) % 2) * n
        half = step % 2

        def body(t8, carry):
            t0 = pl.multiple_of(t8 * 8, 8)
            b = base + t0 * TOP_K
            for tt in range(8):
                for k in range(TOP_K):
                    pltpu.make_async_copy(ys_hbm.at[pl.ds(idx_smem[b + tt * TOP_K + k], 1)],
                                          buf_ref.at[half, k, pl.ds(t0 + tt, 1)],
                                          row_sem.at[half]).start(priority=k % 2)
            return carry

        lax.fori_loop(0, tm // 8, body, 0)

        @pl.when(step + 1 < nsteps)
        def _():
            _idx_copy(dst_hbm, idx_smem, idx_sem, step0 + step + 1, n).start()

    @pl.when(i == 0)
    def _():
        _idx_copy(dst_hbm, idx_smem, idx_sem, step0, n).start()
        gather(i)

    @pl.when(i + 1 < nsteps)
    def _():
        gather(i + 1)

    h2 = h2_ref[...].astype(BF16)
    sg = jnp.dot(h2, wsg_ref[...], preferred_element_type=F32)
    su = jnp.dot(h2, wsu_ref[...], preferred_element_type=F32)
    acc = jnp.dot((sg * jax.nn.sigmoid(sg) * su).astype(BF16), wsd_ref[...], preferred_element_type=F32)

    cur = i % 2
    pltpu.make_async_copy(ys_hbm.at[pl.ds(0, n)], ys_hbm.at[pl.ds(0, n)], row_sem.at[cur]).wait()
    gw = gw_ref[...]
    routed = gw[:, 0:1] * buf_ref[cur, 0]
    for k in range(1, TOP_K):
        routed = routed + gw[:, k:k + 1] * buf_ref[cur, k]
    ffn = routed + acc
    x1 = x1_ref[...]
    seq = tm // bpt
    for b in range(bpt):
        r = slice(b * seq, (b + 1) * seq)
        y_ref[b] = x1[r] + gt2_ref[b] * ffn[r]


def _combine(x1, h2, gw, gt2, wsg, wsu, wsd, dst_flat, ys, *, tm, batch, seq, row_offset):
    d = x1.shape[1]
    n = tm * TOP_K
    bpt = max(tm // seq, 1)
    per_b = max(seq // tm, 1)
    ob = tm // bpt
    step0 = row_offset // tm
    tok = lambda i: (step0 + i, 0)
    c2 = lambda i: (0, 0)
    return pl.pallas_call(
        functools.partial(_combine_kernel, tm=tm, step0=step0, bpt=bpt),
        grid=(batch * seq // tm,),
        in_specs=[pl.BlockSpec((tm, d), tok),
                  pl.BlockSpec((tm, d), tok),
                  pl.BlockSpec((tm, 128), tok),
                  pl.BlockSpec((bpt, 1, d), lambda i: (i // per_b, 0, 0)),
                  pl.BlockSpec(wsg.shape, c2),
                  pl.BlockSpec(wsu.shape, c2),
                  pl.BlockSpec(wsd.shape, c2),
                  pl.BlockSpec(memory_space=pl.ANY),
                  pl.BlockSpec(memory_space=pl.ANY)],
        out_specs=pl.BlockSpec((bpt, ob, d), lambda i: (i // per_b, i % per_b, 0)),
        scratch_shapes=[pltpu.SMEM((2 * n,), jnp.int32),
                        pltpu.VMEM((2, TOP_K, tm, d), F32),
                        pltpu.SemaphoreType.DMA((2,)),
                        pltpu.SemaphoreType.DMA((2,))],
        out_shape=jax.ShapeDtypeStruct((batch, seq, d), F32),
        compiler_params=_cparams(("arbitrary",)),
        name="combine",
    )(x1, h2, gw, gt2, wsg, wsu, wsd, dst_flat, ys)


def _bias_table(rel_bias, sb, kw):
    r = np.arange(sb)[:, None]
    kk = np.arange(kw)[None, :]
    rel = r + ATT_PAST_WINDOW - kk
    idx = np.clip(rel, -(CHUNK - 1), REL_CLIP) + (CHUNK - 1)
    qc, kc = r // CHUNK, kk // CHUNK
    band = (kc >= qc) & (kc <= qc + LEFT_CHUNKS)
    return jnp.where(band[None], rel_bias[:, idx].astype(F32), NEG_INF)


def _visit_plan(offs, m, tm, nvis):
    cnt = offs[1:] - offs[:-1]
    first = offs[:-1] // tm
    last = jnp.maximum(offs[1:] - 1, 0) // tm
    per = jnp.where(cnt > 0, last - first + 1, 0)
    vend = jnp.cumsum(per)
    vstart = vend - per
    total = vend[-1]
    v = jnp.arange(nvis, dtype=jnp.int32)
    g = jnp.minimum(jnp.sum(vend[None, :] <= v[:, None], axis=1), N_EXPERTS - 1).astype(jnp.int32)
    tile = (first[g] + v - vstart[g]).astype(jnp.int32)
    live = v < total
    last_tile = m // tm - 1
    tile = jnp.where(live, tile, last_tile)
    g = jnp.where(live, g, N_EXPERTS - 1)
    return tile, g, total.reshape(1).astype(jnp.int32)


def kernel(x_prompt, x_sample, c_prompt, c_sample, cache_k, cache_v, w_ada, b_ada, g_mix, w_in, g_gmlp_v,
           w_spatial, b_spatial, g_q, g_k, rel_bias, g_out_a, g_out_b, w_out, g_ffn, w_router, b_router,
           w_exp_gate, w_exp_up, w_exp_down, w_sh_gate, w_sh_up, w_sh_down):
    depth = w_ada.shape[0]
    assert depth == 1, "single layer"
    bp, sp, d = x_prompt.shape
    bs, ss, _ = x_sample.shape
    tp, ts = bp * sp, bs * ss
    t_all = tp + ts
    l = 0

    c_all = jnp.concatenate([c_prompt, c_sample], axis=0)
    nb = c_all.shape[0]
    nb_pad = -(-nb // 8) * 8
    c_all = jnp.pad(c_all, ((0, nb_pad - nb), (0, 0)))
    mod = _ada(c_all, w_ada[l], b_ada[l])[:nb].reshape(nb, 6, 1, d)
    sh1, sc1, gt1, sh2, sc2, gt2 = [mod[:, i] for i in range(6)]

    w_in_bf = w_in[l].astype(BF16)
    w_out_bf = w_out[l].astype(BF16)
    gv = g_gmlp_v[l].reshape(1, W_A)
    gq = jnp.tile(g_q[l], B_HEADS).reshape(1, W_B)
    gk = jnp.tile(g_k[l], B_HEADS).reshape(1, W_B)
    hid = np.arange(W_B) // HEAD_DIM
    hsum = jnp.asarray(hid[:, None] == hid[None, :], BF16)
    gmix = g_mix[l].reshape(1, d)
    goa = g_out_a[l].reshape(1, W_A)
    gob = g_out_b[l].reshape(1, W_B)
    gffn = g_ffn[l].reshape(1, d)
    bias_p = _bias_table(rel_bias[l], GMLP_BLOCK, ATT_PAST_WINDOW + GMLP_BLOCK)
    bsp_p = jnp.broadcast_to(b_spatial[l][:, :, None], (A_GROUPS, GMLP_BLOCK, A_GROUP_DIM))

    cl = cache_k.shape[2]
    us, vns, qs, ks_bf, vs_bf, k32s, v32s = _in_proj(
        x_sample, sh1[bp:], sc1[bp:], gmix, w_in_bf, gv, gq, gk, hsum,
        tm=ss, pad_rows=0, vn_dtype=F32)
    kcat = jnp.concatenate([cache_k[l].reshape(bs, cl, W_B).astype(BF16), ks_bf], axis=1)
    vcat = jnp.concatenate([cache_v[l].reshape(bs, cl, W_B).astype(BF16), vs_bf], axis=1)
    bias_s = bias_p[:, :ss, ATT_PAST_WINDOW - cl:ATT_PAST_WINDOW + ss]
    x1_s, h2_s = _mixer(x_sample, us, vns, qs, kcat, vcat, bias_s, w_spatial[l][:, :ss, :ss], bsp_p[:, :ss],
                        goa, gob, w_out_bf, gt1[bp:], gffn, sh2[bp:], sc2[bp:],
                        sb=ss, nsb=1, padded=False)
    new_k_sample = k32s.reshape(1, bs, ss, B_HEADS, HEAD_DIM)
    new_v_sample = v32s.reshape(1, bs, ss, B_HEADS, HEAD_DIM)
    new_gmlp_v_sample = vns.reshape(1, bs, ss, A_GROUPS, A_GROUP_DIM)

    u, vn, q, kpad, vpad, k32, v32 = _in_proj(
        x_prompt, sh1[:bp], sc1[:bp], gmix, w_in_bf, gv, gq, gk, hsum,
        tm=512, pad_rows=ATT_PAST_WINDOW, vn_dtype=BF16)
    x1, h2 = _mixer(x_prompt, u, vn, q, kpad, vpad, bias_p, w_spatial[l], bsp_p, goa, gob, w_out_bf,
                    gt1[:bp], gffn, sh2[:bp], sc2[:bp],
                    sb=GMLP_BLOCK, nsb=2, padded=True, tail=(x1_s, h2_s))
    assert x1.shape[0] == t_all
    keep = min(ATT_PAST_WINDOW, sp)
    new_k_prompt = k32[:, sp - keep:].reshape(1, bp, keep, B_HEADS, HEAD_DIM)
    new_v_prompt = v32[:, sp - keep:].reshape(1, bp, keep, B_HEADS, HEAD_DIM)

    tmr = 256
    tme = 512
    assert t_all % tmr == 0 and (t_all * TOP_K) % tme == 0 and sp % 256 == 0 and ts % (2 * ss) == 0
    assert t_all * TOP_K < (1 << RANK_BITS)
    pk, gw, cnt = _route(h2, w_router[l].astype(BF16), b_router[l], tm=tmr)
    offs = jnp.concatenate([jnp.zeros((1,), jnp.int32), jnp.cumsum(cnt[0]).astype(jnp.int32)])
    dst_flat = _slots(pk, offs, tm=tmr)[:, :TOP_K].reshape(-1)
    xs = _dispatch(h2, dst_flat, tm=tmr)
    m = t_all * TOP_K
    nvis = m // tme + N_EXPERTS - 1
    vt, vg, nv = _visit_plan(offs, m, tme, nvis)
    ys = _experts(vt, vg, offs, nv, xs, w_exp_gate[l], w_exp_up[l], w_exp_down[l], tm=tme)

    wsg, wsu, wsd = w_sh_gate[l].astype(BF16), w_sh_up[l].astype(BF16), w_sh_down[l].astype(BF16)
    y_prompt = _combine(x1, h2, gw, gt2[:bp], wsg, wsu, wsd, dst_flat, ys,
                        tm=256, batch=bp, seq=sp, row_offset=0)
    y_sample = _combine(x1, h2, gw, gt2[bp:], wsg, wsu, wsd, dst_flat, ys,
                        tm=2 * ss, batch=bs, seq=ss, row_offset=tp)
    return (y_prompt, y_sample, new_k_prompt, new_v_prompt, new_k_sample, new_v_sample, new_gmlp_v_sample)
```

```python
import functools

import jax
import jax.numpy as jnp
import numpy as np
from jax import lax
from jax.experimental import pallas as pl
from jax.experimental.pallas import tpu as pltpu

F32 = jnp.float32
BF16 = jnp.bfloat16

D_MODEL = 1024
CHUNK = 64
GMLP_BLOCK = 128
W_A = 512
A_GROUPS = 4
A_GROUP_DIM = 128
W_B = 512
B_HEADS = 8
HEAD_DIM = 64
LEFT_CHUNKS = 8
ATT_PAST_WINDOW = LEFT_CHUNKS * CHUNK
REL_CLIP = 128
D_IN = 2 * W_A + 3 * W_B
N_EXPERTS = 256
TOP_K = 8
D_EXPERT = 256
ROUTED_SCALE = 2.5
EPS = 1e-6
NEG_INF = -1e30

RANK_BITS = 20
VMEM_LIMIT = 56 * 1024 * 1024


def _cparams(sem, vmem=VMEM_LIMIT):
    return pltpu.CompilerParams(dimension_semantics=sem, vmem_limit_bytes=vmem)


def _rms(x, axis=-1):
    return lax.rsqrt(jnp.mean(x * x, axis=axis, keepdims=True) + EPS)


def _ada_kernel(c_ref, w_ref, b_ref, o_ref):
    c = c_ref[...]
    s = c * jax.nn.sigmoid(c)
    o_ref[...] = jnp.dot(s.astype(BF16), w_ref[...].astype(BF16), preferred_element_type=F32) + b_ref[...]


def _ada(c, w_ada, b_ada):
    n, d = c.shape
    nout = w_ada.shape[1]
    tn = 1024
    return pl.pallas_call(
        _ada_kernel,
        grid=(nout // tn,),
        in_specs=[pl.BlockSpec((n, d), lambda j: (0, 0)),
                  pl.BlockSpec((d, tn), lambda j: (0, j)),
                  pl.BlockSpec((1, tn), lambda j: (0, j))],
        out_specs=pl.BlockSpec((n, tn), lambda j: (0, j)),
        out_shape=jax.ShapeDtypeStruct((n, nout), F32),
        compiler_params=_cparams(("arbitrary",)),
        name="ada",
    )(c, w_ada, b_ada.reshape(1, nout))


def _in_proj_kernel(x_ref, sh_ref, sc_ref, gmix_ref, win_ref, gv_ref, gq_ref, gk_ref, hsum_ref,
                    u_ref, vn_ref, q_ref, kp_ref, vp_ref, k32_ref, v32_ref, *, npad, first_kept):
    i = pl.program_id(1)

    @pl.when(i < npad)
    def _():
        kp_ref[...] = jnp.zeros_like(kp_ref)
        vp_ref[...] = jnp.zeros_like(vp_ref)

    @pl.when(i >= npad)
    def _():
        x = x_ref[0]
        h = x * _rms(x) * gmix_ref[...]
        h = h * (1.0 + sc_ref[0]) + sh_ref[0]
        proj = jnp.dot(h.astype(BF16), win_ref[...], preferred_element_type=F32)
        u = jax.nn.gelu(proj[:, 0:W_A])
        u_ref[0] = u.astype(u_ref.dtype)
        v = jax.nn.gelu(proj[:, W_A:2 * W_A])
        for g in range(A_GROUPS):
            vg = v[:, g * A_GROUP_DIM:(g + 1) * A_GROUP_DIM]
            vn = vg * _rms(vg) * gv_ref[:, g * A_GROUP_DIM:(g + 1) * A_GROUP_DIM]
            vn_ref[0, :, g * A_GROUP_DIM:(g + 1) * A_GROUP_DIM] = vn.astype(vn_ref.dtype)
        q = proj[:, 2 * W_A:2 * W_A + W_B]
        k = proj[:, 2 * W_A + W_B:2 * W_A + 2 * W_B]
        va = proj[:, 2 * W_A + 2 * W_B:]
        q2 = jnp.dot((q * q).astype(BF16), hsum_ref[...], preferred_element_type=F32) * (1.0 / HEAD_DIM)
        k2 = jnp.dot((k * k).astype(BF16), hsum_ref[...], preferred_element_type=F32) * (1.0 / HEAD_DIM)
        qn = q * lax.rsqrt(q2 + EPS) * gq_ref[...]
        kn = k * lax.rsqrt(k2 + EPS) * gk_ref[...]
        q_ref[0] = (qn * (HEAD_DIM ** -0.5)).astype(q_ref.dtype)
        kp_ref[0] = kn.astype(kp_ref.dtype)
        vp_ref[0] = va.astype(vp_ref.dtype)

        @pl.when(i >= first_kept)
        def _():
            k32_ref[0] = kn
            v32_ref[0] = va


def _in_proj(x, sh1, sc1, g_mix, w_in_bf, gv, gq, gk, hsum, *, tm, pad_rows, keep, vn_dtype):
    b, l, d = x.shape
    assert l % tm == 0 and pad_rows % tm == 0 and keep % tm == 0
    npad = pad_rows // tm
    nt = l // tm
    first_kept = npad + nt - keep // tm
    row = lambda bi, i: (bi, jnp.maximum(i - npad, 0), 0)
    kept = lambda bi, i: (bi, jnp.maximum(i - first_kept, 0), 0)
    bvec = lambda bi, i: (bi, 0, 0)
    const2 = lambda bi, i: (0, 0)
    out_shapes = (
        jax.ShapeDtypeStruct((b, l, W_A), BF16),
        jax.ShapeDtypeStruct((b, l, W_A), vn_dtype),
        jax.ShapeDtypeStruct((b, l, W_B), BF16),
        jax.ShapeDtypeStruct((b, pad_rows + l, W_B), BF16),
        jax.ShapeDtypeStruct((b, pad_rows + l, W_B), BF16),
        jax.ShapeDtypeStruct((b, keep, W_B), F32),
        jax.ShapeDtypeStruct((b, keep, W_B), F32),
    )
    padrow = lambda bi, i: (bi, i, 0)
    return pl.pallas_call(
        functools.partial(_in_proj_kernel, npad=npad, first_kept=first_kept),
        grid=(b, nt + npad),
        in_specs=[pl.BlockSpec((1, tm, d), row),
                  pl.BlockSpec((1, 1, d), bvec),
                  pl.BlockSpec((1, 1, d), bvec),
                  pl.BlockSpec((1, d), const2),
                  pl.BlockSpec((d, D_IN), const2),
                  pl.BlockSpec((1, W_A), const2),
                  pl.BlockSpec((1, W_B), const2),
                  pl.BlockSpec((1, W_B), const2),
                  pl.BlockSpec((W_B, W_B), const2)],
        out_specs=(pl.BlockSpec((1, tm, W_A), row),
                   pl.BlockSpec((1, tm, W_A), row),
                   pl.BlockSpec((1, tm, W_B), row),
                   pl.BlockSpec((1, tm, W_B), padrow),
                   pl.BlockSpec((1, tm, W_B), padrow),
                   pl.BlockSpec((1, tm, W_B), kept),
                   pl.BlockSpec((1, tm, W_B), kept)),
        out_shape=out_shapes,
        compiler_params=_cparams(("arbitrary", "arbitrary")),
        name="in_proj",
    )(x, sh1, sc1, g_mix, w_in_bf, gv, gq, gk, hsum)


def _mixer_kernel(x_ref, u_ref, vn_ref, q_ref, kp_ref, vp_ref, bias_ref, wsp_ref, bsp_ref,
                  goa_ref, gob_ref, wout_ref, gt1_ref, gffn_ref, sh2_ref, sc2_ref, *rest,
                  sb, nsb, kw, padded, nj, nmain):
    x1_ref, h2_ref, merged_ref = rest[-3:]
    i = pl.program_id(0)
    if len(rest) == 5:
        tx1_ref, th2_ref = rest[:2]

        @pl.when(i >= nmain)
        def _():
            x1_ref[...] = tx1_ref[...]
            h2_ref[...] = th2_ref[...]

    @pl.when(i < nmain)
    def _():
        _mixer_body(x_ref, u_ref, vn_ref, q_ref, kp_ref, vp_ref, bias_ref, wsp_ref, bsp_ref,
                    goa_ref, gob_ref, wout_ref, gt1_ref, gffn_ref, sh2_ref, sc2_ref,
                    x1_ref, h2_ref, merged_ref, i % nj, sb=sb, nsb=nsb, kw=kw, padded=padded)


def _mixer_body(x_ref, u_ref, vn_ref, q_ref, kp_ref, vp_ref, bias_ref, wsp_ref, bsp_ref,
                goa_ref, gob_ref, wout_ref, gt1_ref, gffn_ref, sh2_ref, sc2_ref,
                x1_ref, h2_ref, merged_ref, j, *, sb, nsb, kw, padded):
    rows = sb * nsb
    ri = lax.broadcasted_iota(jnp.int32, (sb, sb), 0)
    ci = lax.broadcasted_iota(jnp.int32, (sb, sb), 1)
    gmask = (ci // CHUNK) <= (ri // CHUNK)
    lane = lax.broadcasted_iota(jnp.int32, (sb, 2 * HEAD_DIM), 1)
    lo_half = lane < HEAD_DIM
    kidx = lax.broadcasted_iota(jnp.int32, (sb, kw), 1)

    for s in range(nsb):
        r0 = s * sb
        vn = vn_ref[0, r0:r0 + sb, :].astype(BF16)
        u = u_ref[0, r0:r0 + sb, :].astype(F32)
        zs = []
        for g in range(A_GROUPS):
            w = jnp.where(gmask, wsp_ref[g], 0.0).astype(BF16)
            z = jnp.dot(w, vn[:, g * A_GROUP_DIM:(g + 1) * A_GROUP_DIM], preferred_element_type=F32)
            zs.append(z + bsp_ref[g])
        a = u * jnp.concatenate(zs, axis=1)
        a = a * _rms(a) * goa_ref[...]
        merged_ref[r0:r0 + sb, 0:W_A] = a.astype(BF16)

        if padded:
            blk0 = j * rows + r0
            kstart = pl.multiple_of(blk0, sb)
            kb = kp_ref[0, pl.ds(kstart, kw), :]
            vb = vp_ref[0, pl.ds(kstart, kw), :]
            valid = kidx >= (ATT_PAST_WINDOW - blk0)
        else:
            kb = kp_ref[0]
            vb = vp_ref[0]
            valid = None
        qb = q_ref[0, r0:r0 + sb, :]
        outs = []
        for hp in range(B_HEADS // 2):
            c0 = hp * 2 * HEAD_DIM
            qp = qb[:, c0:c0 + 2 * HEAD_DIM]
            kp = kb[:, c0:c0 + 2 * HEAD_DIM]
            vp = vb[:, c0:c0 + 2 * HEAD_DIM]
            o_pair = []
            for hh in range(2):
                qm = jnp.where(lo_half if hh == 0 else jnp.logical_not(lo_half), qp, jnp.zeros_like(qp))
                sc = lax.dot_general(qm, kp, (((1,), (1,)), ((), ())), preferred_element_type=F32)
                sc = sc + bias_ref[2 * hp + hh]
                if valid is not None:
                    sc = jnp.where(valid, sc, NEG_INF)
                m = jnp.max(sc, axis=-1, keepdims=True)
                p = jnp.exp(sc - m)
                l = jnp.sum(p, axis=-1, keepdims=True)
                o = jnp.dot(p.astype(BF16), vp, preferred_element_type=F32)
                o_pair.append(o / l)
            outs.append(jnp.where(lo_half, o_pair[0], o_pair[1]))
        bo = jnp.concatenate(outs, axis=1)
        bo = bo * _rms(bo) * gob_ref[...]
        merged_ref[r0:r0 + sb, W_A:W_A + W_B] = bo.astype(BF16)

    mix = jnp.dot(merged_ref[...], wout_ref[...], preferred_element_type=F32)
    x1 = x_ref[0] + gt1_ref[0] * mix
    x1_ref[...] = x1
    h2 = x1 * _rms(x1) * gffn_ref[...]
    h2_ref[...] = h2 * (1.0 + sc2_ref[0]) + sh2_ref[0]


def _mixer(x, u, vn, q, kp, vp, bias, wsp, bsp, goa, gob, wout_bf, gt1, gffn, sh2, sc2,
           *, sb, nsb, padded, tail=None):
    b, l, d = x.shape
    rows = sb * nsb
    assert l % rows == 0
    nj = l // rows
    nmain = b * nj
    ntail = 0
    if tail is not None:
        assert tail[0].shape[0] % rows == 0
        ntail = tail[0].shape[0] // rows
    total_rows = (nmain + ntail) * rows
    kw = bias.shape[2]
    kl = kp.shape[1]
    bidx = lambda i: jnp.minimum(i // nj, b - 1)
    row = lambda i: (bidx(i), i % nj, 0)
    bvec = lambda i: (bidx(i), 0, 0)
    c2 = lambda i: (0, 0)
    c3 = lambda i: (0, 0, 0)
    pool = lambda i: (i, 0)
    in_specs = [pl.BlockSpec((1, rows, d), row),
                pl.BlockSpec((1, rows, W_A), row),
                pl.BlockSpec((1, rows, W_A), row),
                pl.BlockSpec((1, rows, W_B), row),
                pl.BlockSpec((1, kl, W_B), bvec),
                pl.BlockSpec((1, kl, W_B), bvec),
                pl.BlockSpec(bias.shape, c3),
                pl.BlockSpec(wsp.shape, c3),
                pl.BlockSpec(bsp.shape, c3),
                pl.BlockSpec((1, W_A), c2),
                pl.BlockSpec((1, W_B), c2),
                pl.BlockSpec((d, d), c2),
                pl.BlockSpec((1, 1, d), bvec),
                pl.BlockSpec((1, d), c2),
                pl.BlockSpec((1, 1, d), bvec),
                pl.BlockSpec((1, 1, d), bvec)]
    args = [x, u, vn, q, kp, vp, bias, wsp, bsp, goa, gob, wout_bf, gt1, gffn, sh2, sc2]
    if tail is not None:
        tmap = lambda i: (jnp.maximum(i - nmain, 0), 0)
        in_specs += [pl.BlockSpec((rows, d), tmap), pl.BlockSpec((rows, d), tmap)]
        args += list(tail)
    return pl.pallas_call(
        functools.partial(_mixer_kernel, sb=sb, nsb=nsb, kw=kw, padded=padded, nj=nj, nmain=nmain),
        grid=(nmain + ntail,),
        in_specs=in_specs,
        out_specs=(pl.BlockSpec((rows, d), pool), pl.BlockSpec((rows, d), pool)),
        out_shape=(jax.ShapeDtypeStruct((total_rows, d), F32), jax.ShapeDtypeStruct((total_rows, d), F32)),
        scratch_shapes=[pltpu.VMEM((rows, d), BF16)],
        compiler_params=_cparams(("arbitrary",)),
        name="mixer",
    )(*args)


def _route_kernel(h2_ref, wr_ref, br_ref, pk_ref, gw_ref, cnt_ref, run_ref, *, tm):
    i = pl.program_id(0)

    @pl.when(i == 0)
    def _():
        run_ref[...] = jnp.zeros_like(run_ref)

    logits = jnp.dot(h2_ref[...].astype(BF16), wr_ref[...], preferred_element_type=F32)
    aff = jax.nn.sigmoid(logits)
    sel = aff + br_ref[...]
    lane = lax.broadcasted_iota(jnp.int32, (tm, N_EXPERTS), 1)
    out_lane = lax.broadcasted_iota(jnp.int32, (tm, 128), 1)
    idx = jnp.zeros((tm, 128), jnp.int32)
    onehot = jnp.zeros((tm, N_EXPERTS), F32)
    for k in range(TOP_K):
        m = jnp.max(sel, axis=-1, keepdims=True)
        ik = jnp.min(jnp.where(sel == m, lane, N_EXPERTS), axis=-1, keepdims=True)
        oh = lane == ik
        sel = jnp.where(oh, -jnp.inf, sel)
        onehot = jnp.where(oh, 1.0, onehot)
        idx = jnp.where(out_lane == k, ik, idx)
    ri = lax.broadcasted_iota(jnp.int32, (tm, tm), 0)
    ci = lax.broadcasted_iota(jnp.int32, (tm, tm), 1)
    tri = (ci < ri).astype(BF16)
    before = jnp.dot(tri, onehot.astype(BF16), preferred_element_type=F32) + run_ref[...]

    half = N_EXPERTS // 2
    in_hi = idx >= half
    sub = idx & (half - 1)

    def pick(tab):
        return jnp.where(in_hi, jnp.take_along_axis(tab[:, half:], sub, axis=1),
                         jnp.take_along_axis(tab[:, :half], sub, axis=1))

    chosen = out_lane < TOP_K
    a_sel = jnp.where(chosen, pick(aff), 0.0)
    tot = jnp.sum(a_sel, axis=-1, keepdims=True)
    gw_ref[...] = a_sel / tot * ROUTED_SCALE
    rank = pick(before).astype(jnp.int32)
    pk_ref[...] = jnp.where(chosen, idx * (1 << RANK_BITS) + rank, 0)
    run = run_ref[...] + jnp.sum(onehot, axis=0, keepdims=True)
    run_ref[...] = run
    cnt_ref[...] = run.astype(jnp.int32)


def _route(h2, wr_bf, b_router, *, tm):
    t, d = h2.shape
    return pl.pallas_call(
        functools.partial(_route_kernel, tm=tm),
        grid=(t // tm,),
        in_specs=[pl.BlockSpec((tm, d), lambda i: (i, 0)),
                  pl.BlockSpec((d, N_EXPERTS), lambda i: (0, 0)),
                  pl.BlockSpec((1, N_EXPERTS), lambda i: (0, 0))],
        out_specs=(pl.BlockSpec((tm, 128), lambda i: (i, 0)),
                   pl.BlockSpec((tm, 128), lambda i: (i, 0)),
                   pl.BlockSpec((1, N_EXPERTS), lambda i: (0, 0))),
        out_shape=(jax.ShapeDtypeStruct((t, 128), jnp.int32),
                   jax.ShapeDtypeStruct((t, 128), F32),
                   jax.ShapeDtypeStruct((1, N_EXPERTS), jnp.int32)),
        scratch_shapes=[pltpu.VMEM((1, N_EXPERTS), F32)],
        compiler_params=_cparams(("arbitrary",)),
        name="route",
    )(h2, wr_bf, b_router.reshape(1, N_EXPERTS))


def _idx_copy(idx_hbm, idx_smem, sem, step, n, half):
    return pltpu.make_async_copy(idx_hbm.at[pl.ds(pl.multiple_of(step * n, n), n)],
                                 idx_smem.at[pl.ds(half * n, n)], sem.at[half])


def _slots_kernel(pk_ref, offs_ref, dst_ref):
    p = pk_ref[...]
    e = p >> RANK_BITS
    half = N_EXPERTS // 2
    lo = jnp.take_along_axis(jnp.broadcast_to(offs_ref[:, 0:half], p.shape), e & (half - 1), axis=1)
    hi = jnp.take_along_axis(jnp.broadcast_to(offs_ref[:, half:N_EXPERTS], p.shape), e & (half - 1), axis=1)
    dst_ref[...] = (p & ((1 << RANK_BITS) - 1)) + jnp.where(e >= half, hi, lo)


def _slots(pk, offs, *, tm):
    t = pk.shape[0]
    return pl.pallas_call(
        _slots_kernel,
        grid=(t // tm,),
        in_specs=[pl.BlockSpec((tm, 128), lambda i: (i, 0)),
                  pl.BlockSpec((1, N_EXPERTS), lambda i: (0, 0))],
        out_specs=pl.BlockSpec((tm, 128), lambda i: (i, 0)),
        out_shape=jax.ShapeDtypeStruct((t, 128), jnp.int32),
        compiler_params=_cparams(("arbitrary",)),
        name="slots",
    )(pk, offs[:N_EXPERTS].reshape(1, N_EXPERTS))


def _dispatch_kernel(h2_ref, dst_hbm, xs_hbm, idx_smem, idx_sem, row_sem, *, tm):
    i = pl.program_id(0)
    n = tm * TOP_K

    @pl.when(i == 0)
    def _():
        _idx_copy(dst_hbm, idx_smem, idx_sem, i, n, 0).start()

    def scatter(half):
        _idx_copy(dst_hbm, idx_smem, idx_sem, i, n, half).wait()

        @pl.when(i + 1 < pl.num_programs(0))
        def _():
            _idx_copy(dst_hbm, idx_smem, idx_sem, i + 1, n, 1 - half).start()

        def body(t8, carry):
            t0 = pl.multiple_of(t8 * 8, 8)
            b = half * n + t0 * TOP_K
            for tt in range(8):
                src = h2_ref.at[pl.ds(t0 + tt, 1)]
                for k in range(TOP_K):
                    pltpu.make_async_copy(src, xs_hbm.at[pl.ds(idx_smem[b + tt * TOP_K + k], 1)],
                                          row_sem).start(priority=k % 2)
            return carry

        lax.fori_loop(0, tm // 8, body, 0)

    for half in range(2):
        pl.when(i % 2 == half)(functools.partial(scatter, half))
    pltpu.make_async_copy(xs_hbm.at[pl.ds(0, n)], xs_hbm.at[pl.ds(0, n)], row_sem).wait()


def _dispatch(h2, dst_flat, *, tm):
    t, d = h2.shape
    n = tm * TOP_K
    return pl.pallas_call(
        functools.partial(_dispatch_kernel, tm=tm),
        grid=(t // tm,),
        in_specs=[pl.BlockSpec((tm, d), lambda i: (i, 0)),
                  pl.BlockSpec(memory_space=pl.ANY)],
        out_specs=pl.BlockSpec(memory_space=pl.ANY),
        scratch_shapes=[pltpu.SMEM((2 * n,), jnp.int32),
                        pltpu.SemaphoreType.DMA((2,)),
                        pltpu.SemaphoreType.DMA(())],
        out_shape=jax.ShapeDtypeStruct((t * TOP_K, d), F32),
        compiler_params=_cparams(("arbitrary",)),
        name="dispatch",
    )(h2, dst_flat)


def _experts_kernel(vt_ref, vg_ref, offs_ref, nv_ref, xs_ref, wg_ref, wu_ref, wd_ref, ys_ref,
                    wgb_ref, wub_ref, wdb_ref, *, tm):
    v = pl.program_id(0)

    @pl.when(v < nv_ref[0])
    def _():
        g = vg_ref[v]
        tile = vt_ref[v]
        prev = jnp.maximum(v - 1, 0)
        first = jnp.logical_or(v == 0, vt_ref[prev] != tile)

        @pl.when(jnp.logical_or(v == 0, vg_ref[prev] != g))
        def _():
            wgb_ref[...] = wg_ref[0].astype(BF16)
            wub_ref[...] = wu_ref[0].astype(BF16)
            wdb_ref[...] = wd_ref[0].astype(BF16)

        rowid = tile * tm + lax.broadcasted_iota(jnp.int32, (tm, 1), 0)
        mine = jnp.logical_and(rowid >= offs_ref[g], rowid < offs_ref[g + 1])
        x = xs_ref[...].astype(BF16)
        hg = jnp.dot(x, wgb_ref[...], preferred_element_type=F32)
        hu = jnp.dot(x, wub_ref[...], preferred_element_type=F32)
        hm = (hg * jax.nn.sigmoid(hg) * hu).astype(BF16)
        y = jnp.dot(hm, wdb_ref[...], preferred_element_type=F32)

        @pl.when(first)
        def _():
            ys_ref[...] = jnp.where(mine, y, 0.0)

        @pl.when(jnp.logical_not(first))
        def _():
            ys_ref[...] = jnp.where(mine, y, ys_ref[...])


def _experts(vt, vg, offs, nv, xs, w_gate, w_up, w_down, *, tm):
    m, d = xs.shape
    nvis = vt.shape[0]
    de = w_gate.shape[2]
    xmap = lambda v, vt, vg, offs, nv: (vt[v], 0)
    wmap = lambda v, vt, vg, offs, nv: (vg[v], 0, 0)
    grid_spec = pltpu.PrefetchScalarGridSpec(
        num_scalar_prefetch=4,
        grid=(nvis,),
        in_specs=[pl.BlockSpec((tm, d), xmap),
                  pl.BlockSpec((1, d, de), wmap),
                  pl.BlockSpec((1, d, de), wmap),
                  pl.BlockSpec((1, de, d), wmap)],
        out_specs=pl.BlockSpec((tm, d), xmap),
        scratch_shapes=[pltpu.VMEM((d, de), BF16), pltpu.VMEM((d, de), BF16), pltpu.VMEM((de, d), BF16)],
    )
    return pl.pallas_call(
        functools.partial(_experts_kernel, tm=tm),
        grid_spec=grid_spec,
        out_shape=jax.ShapeDtypeStruct((m, d), F32),
        compiler_params=_cparams(("arbitrary",)),
        name="experts",
    )(vt, vg, offs, nv, xs, w_gate, w_up, w_down)


def _combine_kernel(x1_ref, h2_ref, gw_ref, gt2_ref, wsg_ref, wsu_ref, wsd_ref, dst_hbm, ys_hbm,
                    y_ref, idx_smem, buf_ref, idx_sem, row_sem, *, tm, step0, bpt):
    i = pl.program_id(0)
    nsteps = pl.num_programs(0)
    n = tm * TOP_K

    def gather(step, half):
        _idx_copy(dst_hbm, idx_smem, idx_sem, step0 + step, n, half).wait()

        def body(t8, carry):
            t0 = pl.multiple_of(t8 * 8, 8)
            b = half * n + t0 * TOP_K
            for tt in range(8):
                for k in range(TOP_K):
                    pltpu.make_async_copy(ys_hbm.at[pl.ds(idx_smem[b + tt * TOP_K + k], 1)],
                                          buf_ref.at[half, k, pl.ds(t0 + tt, 1)],
                                          row_sem.at[half]).start(priority=k % 2)
            return carry

        lax.fori_loop(0, tm // 8, body, 0)

        @pl.when(step + 1 < nsteps)
        def _():
            _idx_copy(dst_hbm, idx_smem, idx_sem, step0 + step + 1, n, 1 - half).start()

    @pl.when(i == 0)
    def _():
        _idx_copy(dst_hbm, idx_smem, idx_sem, step0, n, 0).start()
        gather(i, 0)

    for half in range(2):
        pl.when(jnp.logical_and(i + 1 < nsteps, (i + 1) % 2 == half))(functools.partial(gather, i + 1, half))

    h2 = h2_ref[...].astype(BF16)
    sg = jnp.dot(h2, wsg_ref[...], preferred_element_type=F32)
    su = jnp.dot(h2, wsu_ref[...], preferred_element_type=F32)
    acc = jnp.dot((sg * jax.nn.sigmoid(sg) * su).astype(BF16), wsd_ref[...], preferred_element_type=F32)

    cur = i % 2
    pltpu.make_async_copy(ys_hbm.at[pl.ds(0, n)], ys_hbm.at[pl.ds(0, n)], row_sem.at[cur]).wait()
    gw = gw_ref[...]
    routed = gw[:, 0:1] * buf_ref[cur, 0]
    for k in range(1, TOP_K):
        routed = routed + gw[:, k:k + 1] * buf_ref[cur, k]
    ffn = routed + acc
    x1 = x1_ref[...]
    seq = tm // bpt
    for b in range(bpt):
        r = slice(b * seq, (b + 1) * seq)
        y_ref[b] = x1[r] + gt2_ref[b] * ffn[r]


def _combine(x1, h2, gw, gt2, wsg, wsu, wsd, dst_flat, ys, *, tm, batch, seq, row_offset):
    d = x1.shape[1]
    n = tm * TOP_K
    bpt = max(tm // seq, 1)
    per_b = max(seq // tm, 1)
    ob = tm // bpt
    step0 = row_offset // tm
    tok = lambda i: (step0 + i, 0)
    c2 = lambda i: (0, 0)
    return pl.pallas_call(
        functools.partial(_combine_kernel, tm=tm, step0=step0, bpt=bpt),
        grid=(batch * seq // tm,),
        in_specs=[pl.BlockSpec((tm, d), tok),
                  pl.BlockSpec((tm, d), tok),
                  pl.BlockSpec((tm, 128), tok),
                  pl.BlockSpec((bpt, 1, d), lambda i: (i // per_b, 0, 0)),
                  pl.BlockSpec(wsg.shape, c2),
                  pl.BlockSpec(wsu.shape, c2),
                  pl.BlockSpec(wsd.shape, c2),
                  pl.BlockSpec(memory_space=pl.ANY),
                  pl.BlockSpec(memory_space=pl.ANY)],
        out_specs=pl.BlockSpec((bpt, ob, d), lambda i: (i // per_b, i % per_b, 0)),
        scratch_shapes=[pltpu.SMEM((2 * n,), jnp.int32),
                        pltpu.VMEM((2, TOP_K, tm, d), F32),
                        pltpu.SemaphoreType.DMA((2,)),
                        pltpu.SemaphoreType.DMA((2,))],
        out_shape=jax.ShapeDtypeStruct((batch, seq, d), F32),
        compiler_params=_cparams(("arbitrary",)),
        name="combine",
    )(x1, h2, gw, gt2, wsg, wsu, wsd, dst_flat, ys)


def _bias_table(rel_bias, sb, kw):
    h, n_rel = rel_bias.shape
    lo = (kw - 1) - ATT_PAST_WINDOW - (CHUNK - 1)
    ntot = sb + kw - 1
    s = jnp.concatenate([jnp.broadcast_to(rel_bias[:, :1], (h, lo)), rel_bias,
                         jnp.broadcast_to(rel_bias[:, -1:], (h, ntot - lo - n_rel))], axis=1).astype(F32)
    s_rev = jnp.pad(s[:, ::-1], ((0, 0), (0, 1)))
    skew = jnp.broadcast_to(s_rev[:, None, :], (h, sb, ntot + 1)).reshape(h, -1)[:, :sb * ntot]
    toep = skew.reshape(h, sb, ntot)[:, :, sb - 1:sb - 1 + kw]
    r = np.arange(sb)[:, None]
    kk = np.arange(kw)[None, :]
    qc, kc = r // CHUNK, kk // CHUNK
    band = (kc >= qc) & (kc <= qc + LEFT_CHUNKS)
    return jnp.where(band[None], toep, NEG_INF)


def _visit_plan(offs, m, tm, nvis):
    cnt = offs[1:] - offs[:-1]
    first = offs[:-1] // tm
    last = jnp.maximum(offs[1:] - 1, 0) // tm
    per = jnp.where(cnt > 0, last - first + 1, 0)
    vend = jnp.cumsum(per)
    vstart = vend - per
    total = vend[-1]
    v = jnp.arange(nvis, dtype=jnp.int32)
    g = jnp.minimum(jnp.sum(vend[None, :] <= v[:, None], axis=1), N_EXPERTS - 1).astype(jnp.int32)
    tile = (first[g] + v - vstart[g]).astype(jnp.int32)
    live = v < total
    last_tile = m // tm - 1
    tile = jnp.where(live, tile, last_tile)
    g = jnp.where(live, g, N_EXPERTS - 1)
    return tile, g, total.reshape(1).astype(jnp.int32)


def kernel(x_prompt, x_sample, c_prompt, c_sample, cache_k, cache_v, w_ada, b_ada, g_mix, w_in, g_gmlp_v,
           w_spatial, b_spatial, g_q, g_k, rel_bias, g_out_a, g_out_b, w_out, g_ffn, w_router, b_router,
           w_exp_gate, w_exp_up, w_exp_down, w_sh_gate, w_sh_up, w_sh_down):
    depth = w_ada.shape[0]
    assert depth == 1, "single layer"
    bp, sp, d = x_prompt.shape
    bs, ss, _ = x_sample.shape
    tp, ts = bp * sp, bs * ss
    t_all = tp + ts
    l = 0

    c_all = jnp.concatenate([c_prompt, c_sample], axis=0)
    nb = c_all.shape[0]
    nb_pad = -(-nb // 8) * 8
    c_all = jnp.pad(c_all, ((0, nb_pad - nb), (0, 0)))
    mod = _ada(c_all, w_ada[l], b_ada[l])[:nb].reshape(nb, 6, 1, d)
    sh1, sc1, gt1, sh2, sc2, gt2 = [mod[:, i] for i in range(6)]

    w_in_bf = w_in[l].astype(BF16)
    w_out_bf = w_out[l].astype(BF16)
    gv = g_gmlp_v[l].reshape(1, W_A)
    gq = jnp.tile(g_q[l], B_HEADS).reshape(1, W_B)
    gk = jnp.tile(g_k[l], B_HEADS).reshape(1, W_B)
    hid = np.arange(W_B) // HEAD_DIM
    hsum = jnp.asarray(hid[:, None] == hid[None, :], BF16)
    gmix = g_mix[l].reshape(1, d)
    goa = g_out_a[l].reshape(1, W_A)
    gob = g_out_b[l].reshape(1, W_B)
    gffn = g_ffn[l].reshape(1, d)
    bias_p = _bias_table(rel_bias[l], GMLP_BLOCK, ATT_PAST_WINDOW + GMLP_BLOCK)
    bsp_p = jnp.broadcast_to(b_spatial[l][:, :, None], (A_GROUPS, GMLP_BLOCK, A_GROUP_DIM))

    cl = cache_k.shape[2]
    us, vns, qs, ks_bf, vs_bf, k32s, v32s = _in_proj(
        x_sample, sh1[bp:], sc1[bp:], gmix, w_in_bf, gv, gq, gk, hsum,
        tm=ss, pad_rows=0, keep=ss, vn_dtype=F32)
    kcat = jnp.concatenate([cache_k[l].reshape(bs, cl, W_B).astype(BF16), ks_bf], axis=1)
    vcat = jnp.concatenate([cache_v[l].reshape(bs, cl, W_B).astype(BF16), vs_bf], axis=1)
    bias_s = bias_p[:, :ss, ATT_PAST_WINDOW - cl:ATT_PAST_WINDOW + ss]
    x1_s, h2_s = _mixer(x_sample, us, vns, qs, kcat, vcat, bias_s, w_spatial[l][:, :ss, :ss], bsp_p[:, :ss],
                        goa, gob, w_out_bf, gt1[bp:], gffn, sh2[bp:], sc2[bp:],
                        sb=ss, nsb=1, padded=False)
    new_k_sample = k32s.reshape(1, bs, ss, B_HEADS, HEAD_DIM)
    new_v_sample = v32s.reshape(1, bs, ss, B_HEADS, HEAD_DIM)
    new_gmlp_v_sample = vns.reshape(1, bs, ss, A_GROUPS, A_GROUP_DIM)

    keep = min(ATT_PAST_WINDOW, sp)
    u, vn, q, kpad, vpad, k32, v32 = _in_proj(
        x_prompt, sh1[:bp], sc1[:bp], gmix, w_in_bf, gv, gq, gk, hsum,
        tm=512, pad_rows=ATT_PAST_WINDOW, keep=keep, vn_dtype=BF16)
    x1, h2 = _mixer(x_prompt, u, vn, q, kpad, vpad, bias_p, w_spatial[l], bsp_p, goa, gob, w_out_bf,
                    gt1[:bp], gffn, sh2[:bp], sc2[:bp],
                    sb=GMLP_BLOCK, nsb=2, padded=True, tail=(x1_s, h2_s))
    assert x1.shape[0] == t_all
    new_k_prompt = k32.reshape(1, bp, keep, B_HEADS, HEAD_DIM)
    new_v_prompt = v32.reshape(1, bp, keep, B_HEADS, HEAD_DIM)

    tmr = 256
    tme = 512
    assert t_all % tmr == 0 and (t_all * TOP_K) % tme == 0 and sp % 256 == 0 and ts % (2 * ss) == 0
    assert t_all * TOP_K < (1 << RANK_BITS)
    pk, gw, cnt = _route(h2, w_router[l].astype(BF16), b_router[l], tm=tmr)
    offs = jnp.concatenate([jnp.zeros((1,), jnp.int32), jnp.cumsum(cnt[0]).astype(jnp.int32)])
    dst_flat = _slots(pk, offs, tm=tmr)[:, :TOP_K].reshape(-1)
    xs = _dispatch(h2, dst_flat, tm=tmr)
    m = t_all * TOP_K
    nvis = m // tme + N_EXPERTS - 1
    vt, vg, nv = _visit_plan(offs, m, tme, nvis)
    ys = _experts(vt, vg, offs, nv, xs, w_exp_gate[l], w_exp_up[l], w_exp_down[l], tm=tme)

    wsg, wsu, wsd = w_sh_gate[l].astype(BF16), w_sh_up[l].astype(BF16), w_sh_down[l].astype(BF16)
    y_prompt = _combine(x1, h2, gw, gt2[:bp], wsg, wsu, wsd, dst_flat, ys,
                        tm=256, batch=bp, seq=sp, row_offset=0)
    y_sample = _combine(x1, h2, gw, gt2[bp:], wsg, wsu, wsd, dst_flat, ys,
                        tm=2 * ss, batch=bs, seq=ss, row_offset=tp)
    return (y_prompt, y_sample, new_k_prompt, new_v_prompt, new_k_sample, new_v_sample, new_gmlp_v_sample)
```

```python
import functools

import jax
import jax.numpy as jnp
import numpy as np
from jax import lax
from jax.experimental import pallas as pl
from jax.experimental.pallas import tpu as pltpu

F32 = jnp.float32
BF16 = jnp.bfloat16

D_MODEL = 1024
CHUNK = 64
GMLP_BLOCK = 128
W_A = 512
A_GROUPS = 4
A_GROUP_DIM = 128
W_B = 512
B_HEADS = 8
HEAD_DIM = 64
LEFT_CHUNKS = 8
ATT_PAST_WINDOW = LEFT_CHUNKS * CHUNK
REL_CLIP = 128
D_IN = 2 * W_A + 3 * W_B
N_EXPERTS = 256
TOP_K = 8
D_EXPERT = 256
ROUTED_SCALE = 2.5
EPS = 1e-6
NEG_INF = -1e30

RANK_BITS = 20
VMEM_LIMIT = 56 * 1024 * 1024


def _cparams(sem, vmem=VMEM_LIMIT):
    return pltpu.CompilerParams(dimension_semantics=sem, vmem_limit_bytes=vmem)


def _rms(x, axis=-1):
    return lax.rsqrt(jnp.mean(x * x, axis=axis, keepdims=True) + EPS)


LANES = 128
NCH = D_MODEL // LANES


def _load_rows(ref, n, lead=()):
    return jnp.concatenate([ref[(*lead, pl.ds(c, n, stride=NCH), slice(None))] for c in range(NCH)], axis=1)


def _store_rows(ref, val):
    n = val.shape[0]
    for c in range(NCH):
        ref[pl.ds(c, n, stride=NCH), :] = val[:, c * LANES:(c + 1) * LANES]


def _ada_kernel(c_ref, w_ref, b_ref, o_ref):
    c = c_ref[...]
    s = c * jax.nn.sigmoid(c)
    o_ref[...] = jnp.dot(s.astype(BF16), w_ref[...].astype(BF16), preferred_element_type=F32) + b_ref[...]


def _ada(c, w_ada, b_ada):
    n, d = c.shape
    nout = w_ada.shape[1]
    tn = 1024
    return pl.pallas_call(
        _ada_kernel,
        grid=(nout // tn,),
        in_specs=[pl.BlockSpec((n, d), lambda j: (0, 0)),
                  pl.BlockSpec((d, tn), lambda j: (0, j)),
                  pl.BlockSpec((1, tn), lambda j: (0, j))],
        out_specs=pl.BlockSpec((n, tn), lambda j: (0, j)),
        out_shape=jax.ShapeDtypeStruct((n, nout), F32),
        compiler_params=_cparams(("arbitrary",)),
        name="ada",
    )(c, w_ada, b_ada.reshape(1, nout))


def _in_proj_kernel(x_ref, sh_ref, sc_ref, gmix_ref, win_ref, gv_ref, gq_ref, gk_ref, hsum_ref,
                    u_ref, vn_ref, q_ref, kp_ref, vp_ref, k32_ref, v32_ref, *, npad, first_kept):
    i = pl.program_id(1)

    @pl.when(i < npad)
    def _():
        kp_ref[...] = jnp.zeros_like(kp_ref)
        vp_ref[...] = jnp.zeros_like(vp_ref)

    @pl.when(i >= npad)
    def _():
        x = x_ref[0]
        h = x * _rms(x) * gmix_ref[...]
        h = h * (1.0 + sc_ref[0]) + sh_ref[0]
        proj = jnp.dot(h.astype(BF16), win_ref[...], preferred_element_type=F32)
        u = jax.nn.gelu(proj[:, 0:W_A])
        u_ref[0] = u.astype(u_ref.dtype)
        v = jax.nn.gelu(proj[:, W_A:2 * W_A])
        for g in range(A_GROUPS):
            vg = v[:, g * A_GROUP_DIM:(g + 1) * A_GROUP_DIM]
            vn = vg * _rms(vg) * gv_ref[:, g * A_GROUP_DIM:(g + 1) * A_GROUP_DIM]
            vn_ref[0, :, g * A_GROUP_DIM:(g + 1) * A_GROUP_DIM] = vn.astype(vn_ref.dtype)
        q = proj[:, 2 * W_A:2 * W_A + W_B]
        k = proj[:, 2 * W_A + W_B:2 * W_A + 2 * W_B]
        va = proj[:, 2 * W_A + 2 * W_B:]
        q2 = jnp.dot((q * q).astype(BF16), hsum_ref[...], preferred_element_type=F32) * (1.0 / HEAD_DIM)
        k2 = jnp.dot((k * k).astype(BF16), hsum_ref[...], preferred_element_type=F32) * (1.0 / HEAD_DIM)
        qn = q * lax.rsqrt(q2 + EPS) * gq_ref[...]
        kn = k * lax.rsqrt(k2 + EPS) * gk_ref[...]
        q_ref[0] = (qn * (HEAD_DIM ** -0.5)).astype(q_ref.dtype)
        kp_ref[0] = kn.astype(kp_ref.dtype)
        vp_ref[0] = va.astype(vp_ref.dtype)

        @pl.when(i >= first_kept)
        def _():
            k32_ref[0] = kn
            v32_ref[0] = va


def _in_proj(x, sh1, sc1, g_mix, w_in_bf, gv, gq, gk, hsum, *, tm, pad_rows, keep, vn_dtype):
    b, l, d = x.shape
    assert l % tm == 0 and pad_rows % tm == 0 and keep % tm == 0
    npad = pad_rows // tm
    nt = l // tm
    first_kept = npad + nt - keep // tm
    row = lambda bi, i: (bi, jnp.maximum(i - npad, 0), 0)
    kept = lambda bi, i: (bi, jnp.maximum(i - first_kept, 0), 0)
    bvec = lambda bi, i: (bi, 0, 0)
    const2 = lambda bi, i: (0, 0)
    out_shapes = (
        jax.ShapeDtypeStruct((b, l, W_A), BF16),
        jax.ShapeDtypeStruct((b, l, W_A), vn_dtype),
        jax.ShapeDtypeStruct((b, l, W_B), BF16),
        jax.ShapeDtypeStruct((b, pad_rows + l, W_B), BF16),
        jax.ShapeDtypeStruct((b, pad_rows + l, W_B), BF16),
        jax.ShapeDtypeStruct((b, keep, W_B), F32),
        jax.ShapeDtypeStruct((b, keep, W_B), F32),
    )
    padrow = lambda bi, i: (bi, i, 0)
    return pl.pallas_call(
        functools.partial(_in_proj_kernel, npad=npad, first_kept=first_kept),
        grid=(b, nt + npad),
        in_specs=[pl.BlockSpec((1, tm, d), row),
                  pl.BlockSpec((1, 1, d), bvec),
                  pl.BlockSpec((1, 1, d), bvec),
                  pl.BlockSpec((1, d), const2),
                  pl.BlockSpec((d, D_IN), const2),
                  pl.BlockSpec((1, W_A), const2),
                  pl.BlockSpec((1, W_B), const2),
                  pl.BlockSpec((1, W_B), const2),
                  pl.BlockSpec((W_B, W_B), const2)],
        out_specs=(pl.BlockSpec((1, tm, W_A), row),
                   pl.BlockSpec((1, tm, W_A), row),
                   pl.BlockSpec((1, tm, W_B), row),
                   pl.BlockSpec((1, tm, W_B), padrow),
                   pl.BlockSpec((1, tm, W_B), padrow),
                   pl.BlockSpec((1, tm, W_B), kept),
                   pl.BlockSpec((1, tm, W_B), kept)),
        out_shape=out_shapes,
        compiler_params=_cparams(("arbitrary", "arbitrary")),
        name="in_proj",
    )(x, sh1, sc1, g_mix, w_in_bf, gv, gq, gk, hsum)


def _mixer_kernel(x_ref, u_ref, vn_ref, q_ref, kp_ref, vp_ref, bias_ref, wsp_ref, bsp_ref,
                  goa_ref, gob_ref, wout_ref, gt1_ref, gffn_ref, sh2_ref, sc2_ref, *rest,
                  sb, nsb, kw, padded, nj, nmain):
    x1_ref, h2_ref, merged_ref = rest[-3:]
    i = pl.program_id(0)
    if len(rest) == 5:
        tx1_ref, th2_ref = rest[:2]

        @pl.when(i >= nmain)
        def _():
            x1_ref[...] = tx1_ref[...]
            h2_ref[...] = th2_ref[...]

    @pl.when(i < nmain)
    def _():
        _mixer_body(x_ref, u_ref, vn_ref, q_ref, kp_ref, vp_ref, bias_ref, wsp_ref, bsp_ref,
                    goa_ref, gob_ref, wout_ref, gt1_ref, gffn_ref, sh2_ref, sc2_ref,
                    x1_ref, h2_ref, merged_ref, i % nj, sb=sb, nsb=nsb, kw=kw, padded=padded)


def _mixer_body(x_ref, u_ref, vn_ref, q_ref, kp_ref, vp_ref, bias_ref, wsp_ref, bsp_ref,
                goa_ref, gob_ref, wout_ref, gt1_ref, gffn_ref, sh2_ref, sc2_ref,
                x1_ref, h2_ref, merged_ref, j, *, sb, nsb, kw, padded):
    rows = sb * nsb
    ri = lax.broadcasted_iota(jnp.int32, (sb, sb), 0)
    ci = lax.broadcasted_iota(jnp.int32, (sb, sb), 1)
    gmask = (ci // CHUNK) <= (ri // CHUNK)
    lane = lax.broadcasted_iota(jnp.int32, (sb, 2 * HEAD_DIM), 1)
    lo_half = lane < HEAD_DIM
    kidx = lax.broadcasted_iota(jnp.int32, (sb, kw), 1)

    for s in range(nsb):
        r0 = s * sb
        vn = vn_ref[0, r0:r0 + sb, :].astype(BF16)
        u = u_ref[0, r0:r0 + sb, :].astype(F32)
        zs = []
        for g in range(A_GROUPS):
            w = jnp.where(gmask, wsp_ref[g], 0.0).astype(BF16)
            z = jnp.dot(w, vn[:, g * A_GROUP_DIM:(g + 1) * A_GROUP_DIM], preferred_element_type=F32)
            zs.append(z + bsp_ref[g])
        a = u * jnp.concatenate(zs, axis=1)
        a = a * _rms(a) * goa_ref[...]
        merged_ref[r0:r0 + sb, 0:W_A] = a.astype(BF16)

        if padded:
            blk0 = j * rows + r0
            kstart = pl.multiple_of(blk0, sb)
            kb = kp_ref[0, pl.ds(kstart, kw), :]
            vb = vp_ref[0, pl.ds(kstart, kw), :]
            valid = kidx >= (ATT_PAST_WINDOW - blk0)
        else:
            kb = kp_ref[0]
            vb = vp_ref[0]
            valid = None
        qb = q_ref[0, r0:r0 + sb, :]
        outs = []
        for hp in range(B_HEADS // 2):
            c0 = hp * 2 * HEAD_DIM
            qp = qb[:, c0:c0 + 2 * HEAD_DIM]
            kp = kb[:, c0:c0 + 2 * HEAD_DIM]
            vp = vb[:, c0:c0 + 2 * HEAD_DIM]
            o_pair = []
            for hh in range(2):
                qm = jnp.where(lo_half if hh == 0 else jnp.logical_not(lo_half), qp, jnp.zeros_like(qp))
                sc = lax.dot_general(qm, kp, (((1,), (1,)), ((), ())), preferred_element_type=F32)
                sc = sc + bias_ref[2 * hp + hh]
                if valid is not None:
                    sc = jnp.where(valid, sc, NEG_INF)
                m = jnp.max(sc, axis=-1, keepdims=True)
                p = jnp.exp(sc - m)
                l = jnp.sum(p, axis=-1, keepdims=True)
                o = jnp.dot(p.astype(BF16), vp, preferred_element_type=F32)
                o_pair.append(o / l)
            outs.append(jnp.where(lo_half, o_pair[0], o_pair[1]))
        bo = jnp.concatenate(outs, axis=1)
        bo = bo * _rms(bo) * gob_ref[...]
        merged_ref[r0:r0 + sb, W_A:W_A + W_B] = bo.astype(BF16)

    mix = jnp.dot(merged_ref[...], wout_ref[...], preferred_element_type=F32)
    x1 = x_ref[0] + gt1_ref[0] * mix
    x1_ref[...] = x1
    h2 = x1 * _rms(x1) * gffn_ref[...]
    _store_rows(h2_ref, h2 * (1.0 + sc2_ref[0]) + sh2_ref[0])


def _mixer(x, u, vn, q, kp, vp, bias, wsp, bsp, goa, gob, wout_bf, gt1, gffn, sh2, sc2,
           *, sb, nsb, padded, tail=None):
    b, l, d = x.shape
    rows = sb * nsb
    assert l % rows == 0
    nj = l // rows
    nmain = b * nj
    ntail = 0
    if tail is not None:
        assert tail[0].shape[0] % rows == 0
        ntail = tail[0].shape[0] // rows
    total_rows = (nmain + ntail) * rows
    kw = bias.shape[2]
    kl = kp.shape[1]
    bidx = lambda i: jnp.minimum(i // nj, b - 1)
    row = lambda i: (bidx(i), i % nj, 0)
    bvec = lambda i: (bidx(i), 0, 0)
    c2 = lambda i: (0, 0)
    c3 = lambda i: (0, 0, 0)
    pool = lambda i: (i, 0)
    in_specs = [pl.BlockSpec((1, rows, d), row),
                pl.BlockSpec((1, rows, W_A), row),
                pl.BlockSpec((1, rows, W_A), row),
                pl.BlockSpec((1, rows, W_B), row),
                pl.BlockSpec((1, kl, W_B), bvec),
                pl.BlockSpec((1, kl, W_B), bvec),
                pl.BlockSpec(bias.shape, c3),
                pl.BlockSpec(wsp.shape, c3),
                pl.BlockSpec(bsp.shape, c3),
                pl.BlockSpec((1, W_A), c2),
                pl.BlockSpec((1, W_B), c2),
                pl.BlockSpec((d, d), c2),
                pl.BlockSpec((1, 1, d), bvec),
                pl.BlockSpec((1, d), c2),
                pl.BlockSpec((1, 1, d), bvec),
                pl.BlockSpec((1, 1, d), bvec)]
    args = [x, u, vn, q, kp, vp, bias, wsp, bsp, goa, gob, wout_bf, gt1, gffn, sh2, sc2]
    if tail is not None:
        tmap = lambda i: (jnp.maximum(i - nmain, 0), 0)
        in_specs += [pl.BlockSpec((rows, d), tmap), pl.BlockSpec((rows * NCH, LANES), tmap)]
        args += list(tail)
    return pl.pallas_call(
        functools.partial(_mixer_kernel, sb=sb, nsb=nsb, kw=kw, padded=padded, nj=nj, nmain=nmain),
        grid=(nmain + ntail,),
        in_specs=in_specs,
        out_specs=(pl.BlockSpec((rows, d), pool), pl.BlockSpec((rows * NCH, LANES), pool)),
        out_shape=(jax.ShapeDtypeStruct((total_rows, d), F32),
                   jax.ShapeDtypeStruct((total_rows * NCH, LANES), F32)),
        scratch_shapes=[pltpu.VMEM((rows, d), BF16)],
        compiler_params=_cparams(("arbitrary",)),
        name="mixer",
    )(*args)


def _route_kernel(h2_ref, wr_ref, br_ref, pk_ref, gw_ref, cnt_ref, run_ref, *, tm):
    i = pl.program_id(0)

    @pl.when(i == 0)
    def _():
        run_ref[...] = jnp.zeros_like(run_ref)

    logits = jnp.dot(_load_rows(h2_ref, tm).astype(BF16), wr_ref[...], preferred_element_type=F32)
    aff = jax.nn.sigmoid(logits)
    sel = aff + br_ref[...]
    lane = lax.broadcasted_iota(jnp.int32, (tm, N_EXPERTS), 1)
    out_lane = lax.broadcasted_iota(jnp.int32, (tm, 128), 1)
    idx = jnp.zeros((tm, 128), jnp.int32)
    onehot = jnp.zeros((tm, N_EXPERTS), F32)
    for k in range(TOP_K):
        m = jnp.max(sel, axis=-1, keepdims=True)
        ik = jnp.min(jnp.where(sel == m, lane, N_EXPERTS), axis=-1, keepdims=True)
        oh = lane == ik
        sel = jnp.where(oh, -jnp.inf, sel)
        onehot = jnp.where(oh, 1.0, onehot)
        idx = jnp.where(out_lane == k, ik, idx)
    ri = lax.broadcasted_iota(jnp.int32, (tm, tm), 0)
    ci = lax.broadcasted_iota(jnp.int32, (tm, tm), 1)
    tri = (ci < ri).astype(BF16)
    before = jnp.dot(tri, onehot.astype(BF16), preferred_element_type=F32) + run_ref[...]

    half = N_EXPERTS // 2
    in_hi = idx >= half
    sub = idx & (half - 1)

    def pick(tab):
        return jnp.where(in_hi, jnp.take_along_axis(tab[:, half:], sub, axis=1),
                         jnp.take_along_axis(tab[:, :half], sub, axis=1))

    chosen = out_lane < TOP_K
    a_sel = jnp.where(chosen, pick(aff), 0.0)
    tot = jnp.sum(a_sel, axis=-1, keepdims=True)
    gw_ref[...] = a_sel / tot * ROUTED_SCALE
    rank = pick(before).astype(jnp.int32)
    pk_ref[...] = jnp.where(chosen, idx * (1 << RANK_BITS) + rank, 0)
    run = run_ref[...] + jnp.sum(onehot, axis=0, keepdims=True)
    run_ref[...] = run
    cnt_ref[...] = run.astype(jnp.int32)


def _route(h2, wr_bf, b_router, *, tm):
    t, d = h2.shape[0] // NCH, D_MODEL
    return pl.pallas_call(
        functools.partial(_route_kernel, tm=tm),
        grid=(t // tm,),
        in_specs=[pl.BlockSpec((tm * NCH, LANES), lambda i: (i, 0)),
                  pl.BlockSpec((d, N_EXPERTS), lambda i: (0, 0)),
                  pl.BlockSpec((1, N_EXPERTS), lambda i: (0, 0))],
        out_specs=(pl.BlockSpec((tm, 128), lambda i: (i, 0)),
                   pl.BlockSpec((tm, 128), lambda i: (i, 0)),
                   pl.BlockSpec((1, N_EXPERTS), lambda i: (0, 0))),
        out_shape=(jax.ShapeDtypeStruct((t, 128), jnp.int32),
                   jax.ShapeDtypeStruct((t, 128), F32),
                   jax.ShapeDtypeStruct((1, N_EXPERTS), jnp.int32)),
        scratch_shapes=[pltpu.VMEM((1, N_EXPERTS), F32)],
        compiler_params=_cparams(("arbitrary",)),
        name="route",
    )(h2, wr_bf, b_router.reshape(1, N_EXPERTS))


def _idx_copy(idx_hbm, idx_smem, sem, step, n, half):
    return pltpu.make_async_copy(idx_hbm.at[pl.ds(pl.multiple_of(step * n, n), n)],
                                 idx_smem.at[pl.ds(half * n, n)], sem.at[half])


def _slots_kernel(pk_ref, offs_ref, dst_ref):
    p = pk_ref[...]
    e = p >> RANK_BITS
    half = N_EXPERTS // 2
    lo = jnp.take_along_axis(jnp.broadcast_to(offs_ref[:, 0:half], p.shape), e & (half - 1), axis=1)
    hi = jnp.take_along_axis(jnp.broadcast_to(offs_ref[:, half:N_EXPERTS], p.shape), e & (half - 1), axis=1)
    dst_ref[...] = ((p & ((1 << RANK_BITS) - 1)) + jnp.where(e >= half, hi, lo)) * NCH


def _slots(pk, offs, *, tm):
    t = pk.shape[0]
    return pl.pallas_call(
        _slots_kernel,
        grid=(t // tm,),
        in_specs=[pl.BlockSpec((tm, 128), lambda i: (i, 0)),
                  pl.BlockSpec((1, N_EXPERTS), lambda i: (0, 0))],
        out_specs=pl.BlockSpec((tm, 128), lambda i: (i, 0)),
        out_shape=jax.ShapeDtypeStruct((t, 128), jnp.int32),
        compiler_params=_cparams(("arbitrary",)),
        name="slots",
    )(pk, offs[:N_EXPERTS].reshape(1, N_EXPERTS))


def _dispatch_kernel(h2_ref, dst_hbm, xs_hbm, idx_smem, idx_sem, row_sem, *, tm):
    i = pl.program_id(0)
    n = tm * TOP_K

    @pl.when(i == 0)
    def _():
        _idx_copy(dst_hbm, idx_smem, idx_sem, i, n, 0).start()

    def scatter(half):
        _idx_copy(dst_hbm, idx_smem, idx_sem, i, n, half).wait()

        @pl.when(i + 1 < pl.num_programs(0))
        def _():
            _idx_copy(dst_hbm, idx_smem, idx_sem, i + 1, n, 1 - half).start()

        def body(t8, carry):
            r0 = pl.multiple_of(t8 * (8 * NCH), 8 * NCH)
            b = half * n + t8 * (8 * TOP_K)
            for tt in range(8):
                src = h2_ref.at[pl.ds(r0 + tt * NCH, NCH)]
                for k in range(TOP_K):
                    dst = pl.multiple_of(idx_smem[b + tt * TOP_K + k], NCH)
                    pltpu.make_async_copy(src, xs_hbm.at[pl.ds(dst, NCH)], row_sem).start(priority=k % 2)
            return carry

        lax.fori_loop(0, tm // 8, body, 0)

    for half in range(2):
        pl.when(i % 2 == half)(functools.partial(scatter, half))
    pltpu.make_async_copy(xs_hbm.at[pl.ds(0, n * NCH)], xs_hbm.at[pl.ds(0, n * NCH)], row_sem).wait()


def _dispatch(h2, dst_flat, *, tm):
    t = h2.shape[0] // NCH
    n = tm * TOP_K
    return pl.pallas_call(
        functools.partial(_dispatch_kernel, tm=tm),
        grid=(t // tm,),
        in_specs=[pl.BlockSpec((tm * NCH, LANES), lambda i: (i, 0)),
                  pl.BlockSpec(memory_space=pl.ANY)],
        out_specs=pl.BlockSpec(memory_space=pl.ANY),
        scratch_shapes=[pltpu.SMEM((2 * n,), jnp.int32),
                        pltpu.SemaphoreType.DMA((2,)),
                        pltpu.SemaphoreType.DMA(())],
        out_shape=jax.ShapeDtypeStruct((t * TOP_K * NCH, LANES), F32),
        compiler_params=_cparams(("arbitrary",)),
        name="dispatch",
    )(h2, dst_flat)


def _experts_kernel(vt_ref, vg_ref, offs_ref, nv_ref, xs_ref, wg_ref, wu_ref, wd_ref, ys_ref,
                    wgb_ref, wub_ref, wdb_ref, *, tm):
    v = pl.program_id(0)

    @pl.when(v < nv_ref[0])
    def _():
        g = vg_ref[v]
        tile = vt_ref[v]
        prev = jnp.maximum(v - 1, 0)
        first = jnp.logical_or(v == 0, vt_ref[prev] != tile)

        @pl.when(jnp.logical_or(v == 0, vg_ref[prev] != g))
        def _():
            wgb_ref[...] = wg_ref[0].astype(BF16)
            wub_ref[...] = wu_ref[0].astype(BF16)
            wdb_ref[...] = wd_ref[0].astype(BF16)

        rowid = tile * tm + lax.broadcasted_iota(jnp.int32, (tm, 1), 0)
        mine = jnp.logical_and(rowid >= offs_ref[g], rowid < offs_ref[g + 1])
        x = _load_rows(xs_ref, tm).astype(BF16)
        hg = jnp.dot(x, wgb_ref[...], preferred_element_type=F32)
        hu = jnp.dot(x, wub_ref[...], preferred_element_type=F32)
        hm = (hg * jax.nn.sigmoid(hg) * hu).astype(BF16)
        y = jnp.dot(hm, wdb_ref[...], preferred_element_type=F32)

        @pl.when(first)
        def _():
            _store_rows(ys_ref, jnp.where(mine, y, 0.0))

        @pl.when(jnp.logical_not(first))
        def _():
            _store_rows(ys_ref, jnp.where(mine, y, _load_rows(ys_ref, tm)))


def _experts(vt, vg, offs, nv, xs, w_gate, w_up, w_down, *, tm):
    m, d = xs.shape[0] // NCH, D_MODEL
    nvis = vt.shape[0]
    de = w_gate.shape[2]
    xmap = lambda v, vt, vg, offs, nv: (vt[v], 0)
    wmap = lambda v, vt, vg, offs, nv: (vg[v], 0, 0)
    grid_spec = pltpu.PrefetchScalarGridSpec(
        num_scalar_prefetch=4,
        grid=(nvis,),
        in_specs=[pl.BlockSpec((tm * NCH, LANES), xmap),
                  pl.BlockSpec((1, d, de), wmap),
                  pl.BlockSpec((1, d, de), wmap),
                  pl.BlockSpec((1, de, d), wmap)],
        out_specs=pl.BlockSpec((tm * NCH, LANES), xmap),
        scratch_shapes=[pltpu.VMEM((d, de), BF16), pltpu.VMEM((d, de), BF16), pltpu.VMEM((de, d), BF16)],
    )
    return pl.pallas_call(
        functools.partial(_experts_kernel, tm=tm),
        grid_spec=grid_spec,
        out_shape=jax.ShapeDtypeStruct((m * NCH, LANES), F32),
        compiler_params=_cparams(("arbitrary",)),
        name="experts",
    )(vt, vg, offs, nv, xs, w_gate, w_up, w_down)


def _combine_kernel(x1_ref, h2_ref, gw_ref, gt2_ref, wsg_ref, wsu_ref, wsd_ref, dst_hbm, ys_hbm,
                    y_ref, idx_smem, buf_ref, idx_sem, row_sem, *, tm, step0, bpt):
    i = pl.program_id(0)
    nsteps = pl.num_programs(0)
    n = tm * TOP_K

    def gather(step, half):
        _idx_copy(dst_hbm, idx_smem, idx_sem, step0 + step, n, half).wait()

        def body(t8, carry):
            r0 = pl.multiple_of(t8 * (8 * NCH), 8 * NCH)
            b = half * n + t8 * (8 * TOP_K)
            for tt in range(8):
                for k in range(TOP_K):
                    src = pl.multiple_of(idx_smem[b + tt * TOP_K + k], NCH)
                    pltpu.make_async_copy(ys_hbm.at[pl.ds(src, NCH)],
                                          buf_ref.at[half, k, pl.ds(r0 + tt * NCH, NCH)],
                                          row_sem.at[half]).start(priority=k % 2)
            return carry

        lax.fori_loop(0, tm // 8, body, 0)

        @pl.when(step + 1 < nsteps)
        def _():
            _idx_copy(dst_hbm, idx_smem, idx_sem, step0 + step + 1, n, 1 - half).start()

    @pl.when(i == 0)
    def _():
        _idx_copy(dst_hbm, idx_smem, idx_sem, step0, n, 0).start()
        gather(i, 0)

    for half in range(2):
        pl.when(jnp.logical_and(i + 1 < nsteps, (i + 1) % 2 == half))(functools.partial(gather, i + 1, half))

    h2 = _load_rows(h2_ref, tm).astype(BF16)
    sg = jnp.dot(h2, wsg_ref[...], preferred_element_type=F32)
    su = jnp.dot(h2, wsu_ref[...], preferred_element_type=F32)
    acc = jnp.dot((sg * jax.nn.sigmoid(sg) * su).astype(BF16), wsd_ref[...], preferred_element_type=F32)

    def finish(cur):
        pltpu.make_async_copy(ys_hbm.at[pl.ds(0, n * NCH)], ys_hbm.at[pl.ds(0, n * NCH)], row_sem.at[cur]).wait()
        gw = gw_ref[...]
        routed = gw[:, 0:1] * _load_rows(buf_ref, tm, (cur, 0))
        for k in range(1, TOP_K):
            routed = routed + gw[:, k:k + 1] * _load_rows(buf_ref, tm, (cur, k))
        ffn = routed + acc
        x1 = x1_ref[...]
        seq = tm // bpt
        for b in range(bpt):
            r = slice(b * seq, (b + 1) * seq)
            y_ref[b] = x1[r] + gt2_ref[b] * ffn[r]

    for cur in range(2):
        pl.when(i % 2 == cur)(functools.partial(finish, cur))


def _combine(x1, h2, gw, gt2, wsg, wsu, wsd, dst_flat, ys, *, tm, batch, seq, row_offset):
    d = x1.shape[1]
    n = tm * TOP_K
    bpt = max(tm // seq, 1)
    per_b = max(seq // tm, 1)
    ob = tm // bpt
    step0 = row_offset // tm
    tok = lambda i: (step0 + i, 0)
    c2 = lambda i: (0, 0)
    return pl.pallas_call(
        functools.partial(_combine_kernel, tm=tm, step0=step0, bpt=bpt),
        grid=(batch * seq // tm,),
        in_specs=[pl.BlockSpec((tm, d), tok),
                  pl.BlockSpec((tm * NCH, LANES), tok),
                  pl.BlockSpec((tm, 128), tok),
                  pl.BlockSpec((bpt, 1, d), lambda i: (i // per_b, 0, 0)),
                  pl.BlockSpec(wsg.shape, c2),
                  pl.BlockSpec(wsu.shape, c2),
                  pl.BlockSpec(wsd.shape, c2),
                  pl.BlockSpec(memory_space=pl.ANY),
                  pl.BlockSpec(memory_space=pl.ANY)],
        out_specs=pl.BlockSpec((bpt, ob, d), lambda i: (i // per_b, i % per_b, 0)),
        scratch_shapes=[pltpu.SMEM((2 * n,), jnp.int32),
                        pltpu.VMEM((2, TOP_K, tm * NCH, LANES), F32),
                        pltpu.SemaphoreType.DMA((2,)),
                        pltpu.SemaphoreType.DMA((2,))],
        out_shape=jax.ShapeDtypeStruct((batch, seq, d), F32),
        compiler_params=_cparams(("arbitrary",)),
        name="combine",
    )(x1, h2, gw, gt2, wsg, wsu, wsd, dst_flat, ys)


def _bias_table(rel_bias, sb, kw):
    h, n_rel = rel_bias.shape
    lo = (kw - 1) - ATT_PAST_WINDOW - (CHUNK - 1)
    ntot = sb + kw - 1
    s = jnp.concatenate([jnp.broadcast_to(rel_bias[:, :1], (h, lo)), rel_bias,
                         jnp.broadcast_to(rel_bias[:, -1:], (h, ntot - lo - n_rel))], axis=1).astype(F32)
    s_rev = jnp.pad(s[:, ::-1], ((0, 0), (0, 1)))
    skew = jnp.broadcast_to(s_rev[:, None, :], (h, sb, ntot + 1)).reshape(h, -1)[:, :sb * ntot]
    toep = skew.reshape(h, sb, ntot)[:, :, sb - 1:sb - 1 + kw]
    r = np.arange(sb)[:, None]
    kk = np.arange(kw)[None, :]
    qc, kc = r // CHUNK, kk // CHUNK
    band = (kc >= qc) & (kc <= qc + LEFT_CHUNKS)
    return jnp.where(band[None], toep, NEG_INF)


def _visit_plan(offs, m, tm, nvis):
    cnt = offs[1:] - offs[:-1]
    first = offs[:-1] // tm
    last = jnp.maximum(offs[1:] - 1, 0) // tm
    per = jnp.where(cnt > 0, last - first + 1, 0)
    vend = jnp.cumsum(per)
    vstart = vend - per
    total = vend[-1]
    v = jnp.arange(nvis, dtype=jnp.int32)
    g = jnp.minimum(jnp.sum(vend[None, :] <= v[:, None], axis=1), N_EXPERTS - 1).astype(jnp.int32)
    tile = (first[g] + v - vstart[g]).astype(jnp.int32)
    live = v < total
    last_tile = m // tm - 1
    tile = jnp.where(live, tile, last_tile)
    g = jnp.where(live, g, N_EXPERTS - 1)
    return tile, g, total.reshape(1).astype(jnp.int32)


def kernel(x_prompt, x_sample, c_prompt, c_sample, cache_k, cache_v, w_ada, b_ada, g_mix, w_in, g_gmlp_v,
           w_spatial, b_spatial, g_q, g_k, rel_bias, g_out_a, g_out_b, w_out, g_ffn, w_router, b_router,
           w_exp_gate, w_exp_up, w_exp_down, w_sh_gate, w_sh_up, w_sh_down):
    depth = w_ada.shape[0]
    assert depth == 1, "single layer"
    bp, sp, d = x_prompt.shape
    bs, ss, _ = x_sample.shape
    tp, ts = bp * sp, bs * ss
    t_all = tp + ts
    l = 0

    c_all = jnp.concatenate([c_prompt, c_sample], axis=0)
    nb = c_all.shape[0]
    nb_pad = -(-nb // 8) * 8
    c_all = jnp.pad(c_all, ((0, nb_pad - nb), (0, 0)))
    mod = _ada(c_all, w_ada[l], b_ada[l])[:nb].reshape(nb, 6, 1, d)
    sh1, sc1, gt1, sh2, sc2, gt2 = [mod[:, i] for i in range(6)]

    w_in_bf = w_in[l].astype(BF16)
    w_out_bf = w_out[l].astype(BF16)
    gv = g_gmlp_v[l].reshape(1, W_A)
    gq = jnp.tile(g_q[l], B_HEADS).reshape(1, W_B)
    gk = jnp.tile(g_k[l], B_HEADS).reshape(1, W_B)
    hid = np.arange(W_B) // HEAD_DIM
    hsum = jnp.asarray(hid[:, None] == hid[None, :], BF16)
    gmix = g_mix[l].reshape(1, d)
    goa = g_out_a[l].reshape(1, W_A)
    gob = g_out_b[l].reshape(1, W_B)
    gffn = g_ffn[l].reshape(1, d)
    bias_p = _bias_table(rel_bias[l], GMLP_BLOCK, ATT_PAST_WINDOW + GMLP_BLOCK)
    bsp_p = jnp.broadcast_to(b_spatial[l][:, :, None], (A_GROUPS, GMLP_BLOCK, A_GROUP_DIM))

    cl = cache_k.shape[2]
    us, vns, qs, ks_bf, vs_bf, k32s, v32s = _in_proj(
        x_sample, sh1[bp:], sc1[bp:], gmix, w_in_bf, gv, gq, gk, hsum,
        tm=ss, pad_rows=0, keep=ss, vn_dtype=F32)
    kcat = jnp.concatenate([cache_k[l].reshape(bs, cl, W_B).astype(BF16), ks_bf], axis=1)
    vcat = jnp.concatenate([cache_v[l].reshape(bs, cl, W_B).astype(BF16), vs_bf], axis=1)
    bias_s = bias_p[:, :ss, ATT_PAST_WINDOW - cl:ATT_PAST_WINDOW + ss]
    x1_s, h2_s = _mixer(x_sample, us, vns, qs, kcat, vcat, bias_s, w_spatial[l][:, :ss, :ss], bsp_p[:, :ss],
                        goa, gob, w_out_bf, gt1[bp:], gffn, sh2[bp:], sc2[bp:],
                        sb=ss, nsb=1, padded=False)
    new_k_sample = k32s.reshape(1, bs, ss, B_HEADS, HEAD_DIM)
    new_v_sample = v32s.reshape(1, bs, ss, B_HEADS, HEAD_DIM)
    new_gmlp_v_sample = vns.reshape(1, bs, ss, A_GROUPS, A_GROUP_DIM)

    keep = min(ATT_PAST_WINDOW, sp)
    u, vn, q, kpad, vpad, k32, v32 = _in_proj(
        x_prompt, sh1[:bp], sc1[:bp], gmix, w_in_bf, gv, gq, gk, hsum,
        tm=512, pad_rows=ATT_PAST_WINDOW, keep=keep, vn_dtype=BF16)
    x1, h2 = _mixer(x_prompt, u, vn, q, kpad, vpad, bias_p, w_spatial[l], bsp_p, goa, gob, w_out_bf,
                    gt1[:bp], gffn, sh2[:bp], sc2[:bp],
                    sb=GMLP_BLOCK, nsb=2, padded=True, tail=(x1_s, h2_s))
    assert x1.shape[0] == t_all
    new_k_prompt = k32.reshape(1, bp, keep, B_HEADS, HEAD_DIM)
    new_v_prompt = v32.reshape(1, bp, keep, B_HEADS, HEAD_DIM)

    tmr = 256
    tme = 512
    assert t_all % tmr == 0 and (t_all * TOP_K) % tme == 0 and sp % 256 == 0 and ts % (2 * ss) == 0
    assert t_all * TOP_K < (1 << RANK_BITS)
    pk, gw, cnt = _route(h2, w_router[l].astype(BF16), b_router[l], tm=tmr)
    offs = jnp.concatenate([jnp.zeros((1,), jnp.int32), jnp.cumsum(cnt[0]).astype(jnp.int32)])
    dst_flat = _slots(pk, offs, tm=tmr)[:, :TOP_K].reshape(-1)
    xs = _dispatch(h2, dst_flat, tm=tmr)
    m = t_all * TOP_K
    nvis = m // tme + N_EXPERTS - 1
    vt, vg, nv = _visit_plan(offs, m, tme, nvis)
    ys = _experts(vt, vg, offs, nv, xs, w_exp_gate[l], w_exp_up[l], w_exp_down[l], tm=tme)

    wsg, wsu, wsd = w_sh_gate[l].astype(BF16), w_sh_up[l].astype(BF16), w_sh_down[l].astype(BF16)
    y_prompt = _combine(x1, h2, gw, gt2[:bp], wsg, wsu, wsd, dst_flat, ys,
                        tm=256, batch=bp, seq=sp, row_offset=0)
    y_sample = _combine(x1, h2, gw, gt2[bp:], wsg, wsu, wsd, dst_flat, ys,
                        tm=2 * ss, batch=bs, seq=ss, row_offset=tp)
    return (y_prompt, y_sample, new_k_prompt, new_v_prompt, new_k_sample, new_v_sample, new_gmlp_v_sample)
```

```python
import functools

import jax
import jax.numpy as jnp
import numpy as np
from jax import lax
from jax.experimental import pallas as pl
from jax.experimental.pallas import tpu as pltpu

F32 = jnp.float32
BF16 = jnp.bfloat16

D_MODEL = 1024
CHUNK = 64
GMLP_BLOCK = 128
W_A = 512
A_GROUPS = 4
A_GROUP_DIM = 128
W_B = 512
B_HEADS = 8
HEAD_DIM = 64
LEFT_CHUNKS = 8
ATT_PAST_WINDOW = LEFT_CHUNKS * CHUNK
REL_CLIP = 128
D_IN = 2 * W_A + 3 * W_B
N_EXPERTS = 256
TOP_K = 8
D_EXPERT = 256
ROUTED_SCALE = 2.5
EPS = 1e-6
NEG_INF = -1e30

RANK_BITS = 20
VMEM_LIMIT = 56 * 1024 * 1024


def _cparams(sem, vmem=VMEM_LIMIT):
    return pltpu.CompilerParams(dimension_semantics=sem, vmem_limit_bytes=vmem)


def _rms(x, axis=-1):
    return lax.rsqrt(jnp.mean(x * x, axis=axis, keepdims=True) + EPS)


LANES = 128
NCH = D_MODEL // LANES


def _load_planes(ref, lead=()):
    return jnp.concatenate([ref[(*lead, c)] for c in range(NCH)], axis=1)


def _store_planes(ref, val):
    for c in range(NCH):
        ref[c] = val[:, c * LANES:(c + 1) * LANES]


def _load_rows(ref, n, lead=()):
    return jnp.concatenate([ref[(*lead, pl.ds(c, n, stride=NCH), slice(None))] for c in range(NCH)], axis=1)


def _store_rows(ref, val):
    n = val.shape[0]
    for c in range(NCH):
        ref[pl.ds(c, n, stride=NCH), :] = val[:, c * LANES:(c + 1) * LANES]


def _ada_kernel(c_ref, w_ref, b_ref, o_ref):
    c = c_ref[...]
    s = c * jax.nn.sigmoid(c)
    o_ref[...] = jnp.dot(s.astype(BF16), w_ref[...].astype(BF16), preferred_element_type=F32) + b_ref[...]


def _ada(c, w_ada, b_ada):
    n, d = c.shape
    nout = w_ada.shape[1]
    tn = 1024
    return pl.pallas_call(
        _ada_kernel,
        grid=(nout // tn,),
        in_specs=[pl.BlockSpec((n, d), lambda j: (0, 0)),
                  pl.BlockSpec((d, tn), lambda j: (0, j)),
                  pl.BlockSpec((1, tn), lambda j: (0, j))],
        out_specs=pl.BlockSpec((n, tn), lambda j: (0, j)),
        out_shape=jax.ShapeDtypeStruct((n, nout), F32),
        compiler_params=_cparams(("arbitrary",)),
        name="ada",
    )(c, w_ada, b_ada.reshape(1, nout))


def _in_proj_kernel(x_ref, sh_ref, sc_ref, gmix_ref, win_ref, gv_ref, gq_ref, gk_ref, hsum_ref,
                    u_ref, vn_ref, q_ref, kp_ref, vp_ref, k32_ref, v32_ref, *, npad, first_kept):
    i = pl.program_id(1)

    @pl.when(i < npad)
    def _():
        kp_ref[...] = jnp.zeros_like(kp_ref)
        vp_ref[...] = jnp.zeros_like(vp_ref)

    @pl.when(i >= npad)
    def _():
        x = x_ref[0]
        h = x * _rms(x) * gmix_ref[...]
        h = h * (1.0 + sc_ref[0]) + sh_ref[0]
        proj = jnp.dot(h.astype(BF16), win_ref[...], preferred_element_type=F32)
        u = jax.nn.gelu(proj[:, 0:W_A])
        u_ref[0] = u.astype(u_ref.dtype)
        v = jax.nn.gelu(proj[:, W_A:2 * W_A])
        for g in range(A_GROUPS):
            vg = v[:, g * A_GROUP_DIM:(g + 1) * A_GROUP_DIM]
            vn = vg * _rms(vg) * gv_ref[:, g * A_GROUP_DIM:(g + 1) * A_GROUP_DIM]
            vn_ref[0, :, g * A_GROUP_DIM:(g + 1) * A_GROUP_DIM] = vn.astype(vn_ref.dtype)
        q = proj[:, 2 * W_A:2 * W_A + W_B]
        k = proj[:, 2 * W_A + W_B:2 * W_A + 2 * W_B]
        va = proj[:, 2 * W_A + 2 * W_B:]
        q2 = jnp.dot((q * q).astype(BF16), hsum_ref[...], preferred_element_type=F32) * (1.0 / HEAD_DIM)
        k2 = jnp.dot((k * k).astype(BF16), hsum_ref[...], preferred_element_type=F32) * (1.0 / HEAD_DIM)
        qn = q * lax.rsqrt(q2 + EPS) * gq_ref[...]
        kn = k * lax.rsqrt(k2 + EPS) * gk_ref[...]
        q_ref[0] = (qn * (HEAD_DIM ** -0.5)).astype(q_ref.dtype)
        kp_ref[0] = kn.astype(kp_ref.dtype)
        vp_ref[0] = va.astype(vp_ref.dtype)

        @pl.when(i >= first_kept)
        def _():
            k32_ref[0] = kn
            v32_ref[0] = va


def _in_proj(x, sh1, sc1, g_mix, w_in_bf, gv, gq, gk, hsum, *, tm, pad_rows, keep, vn_dtype):
    b, l, d = x.shape
    assert l % tm == 0 and pad_rows % tm == 0 and keep % tm == 0
    npad = pad_rows // tm
    nt = l // tm
    first_kept = npad + nt - keep // tm
    row = lambda bi, i: (bi, jnp.maximum(i - npad, 0), 0)
    kept = lambda bi, i: (bi, jnp.maximum(i - first_kept, 0), 0)
    bvec = lambda bi, i: (bi, 0, 0)
    const2 = lambda bi, i: (0, 0)
    out_shapes = (
        jax.ShapeDtypeStruct((b, l, W_A), BF16),
        jax.ShapeDtypeStruct((b, l, W_A), vn_dtype),
        jax.ShapeDtypeStruct((b, l, W_B), BF16),
        jax.ShapeDtypeStruct((b, pad_rows + l, W_B), BF16),
        jax.ShapeDtypeStruct((b, pad_rows + l, W_B), BF16),
        jax.ShapeDtypeStruct((b, keep, W_B), F32),
        jax.ShapeDtypeStruct((b, keep, W_B), F32),
    )
    padrow = lambda bi, i: (bi, i, 0)
    return pl.pallas_call(
        functools.partial(_in_proj_kernel, npad=npad, first_kept=first_kept),
        grid=(b, nt + npad),
        in_specs=[pl.BlockSpec((1, tm, d), row),
                  pl.BlockSpec((1, 1, d), bvec),
                  pl.BlockSpec((1, 1, d), bvec),
                  pl.BlockSpec((1, d), const2),
                  pl.BlockSpec((d, D_IN), const2),
                  pl.BlockSpec((1, W_A), const2),
                  pl.BlockSpec((1, W_B), const2),
                  pl.BlockSpec((1, W_B), const2),
                  pl.BlockSpec((W_B, W_B), const2)],
        out_specs=(pl.BlockSpec((1, tm, W_A), row),
                   pl.BlockSpec((1, tm, W_A), row),
                   pl.BlockSpec((1, tm, W_B), row),
                   pl.BlockSpec((1, tm, W_B), padrow),
                   pl.BlockSpec((1, tm, W_B), padrow),
                   pl.BlockSpec((1, tm, W_B), kept),
                   pl.BlockSpec((1, tm, W_B), kept)),
        out_shape=out_shapes,
        compiler_params=_cparams(("arbitrary", "arbitrary")),
        name="in_proj",
    )(x, sh1, sc1, g_mix, w_in_bf, gv, gq, gk, hsum)


def _mixer_kernel(x_ref, u_ref, vn_ref, q_ref, kp_ref, vp_ref, bias_ref, wsp_ref, bsp_ref,
                  goa_ref, gob_ref, wout_ref, gt1_ref, gffn_ref, sh2_ref, sc2_ref, *rest,
                  sb, nsb, kw, padded, nj, nmain):
    x1_ref, h2_ref, merged_ref = rest[-3:]
    i = pl.program_id(0)
    if len(rest) == 5:
        tx1_ref, th2_ref = rest[:2]

        @pl.when(i >= nmain)
        def _():
            x1_ref[...] = tx1_ref[...]
            h2_ref[...] = th2_ref[...]

    @pl.when(i < nmain)
    def _():
        _mixer_body(x_ref, u_ref, vn_ref, q_ref, kp_ref, vp_ref, bias_ref, wsp_ref, bsp_ref,
                    goa_ref, gob_ref, wout_ref, gt1_ref, gffn_ref, sh2_ref, sc2_ref,
                    x1_ref, h2_ref, merged_ref, i % nj, sb=sb, nsb=nsb, kw=kw, padded=padded)


def _mixer_body(x_ref, u_ref, vn_ref, q_ref, kp_ref, vp_ref, bias_ref, wsp_ref, bsp_ref,
                goa_ref, gob_ref, wout_ref, gt1_ref, gffn_ref, sh2_ref, sc2_ref,
                x1_ref, h2_ref, merged_ref, j, *, sb, nsb, kw, padded):
    rows = sb * nsb
    ri = lax.broadcasted_iota(jnp.int32, (sb, sb), 0)
    ci = lax.broadcasted_iota(jnp.int32, (sb, sb), 1)
    gmask = (ci // CHUNK) <= (ri // CHUNK)
    lane = lax.broadcasted_iota(jnp.int32, (sb, 2 * HEAD_DIM), 1)
    lo_half = lane < HEAD_DIM
    kidx = lax.broadcasted_iota(jnp.int32, (sb, kw), 1)

    for s in range(nsb):
        r0 = s * sb
        vn = vn_ref[0, r0:r0 + sb, :].astype(BF16)
        u = u_ref[0, r0:r0 + sb, :].astype(F32)
        zs = []
        for g in range(A_GROUPS):
            w = jnp.where(gmask, wsp_ref[g], 0.0).astype(BF16)
            z = jnp.dot(w, vn[:, g * A_GROUP_DIM:(g + 1) * A_GROUP_DIM], preferred_element_type=F32)
            zs.append(z + bsp_ref[g])
        a = u * jnp.concatenate(zs, axis=1)
        a = a * _rms(a) * goa_ref[...]
        merged_ref[r0:r0 + sb, 0:W_A] = a.astype(BF16)

        if padded:
            blk0 = j * rows + r0
            kstart = pl.multiple_of(blk0, sb)
            kb = kp_ref[0, pl.ds(kstart, kw), :]
            vb = vp_ref[0, pl.ds(kstart, kw), :]
            valid = kidx >= (ATT_PAST_WINDOW - blk0)
        else:
            kb = kp_ref[0]
            vb = vp_ref[0]
            valid = None
        qb = q_ref[0, r0:r0 + sb, :]
        outs = []
        for hp in range(B_HEADS // 2):
            c0 = hp * 2 * HEAD_DIM
            qp = qb[:, c0:c0 + 2 * HEAD_DIM]
            kp = kb[:, c0:c0 + 2 * HEAD_DIM]
            vp = vb[:, c0:c0 + 2 * HEAD_DIM]
            o_pair = []
            for hh in range(2):
                qm = jnp.where(lo_half if hh == 0 else jnp.logical_not(lo_half), qp, jnp.zeros_like(qp))
                sc = lax.dot_general(qm, kp, (((1,), (1,)), ((), ())), preferred_element_type=F32)
                sc = sc + bias_ref[2 * hp + hh]
                if valid is not None:
                    sc = jnp.where(valid, sc, NEG_INF)
                m = jnp.max(sc, axis=-1, keepdims=True)
                p = jnp.exp(sc - m)
                l = jnp.sum(p, axis=-1, keepdims=True)
                o = jnp.dot(p.astype(BF16), vp, preferred_element_type=F32)
                o_pair.append(o / l)
            outs.append(jnp.where(lo_half, o_pair[0], o_pair[1]))
        bo = jnp.concatenate(outs, axis=1)
        bo = bo * _rms(bo) * gob_ref[...]
        merged_ref[r0:r0 + sb, W_A:W_A + W_B] = bo.astype(BF16)

    mix = jnp.dot(merged_ref[...], wout_ref[...], preferred_element_type=F32)
    x1 = x_ref[0] + gt1_ref[0] * mix
    x1_ref[...] = x1
    h2 = x1 * _rms(x1) * gffn_ref[...]
    _store_planes(h2_ref, h2 * (1.0 + sc2_ref[0]) + sh2_ref[0])


def _mixer(x, u, vn, q, kp, vp, bias, wsp, bsp, goa, gob, wout_bf, gt1, gffn, sh2, sc2,
           *, sb, nsb, padded, tail=None):
    b, l, d = x.shape
    rows = sb * nsb
    assert l % rows == 0
    nj = l // rows
    nmain = b * nj
    ntail = 0
    if tail is not None:
        assert tail[0].shape[0] % rows == 0
        ntail = tail[0].shape[0] // rows
    total_rows = (nmain + ntail) * rows
    kw = bias.shape[2]
    kl = kp.shape[1]
    bidx = lambda i: jnp.minimum(i // nj, b - 1)
    row = lambda i: (bidx(i), i % nj, 0)
    bvec = lambda i: (bidx(i), 0, 0)
    c2 = lambda i: (0, 0)
    c3 = lambda i: (0, 0, 0)
    pool = lambda i: (i, 0)
    in_specs = [pl.BlockSpec((1, rows, d), row),
                pl.BlockSpec((1, rows, W_A), row),
                pl.BlockSpec((1, rows, W_A), row),
                pl.BlockSpec((1, rows, W_B), row),
                pl.BlockSpec((1, kl, W_B), bvec),
                pl.BlockSpec((1, kl, W_B), bvec),
                pl.BlockSpec(bias.shape, c3),
                pl.BlockSpec(wsp.shape, c3),
                pl.BlockSpec(bsp.shape, c3),
                pl.BlockSpec((1, W_A), c2),
                pl.BlockSpec((1, W_B), c2),
                pl.BlockSpec((d, d), c2),
                pl.BlockSpec((1, 1, d), bvec),
                pl.BlockSpec((1, d), c2),
                pl.BlockSpec((1, 1, d), bvec),
                pl.BlockSpec((1, 1, d), bvec)]
    args = [x, u, vn, q, kp, vp, bias, wsp, bsp, goa, gob, wout_bf, gt1, gffn, sh2, sc2]
    if tail is not None:
        tmap = lambda i: (jnp.maximum(i - nmain, 0), 0)
        in_specs += [pl.BlockSpec((rows, d), tmap),
                     pl.BlockSpec((NCH, rows, LANES), lambda i: (0, jnp.maximum(i - nmain, 0), 0))]
        args += list(tail)
    return pl.pallas_call(
        functools.partial(_mixer_kernel, sb=sb, nsb=nsb, kw=kw, padded=padded, nj=nj, nmain=nmain),
        grid=(nmain + ntail,),
        in_specs=in_specs,
        out_specs=(pl.BlockSpec((rows, d), pool),
                   pl.BlockSpec((NCH, rows, LANES), lambda i: (0, i, 0))),
        out_shape=(jax.ShapeDtypeStruct((total_rows, d), F32),
                   jax.ShapeDtypeStruct((NCH, total_rows, LANES), F32)),
        scratch_shapes=[pltpu.VMEM((rows, d), BF16)],
        compiler_params=_cparams(("arbitrary",)),
        name="mixer",
    )(*args)


def _route_kernel(h2_ref, wr_ref, br_ref, pk_ref, gw_ref, cnt_ref, run_ref, *, tm):
    i = pl.program_id(0)

    @pl.when(i == 0)
    def _():
        run_ref[...] = jnp.zeros_like(run_ref)

    logits = jnp.dot(_load_planes(h2_ref).astype(BF16), wr_ref[...], preferred_element_type=F32)
    aff = jax.nn.sigmoid(logits)
    sel = aff + br_ref[...]
    lane = lax.broadcasted_iota(jnp.int32, (tm, N_EXPERTS), 1)
    out_lane = lax.broadcasted_iota(jnp.int32, (tm, 128), 1)
    idx = jnp.zeros((tm, 128), jnp.int32)
    onehot = jnp.zeros((tm, N_EXPERTS), F32)
    for k in range(TOP_K):
        m = jnp.max(sel, axis=-1, keepdims=True)
        ik = jnp.min(jnp.where(sel == m, lane, N_EXPERTS), axis=-1, keepdims=True)
        oh = lane == ik
        sel = jnp.where(oh, -jnp.inf, sel)
        onehot = jnp.where(oh, 1.0, onehot)
        idx = jnp.where(out_lane == k, ik, idx)
    ri = lax.broadcasted_iota(jnp.int32, (tm, tm), 0)
    ci = lax.broadcasted_iota(jnp.int32, (tm, tm), 1)
    tri = (ci < ri).astype(BF16)
    before = jnp.dot(tri, onehot.astype(BF16), preferred_element_type=F32) + run_ref[...]

    half = N_EXPERTS // 2
    in_hi = idx >= half
    sub = idx & (half - 1)

    def pick(tab):
        return jnp.where(in_hi, jnp.take_along_axis(tab[:, half:], sub, axis=1),
                         jnp.take_along_axis(tab[:, :half], sub, axis=1))

    chosen = out_lane < TOP_K
    a_sel = jnp.where(chosen, pick(aff), 0.0)
    tot = jnp.sum(a_sel, axis=-1, keepdims=True)
    gw_ref[...] = a_sel / tot * ROUTED_SCALE
    rank = pick(before).astype(jnp.int32)
    pk_ref[...] = jnp.where(chosen, idx * (1 << RANK_BITS) + rank, 0)
    run = run_ref[...] + jnp.sum(onehot, axis=0, keepdims=True)
    run_ref[...] = run
    cnt_ref[...] = run.astype(jnp.int32)


def _route(h2, wr_bf, b_router, *, tm):
    t, d = h2.shape[1], D_MODEL
    return pl.pallas_call(
        functools.partial(_route_kernel, tm=tm),
        grid=(t // tm,),
        in_specs=[pl.BlockSpec((NCH, tm, LANES), lambda i: (0, i, 0)),
                  pl.BlockSpec((d, N_EXPERTS), lambda i: (0, 0)),
                  pl.BlockSpec((1, N_EXPERTS), lambda i: (0, 0))],
        out_specs=(pl.BlockSpec((tm, 128), lambda i: (i, 0)),
                   pl.BlockSpec((tm, 128), lambda i: (i, 0)),
                   pl.BlockSpec((1, N_EXPERTS), lambda i: (0, 0))),
        out_shape=(jax.ShapeDtypeStruct((t, 128), jnp.int32),
                   jax.ShapeDtypeStruct((t, 128), F32),
                   jax.ShapeDtypeStruct((1, N_EXPERTS), jnp.int32)),
        scratch_shapes=[pltpu.VMEM((1, N_EXPERTS), F32)],
        compiler_params=_cparams(("arbitrary",)),
        name="route",
    )(h2, wr_bf, b_router.reshape(1, N_EXPERTS))


def _idx_copy(idx_hbm, idx_smem, sem, step, n, half):
    return pltpu.make_async_copy(idx_hbm.at[pl.ds(pl.multiple_of(step * n, n), n)],
                                 idx_smem.at[pl.ds(half * n, n)], sem.at[half])


def _slots_kernel(pk_ref, offs_ref, dst_ref):
    p = pk_ref[...]
    e = p >> RANK_BITS
    half = N_EXPERTS // 2
    lo = jnp.take_along_axis(jnp.broadcast_to(offs_ref[:, 0:half], p.shape), e & (half - 1), axis=1)
    hi = jnp.take_along_axis(jnp.broadcast_to(offs_ref[:, half:N_EXPERTS], p.shape), e & (half - 1), axis=1)
    dst_ref[...] = ((p & ((1 << RANK_BITS) - 1)) + jnp.where(e >= half, hi, lo)) * NCH


def _slots(pk, offs, *, tm):
    t = pk.shape[0]
    return pl.pallas_call(
        _slots_kernel,
        grid=(t // tm,),
        in_specs=[pl.BlockSpec((tm, 128), lambda i: (i, 0)),
                  pl.BlockSpec((1, N_EXPERTS), lambda i: (0, 0))],
        out_specs=pl.BlockSpec((tm, 128), lambda i: (i, 0)),
        out_shape=jax.ShapeDtypeStruct((t, 128), jnp.int32),
        compiler_params=_cparams(("arbitrary",)),
        name="slots",
    )(pk, offs[:N_EXPERTS].reshape(1, N_EXPERTS))


def _dispatch_kernel(h2_ref, dst_hbm, xs_hbm, idx_smem, idx_sem, row_sem, *, tm):
    i = pl.program_id(0)
    n = tm * TOP_K

    @pl.when(i == 0)
    def _():
        _idx_copy(dst_hbm, idx_smem, idx_sem, i, n, 0).start()

    def scatter(half):
        _idx_copy(dst_hbm, idx_smem, idx_sem, i, n, half).wait()

        @pl.when(i + 1 < pl.num_programs(0))
        def _():
            _idx_copy(dst_hbm, idx_smem, idx_sem, i + 1, n, 1 - half).start()

        def body(t8, carry):
            t0 = pl.multiple_of(t8 * 8, 8)
            b = half * n + t8 * (8 * TOP_K)
            for tt in range(8):
                src = h2_ref.at[:, t0 + tt, :]
                for k in range(TOP_K):
                    dst = pl.multiple_of(idx_smem[b + tt * TOP_K + k], NCH)
                    pltpu.make_async_copy(src, xs_hbm.at[pl.ds(dst, NCH)], row_sem).start(priority=k % 2)
            return carry

        lax.fori_loop(0, tm // 8, body, 0)

    for half in range(2):
        pl.when(i % 2 == half)(functools.partial(scatter, half))
    pltpu.make_async_copy(xs_hbm.at[pl.ds(0, n * NCH)], xs_hbm.at[pl.ds(0, n * NCH)], row_sem).wait()


def _dispatch(h2, dst_flat, *, tm):
    t = h2.shape[1]
    n = tm * TOP_K
    return pl.pallas_call(
        functools.partial(_dispatch_kernel, tm=tm),
        grid=(t // tm,),
        in_specs=[pl.BlockSpec((NCH, tm, LANES), lambda i: (0, i, 0)),
                  pl.BlockSpec(memory_space=pl.ANY)],
        out_specs=pl.BlockSpec(memory_space=pl.ANY),
        scratch_shapes=[pltpu.SMEM((2 * n,), jnp.int32),
                        pltpu.SemaphoreType.DMA((2,)),
                        pltpu.SemaphoreType.DMA(())],
        out_shape=jax.ShapeDtypeStruct((t * TOP_K * NCH, LANES), F32),
        compiler_params=_cparams(("arbitrary",)),
        name="dispatch",
    )(h2, dst_flat)


def _experts_kernel(vt_ref, vg_ref, offs_ref, nv_ref, xs_hbm, wg_ref, wu_ref, wd_ref, ys_ref,
                    wgb_ref, wub_ref, wdb_ref, xbuf_ref, xsem, *, tm):
    v = pl.program_id(0)
    nv = nv_ref[0]

    def x_copies(vv):
        slot = vv % 2
        row0 = pl.multiple_of(vt_ref[vv] * tm, tm)
        return [pltpu.make_async_copy(xs_hbm.at[pl.ds(row0, tm), c, :], xbuf_ref.at[slot, c], xsem.at[slot])
                for c in range(NCH)]

    @pl.when(jnp.logical_and(v == 0, nv > 0))
    def _():
        for cp in x_copies(v):
            cp.start()

    @pl.when(v < nv)
    def _():
        @pl.when(v + 1 < nv)
        def _():
            for cp in x_copies(v + 1):
                cp.start()

        for cp in x_copies(v):
            cp.wait()
        g = vg_ref[v]
        tile = vt_ref[v]
        prev = jnp.maximum(v - 1, 0)
        first = jnp.logical_or(v == 0, vt_ref[prev] != tile)

        @pl.when(jnp.logical_or(v == 0, vg_ref[prev] != g))
        def _():
            wgb_ref[...] = wg_ref[0].astype(BF16)
            wub_ref[...] = wu_ref[0].astype(BF16)
            wdb_ref[...] = wd_ref[0].astype(BF16)

        rowid = tile * tm + lax.broadcasted_iota(jnp.int32, (tm, 1), 0)
        mine = jnp.logical_and(rowid >= offs_ref[g], rowid < offs_ref[g + 1])
        x = _load_planes(xbuf_ref, (v % 2,)).astype(BF16)
        hg = jnp.dot(x, wgb_ref[...], preferred_element_type=F32)
        hu = jnp.dot(x, wub_ref[...], preferred_element_type=F32)
        hm = (hg * jax.nn.sigmoid(hg) * hu).astype(BF16)

        def down(c2):
            return jnp.dot(hm, wdb_ref[:, c2 * 2 * LANES:(c2 + 1) * 2 * LANES], preferred_element_type=F32)

        @pl.when(first)
        def _():
            for c2 in range(NCH // 2):
                y2 = down(c2)
                for j in range(2):
                    ys_ref[pl.ds(2 * c2 + j, tm, stride=NCH), :] = y2[:, j * LANES:(j + 1) * LANES]

        @pl.when(jnp.logical_not(first))
        def _():
            for c2 in range(NCH // 2):
                y2 = down(c2)
                for j in range(2):
                    old = ys_ref[pl.ds(2 * c2 + j, tm, stride=NCH), :]
                    ys_ref[pl.ds(2 * c2 + j, tm, stride=NCH), :] = jnp.where(
                        mine, y2[:, j * LANES:(j + 1) * LANES], old)


def _experts(vt, vg, offs, nv, xs, w_gate, w_up, w_down, *, tm):
    m, d = xs.shape[0] // NCH, D_MODEL
    nvis = vt.shape[0]
    de = w_gate.shape[2]
    xmap = lambda v, vt, vg, offs, nv: (vt[v], 0)
    wmap = lambda v, vt, vg, offs, nv: (vg[v], 0, 0)
    xs3 = xs.reshape(m, NCH, LANES)
    grid_spec = pltpu.PrefetchScalarGridSpec(
        num_scalar_prefetch=4,
        grid=(nvis,),
        in_specs=[pl.BlockSpec(memory_space=pl.ANY),
                  pl.BlockSpec((1, d, de), wmap),
                  pl.BlockSpec((1, d, de), wmap),
                  pl.BlockSpec((1, de, d), wmap)],
        out_specs=pl.BlockSpec((tm * NCH, LANES), xmap),
        scratch_shapes=[pltpu.VMEM((d, de), BF16), pltpu.VMEM((d, de), BF16), pltpu.VMEM((de, d), BF16),
                        pltpu.VMEM((2, NCH, tm, LANES), F32), pltpu.SemaphoreType.DMA((2,))],
    )
    return pl.pallas_call(
        functools.partial(_experts_kernel, tm=tm),
        grid_spec=grid_spec,
        out_shape=jax.ShapeDtypeStruct((m * NCH, LANES), F32),
        compiler_params=_cparams(("arbitrary",)),
        name="experts",
    )(vt, vg, offs, nv, xs3, w_gate, w_up, w_down)


def _combine_kernel(x1_ref, h2_ref, gw_ref, gt2_ref, wsg_ref, wsu_ref, wsd_ref, dst_hbm, ys_hbm,
                    y_ref, idx_smem, buf_ref, idx_sem, row_sem, *, tm, step0, bpt):
    i = pl.program_id(0)
    nsteps = pl.num_programs(0)
    n = tm * TOP_K

    def gather(step, half):
        _idx_copy(dst_hbm, idx_smem, idx_sem, step0 + step, n, half).wait()

        def body(t8, carry):
            t0 = pl.multiple_of(t8 * 8, 8)
            b = half * n + t8 * (8 * TOP_K)
            for tt in range(8):
                for k in range(TOP_K):
                    src = pl.multiple_of(idx_smem[b + tt * TOP_K + k], NCH)
                    pltpu.make_async_copy(ys_hbm.at[pl.ds(src, NCH)],
                                          buf_ref.at[half, k, :, t0 + tt, :],
                                          row_sem.at[half]).start(priority=k % 2)
            return carry

        lax.fori_loop(0, tm // 8, body, 0)

        @pl.when(step + 1 < nsteps)
        def _():
            _idx_copy(dst_hbm, idx_smem, idx_sem, step0 + step + 1, n, 1 - half).start()

    @pl.when(i == 0)
    def _():
        _idx_copy(dst_hbm, idx_smem, idx_sem, step0, n, 0).start()
        gather(i, 0)

    for half in range(2):
        pl.when(jnp.logical_and(i + 1 < nsteps, (i + 1) % 2 == half))(functools.partial(gather, i + 1, half))

    h2 = _load_planes(h2_ref).astype(BF16)
    sg = jnp.dot(h2, wsg_ref[...], preferred_element_type=F32)
    su = jnp.dot(h2, wsu_ref[...], preferred_element_type=F32)
    acc = jnp.dot((sg * jax.nn.sigmoid(sg) * su).astype(BF16), wsd_ref[...], preferred_element_type=F32)

    def finish(cur):
        pltpu.make_async_copy(ys_hbm.at[pl.ds(0, n * NCH)], ys_hbm.at[pl.ds(0, n * NCH)], row_sem.at[cur]).wait()
        gw = gw_ref[...]
        routed = gw[:, 0:1] * _load_planes(buf_ref, (cur, 0))
        for k in range(1, TOP_K):
            routed = routed + gw[:, k:k + 1] * _load_planes(buf_ref, (cur, k))
        ffn = routed + acc
        x1 = x1_ref[...]
        seq = tm // bpt
        for b in range(bpt):
            r = slice(b * seq, (b + 1) * seq)
            y_ref[b] = x1[r] + gt2_ref[b] * ffn[r]

    for cur in range(2):
        pl.when(i % 2 == cur)(functools.partial(finish, cur))


def _combine(x1, h2, gw, gt2, wsg, wsu, wsd, dst_flat, ys, *, tm, batch, seq, row_offset):
    d = x1.shape[1]
    n = tm * TOP_K
    bpt = max(tm // seq, 1)
    per_b = max(seq // tm, 1)
    ob = tm // bpt
    step0 = row_offset // tm
    tok = lambda i: (step0 + i, 0)
    c2 = lambda i: (0, 0)
    return pl.pallas_call(
        functools.partial(_combine_kernel, tm=tm, step0=step0, bpt=bpt),
        grid=(batch * seq // tm,),
        in_specs=[pl.BlockSpec((tm, d), tok),
                  pl.BlockSpec((NCH, tm, LANES), lambda i: (0, step0 + i, 0)),
                  pl.BlockSpec((tm, 128), tok),
                  pl.BlockSpec((bpt, 1, d), lambda i: (i // per_b, 0, 0)),
                  pl.BlockSpec(wsg.shape, c2),
                  pl.BlockSpec(wsu.shape, c2),
                  pl.BlockSpec(wsd.shape, c2),
                  pl.BlockSpec(memory_space=pl.ANY),
                  pl.BlockSpec(memory_space=pl.ANY)],
        out_specs=pl.BlockSpec((bpt, ob, d), lambda i: (i // per_b, i % per_b, 0)),
        scratch_shapes=[pltpu.SMEM((2 * n,), jnp.int32),
                        pltpu.VMEM((2, TOP_K, NCH, tm, LANES), F32),
                        pltpu.SemaphoreType.DMA((2,)),
                        pltpu.SemaphoreType.DMA((2,))],
        out_shape=jax.ShapeDtypeStruct((batch, seq, d), F32),
        compiler_params=_cparams(("arbitrary",)),
        name="combine",
    )(x1, h2, gw, gt2, wsg, wsu, wsd, dst_flat, ys)


def _bias_table(rel_bias, sb, kw):
    h, n_rel = rel_bias.shape
    lo = (kw - 1) - ATT_PAST_WINDOW - (CHUNK - 1)
    ntot = sb + kw - 1
    s = jnp.concatenate([jnp.broadcast_to(rel_bias[:, :1], (h, lo)), rel_bias,
                         jnp.broadcast_to(rel_bias[:, -1:], (h, ntot - lo - n_rel))], axis=1).astype(F32)
    s_rev = jnp.pad(s[:, ::-1], ((0, 0), (0, 1)))
    skew = jnp.broadcast_to(s_rev[:, None, :], (h, sb, ntot + 1)).reshape(h, -1)[:, :sb * ntot]
    toep = skew.reshape(h, sb, ntot)[:, :, sb - 1:sb - 1 + kw]
    r = np.arange(sb)[:, None]
    kk = np.arange(kw)[None, :]
    qc, kc = r // CHUNK, kk // CHUNK
    band = (kc >= qc) & (kc <= qc + LEFT_CHUNKS)
    return jnp.where(band[None], toep, NEG_INF)


def _visit_plan(offs, m, tm, nvis):
    cnt = offs[1:] - offs[:-1]
    first = offs[:-1] // tm
    last = jnp.maximum(offs[1:] - 1, 0) // tm
    per = jnp.where(cnt > 0, last - first + 1, 0)
    vend = jnp.cumsum(per)
    vstart = vend - per
    total = vend[-1]
    v = jnp.arange(nvis, dtype=jnp.int32)
    g = jnp.minimum(jnp.sum(vend[None, :] <= v[:, None], axis=1), N_EXPERTS - 1).astype(jnp.int32)
    tile = (first[g] + v - vstart[g]).astype(jnp.int32)
    live = v < total
    last_tile = m // tm - 1
    tile = jnp.where(live, tile, last_tile)
    g = jnp.where(live, g, N_EXPERTS - 1)
    return tile, g, total.reshape(1).astype(jnp.int32)


def kernel(x_prompt, x_sample, c_prompt, c_sample, cache_k, cache_v, w_ada, b_ada, g_mix, w_in, g_gmlp_v,
           w_spatial, b_spatial, g_q, g_k, rel_bias, g_out_a, g_out_b, w_out, g_ffn, w_router, b_router,
           w_exp_gate, w_exp_up, w_exp_down, w_sh_gate, w_sh_up, w_sh_down):
    depth = w_ada.shape[0]
    assert depth == 1, "single layer"
    bp, sp, d = x_prompt.shape
    bs, ss, _ = x_sample.shape
    tp, ts = bp * sp, bs * ss
    t_all = tp + ts
    l = 0

    c_all = jnp.concatenate([c_prompt, c_sample], axis=0)
    nb = c_all.shape[0]
    nb_pad = -(-nb // 8) * 8
    c_all = jnp.pad(c_all, ((0, nb_pad - nb), (0, 0)))
    mod = _ada(c_all, w_ada[l], b_ada[l])[:nb].reshape(nb, 6, 1, d)
    sh1, sc1, gt1, sh2, sc2, gt2 = [mod[:, i] for i in range(6)]

    w_in_bf = w_in[l].astype(BF16)
    w_out_bf = w_out[l].astype(BF16)
    gv = g_gmlp_v[l].reshape(1, W_A)
    gq = jnp.tile(g_q[l], B_HEADS).reshape(1, W_B)
    gk = jnp.tile(g_k[l], B_HEADS).reshape(1, W_B)
    hid = np.arange(W_B) // HEAD_DIM
    hsum = jnp.asarray(hid[:, None] == hid[None, :], BF16)
    gmix = g_mix[l].reshape(1, d)
    goa = g_out_a[l].reshape(1, W_A)
    gob = g_out_b[l].reshape(1, W_B)
    gffn = g_ffn[l].reshape(1, d)
    bias_p = _bias_table(rel_bias[l], GMLP_BLOCK, ATT_PAST_WINDOW + GMLP_BLOCK)
    bsp_p = jnp.broadcast_to(b_spatial[l][:, :, None], (A_GROUPS, GMLP_BLOCK, A_GROUP_DIM))

    cl = cache_k.shape[2]
    us, vns, qs, ks_bf, vs_bf, k32s, v32s = _in_proj(
        x_sample, sh1[bp:], sc1[bp:], gmix, w_in_bf, gv, gq, gk, hsum,
        tm=ss, pad_rows=0, keep=ss, vn_dtype=F32)
    kcat = jnp.concatenate([cache_k[l].reshape(bs, cl, W_B).astype(BF16), ks_bf], axis=1)
    vcat = jnp.concatenate([cache_v[l].reshape(bs, cl, W_B).astype(BF16), vs_bf], axis=1)
    bias_s = bias_p[:, :ss, ATT_PAST_WINDOW - cl:ATT_PAST_WINDOW + ss]
    x1_s, h2_s = _mixer(x_sample, us, vns, qs, kcat, vcat, bias_s, w_spatial[l][:, :ss, :ss], bsp_p[:, :ss],
                        goa, gob, w_out_bf, gt1[bp:], gffn, sh2[bp:], sc2[bp:],
                        sb=ss, nsb=1, padded=False)
    new_k_sample = k32s.reshape(1, bs, ss, B_HEADS, HEAD_DIM)
    new_v_sample = v32s.reshape(1, bs, ss, B_HEADS, HEAD_DIM)
    new_gmlp_v_sample = vns.reshape(1, bs, ss, A_GROUPS, A_GROUP_DIM)

    keep = min(ATT_PAST_WINDOW, sp)
    u, vn, q, kpad, vpad, k32, v32 = _in_proj(
        x_prompt, sh1[:bp], sc1[:bp], gmix, w_in_bf, gv, gq, gk, hsum,
        tm=512, pad_rows=ATT_PAST_WINDOW, keep=keep, vn_dtype=BF16)
    x1, h2 = _mixer(x_prompt, u, vn, q, kpad, vpad, bias_p, w_spatial[l], bsp_p, goa, gob, w_out_bf,
                    gt1[:bp], gffn, sh2[:bp], sc2[:bp],
                    sb=GMLP_BLOCK, nsb=2, padded=True, tail=(x1_s, h2_s))
    assert x1.shape[0] == t_all
    new_k_prompt = k32.reshape(1, bp, keep, B_HEADS, HEAD_DIM)
    new_v_prompt = v32.reshape(1, bp, keep, B_HEADS, HEAD_DIM)

    tmr = 256
    tme = 512
    assert t_all % tmr == 0 and (t_all * TOP_K) % tme == 0 and sp % 256 == 0 and ts % (2 * ss) == 0
    assert t_all * TOP_K < (1 << RANK_BITS)
    pk, gw, cnt = _route(h2, w_router[l].astype(BF16), b_router[l], tm=tmr)
    offs = jnp.concatenate([jnp.zeros((1,), jnp.int32), jnp.cumsum(cnt[0]).astype(jnp.int32)])
    dst_flat = _slots(pk, offs, tm=tmr)[:, :TOP_K].reshape(-1)
    xs = _dispatch(h2, dst_flat, tm=tmr)
    m = t_all * TOP_K
    nvis = m // tme + N_EXPERTS - 1
    vt, vg, nv = _visit_plan(offs, m, tme, nvis)
    ys = _experts(vt, vg, offs, nv, xs, w_exp_gate[l], w_exp_up[l], w_exp_down[l], tm=tme)

    wsg, wsu, wsd = w_sh_gate[l].astype(BF16), w_sh_up[l].astype(BF16), w_sh_down[l].astype(BF16)
    y_prompt = _combine(x1, h2, gw, gt2[:bp], wsg, wsu, wsd, dst_flat, ys,
                        tm=256, batch=bp, seq=sp, row_offset=0)
    y_sample = _combine(x1, h2, gw, gt2[bp:], wsg, wsu, wsd, dst_flat, ys,
                        tm=2 * ss, batch=bs, seq=ss, row_offset=tp)
    return (y_prompt, y_sample, new_k_prompt, new_v_prompt, new_k_sample, new_v_sample, new_gmlp_v_sample)
```

```python
import functools

import jax
import jax.numpy as jnp
import numpy as np
from jax import lax
from jax.experimental import pallas as pl
from jax.experimental.pallas import tpu as pltpu

F32 = jnp.float32
BF16 = jnp.bfloat16

D_MODEL = 1024
CHUNK = 64
GMLP_BLOCK = 128
W_A = 512
A_GROUPS = 4
A_GROUP_DIM = 128
W_B = 512
B_HEADS = 8
HEAD_DIM = 64
LEFT_CHUNKS = 8
ATT_PAST_WINDOW = LEFT_CHUNKS * CHUNK
REL_CLIP = 128
D_IN = 2 * W_A + 3 * W_B
N_EXPERTS = 256
TOP_K = 8
D_EXPERT = 256
ROUTED_SCALE = 2.5
EPS = 1e-6
NEG_INF = -1e30

RANK_BITS = 20
VMEM_LIMIT = 56 * 1024 * 1024


def _cparams(sem, vmem=VMEM_LIMIT):
    return pltpu.CompilerParams(dimension_semantics=sem, vmem_limit_bytes=vmem)


def _rms(x, axis=-1):
    return lax.rsqrt(jnp.mean(x * x, axis=axis, keepdims=True) + EPS)


LANES = 128
NCH = D_MODEL // LANES


def _load_rows(ref, n, lead=()):
    return jnp.concatenate([ref[(*lead, pl.ds(c, n, stride=NCH), slice(None))] for c in range(NCH)], axis=1)


def _store_rows(ref, val):
    n = val.shape[0]
    for c in range(NCH):
        ref[pl.ds(c, n, stride=NCH), :] = val[:, c * LANES:(c + 1) * LANES]


def _ada_kernel(c_ref, w_ref, b_ref, o_ref):
    c = c_ref[...]
    s = c * jax.nn.sigmoid(c)
    o_ref[...] = jnp.dot(s.astype(BF16), w_ref[...].astype(BF16), preferred_element_type=F32) + b_ref[...]


def _ada(c, w_ada, b_ada):
    n, d = c.shape
    nout = w_ada.shape[1]
    tn = 1024
    return pl.pallas_call(
        _ada_kernel,
        grid=(nout // tn,),
        in_specs=[pl.BlockSpec((n, d), lambda j: (0, 0)),
                  pl.BlockSpec((d, tn), lambda j: (0, j)),
                  pl.BlockSpec((1, tn), lambda j: (0, j))],
        out_specs=pl.BlockSpec((n, tn), lambda j: (0, j)),
        out_shape=jax.ShapeDtypeStruct((n, nout), F32),
        compiler_params=_cparams(("arbitrary",)),
        name="ada",
    )(c, w_ada, b_ada.reshape(1, nout))


def _in_proj_kernel(x_ref, sh_ref, sc_ref, gmix_ref, win_ref, gv_ref, gq_ref, gk_ref, hsum_ref,
                    u_ref, vn_ref, q_ref, kp_ref, vp_ref, k32_ref, v32_ref, *, npad, first_kept):
    i = pl.program_id(1)

    @pl.when(i < npad)
    def _():
        kp_ref[...] = jnp.zeros_like(kp_ref)
        vp_ref[...] = jnp.zeros_like(vp_ref)

    @pl.when(i >= npad)
    def _():
        x = x_ref[0]
        h = x * _rms(x) * gmix_ref[...]
        h = h * (1.0 + sc_ref[0]) + sh_ref[0]
        proj = jnp.dot(h.astype(BF16), win_ref[...], preferred_element_type=F32)
        u = jax.nn.gelu(proj[:, 0:W_A])
        u_ref[0] = u.astype(u_ref.dtype)
        v = jax.nn.gelu(proj[:, W_A:2 * W_A])
        for g in range(A_GROUPS):
            vg = v[:, g * A_GROUP_DIM:(g + 1) * A_GROUP_DIM]
            vn = vg * _rms(vg) * gv_ref[:, g * A_GROUP_DIM:(g + 1) * A_GROUP_DIM]
            vn_ref[0, :, g * A_GROUP_DIM:(g + 1) * A_GROUP_DIM] = vn.astype(vn_ref.dtype)
        q = proj[:, 2 * W_A:2 * W_A + W_B]
        k = proj[:, 2 * W_A + W_B:2 * W_A + 2 * W_B]
        va = proj[:, 2 * W_A + 2 * W_B:]
        q2 = jnp.dot((q * q).astype(BF16), hsum_ref[...], preferred_element_type=F32) * (1.0 / HEAD_DIM)
        k2 = jnp.dot((k * k).astype(BF16), hsum_ref[...], preferred_element_type=F32) * (1.0 / HEAD_DIM)
        qn = q * lax.rsqrt(q2 + EPS) * gq_ref[...]
        kn = k * lax.rsqrt(k2 + EPS) * gk_ref[...]
        q_ref[0] = (qn * (HEAD_DIM ** -0.5)).astype(q_ref.dtype)
        kp_ref[0] = kn.astype(kp_ref.dtype)
        vp_ref[0] = va.astype(vp_ref.dtype)

        @pl.when(i >= first_kept)
        def _():
            k32_ref[0] = kn
            v32_ref[0] = va


def _in_proj(x, sh1, sc1, g_mix, w_in_bf, gv, gq, gk, hsum, *, tm, pad_rows, keep, vn_dtype):
    b, l, d = x.shape
    assert l % tm == 0 and pad_rows % tm == 0 and keep % tm == 0
    npad = pad_rows // tm
    nt = l // tm
    first_kept = npad + nt - keep // tm
    row = lambda bi, i: (bi, jnp.maximum(i - npad, 0), 0)
    kept = lambda bi, i: (bi, jnp.maximum(i - first_kept, 0), 0)
    bvec = lambda bi, i: (bi, 0, 0)
    const2 = lambda bi, i: (0, 0)
    out_shapes = (
        jax.ShapeDtypeStruct((b, l, W_A), BF16),
        jax.ShapeDtypeStruct((b, l, W_A), vn_dtype),
        jax.ShapeDtypeStruct((b, l, W_B), BF16),
        jax.ShapeDtypeStruct((b, pad_rows + l, W_B), BF16),
        jax.ShapeDtypeStruct((b, pad_rows + l, W_B), BF16),
        jax.ShapeDtypeStruct((b, keep, W_B), F32),
        jax.ShapeDtypeStruct((b, keep, W_B), F32),
    )
    padrow = lambda bi, i: (bi, i, 0)
    return pl.pallas_call(
        functools.partial(_in_proj_kernel, npad=npad, first_kept=first_kept),
        grid=(b, nt + npad),
        in_specs=[pl.BlockSpec((1, tm, d), row),
                  pl.BlockSpec((1, 1, d), bvec),
                  pl.BlockSpec((1, 1, d), bvec),
                  pl.BlockSpec((1, d), const2),
                  pl.BlockSpec((d, D_IN), const2),
                  pl.BlockSpec((1, W_A), const2),
                  pl.BlockSpec((1, W_B), const2),
                  pl.BlockSpec((1, W_B), const2),
                  pl.BlockSpec((W_B, W_B), const2)],
        out_specs=(pl.BlockSpec((1, tm, W_A), row),
                   pl.BlockSpec((1, tm, W_A), row),
                   pl.BlockSpec((1, tm, W_B), row),
                   pl.BlockSpec((1, tm, W_B), padrow),
                   pl.BlockSpec((1, tm, W_B), padrow),
                   pl.BlockSpec((1, tm, W_B), kept),
                   pl.BlockSpec((1, tm, W_B), kept)),
        out_shape=out_shapes,
        compiler_params=_cparams(("arbitrary", "arbitrary")),
        name="in_proj",
    )(x, sh1, sc1, g_mix, w_in_bf, gv, gq, gk, hsum)


def _mixer_kernel(x_ref, u_ref, vn_ref, q_ref, kp_ref, vp_ref, bias_ref, wsp_ref, bsp_ref,
                  goa_ref, gob_ref, wout_ref, gt1_ref, gffn_ref, sh2_ref, sc2_ref, *rest,
                  sb, nsb, kw, padded, nj, nmain):
    x1_ref, h2_ref, merged_ref = rest[-3:]
    i = pl.program_id(0)
    if len(rest) == 5:
        tx1_ref, th2_ref = rest[:2]

        @pl.when(i >= nmain)
        def _():
            x1_ref[...] = tx1_ref[...]
            h2_ref[...] = th2_ref[...]

    @pl.when(i < nmain)
    def _():
        _mixer_body(x_ref, u_ref, vn_ref, q_ref, kp_ref, vp_ref, bias_ref, wsp_ref, bsp_ref,
                    goa_ref, gob_ref, wout_ref, gt1_ref, gffn_ref, sh2_ref, sc2_ref,
                    x1_ref, h2_ref, merged_ref, i % nj, sb=sb, nsb=nsb, kw=kw, padded=padded)


def _mixer_body(x_ref, u_ref, vn_ref, q_ref, kp_ref, vp_ref, bias_ref, wsp_ref, bsp_ref,
                goa_ref, gob_ref, wout_ref, gt1_ref, gffn_ref, sh2_ref, sc2_ref,
                x1_ref, h2_ref, merged_ref, j, *, sb, nsb, kw, padded):
    rows = sb * nsb
    ri = lax.broadcasted_iota(jnp.int32, (sb, sb), 0)
    ci = lax.broadcasted_iota(jnp.int32, (sb, sb), 1)
    gmask = (ci // CHUNK) <= (ri // CHUNK)
    lane = lax.broadcasted_iota(jnp.int32, (sb, 2 * HEAD_DIM), 1)
    lo_half = lane < HEAD_DIM
    kidx = lax.broadcasted_iota(jnp.int32, (sb, kw), 1)

    for s in range(nsb):
        r0 = s * sb
        vn = vn_ref[0, r0:r0 + sb, :].astype(BF16)
        u = u_ref[0, r0:r0 + sb, :].astype(F32)
        zs = []
        for g in range(A_GROUPS):
            w = jnp.where(gmask, wsp_ref[g], 0.0).astype(BF16)
            z = jnp.dot(w, vn[:, g * A_GROUP_DIM:(g + 1) * A_GROUP_DIM], preferred_element_type=F32)
            zs.append(z + bsp_ref[g])
        a = u * jnp.concatenate(zs, axis=1)
        a = a * _rms(a) * goa_ref[...]
        merged_ref[r0:r0 + sb, 0:W_A] = a.astype(BF16)

        if padded:
            blk0 = j * rows + r0
            kstart = pl.multiple_of(blk0, sb)
            kb = kp_ref[0, pl.ds(kstart, kw), :]
            vb = vp_ref[0, pl.ds(kstart, kw), :]
            valid = kidx >= (ATT_PAST_WINDOW - blk0)
        else:
            kb = kp_ref[0]
            vb = vp_ref[0]
            valid = None
        qb = q_ref[0, r0:r0 + sb, :]
        outs = []
        for hp in range(B_HEADS // 2):
            c0 = hp * 2 * HEAD_DIM
            qp = qb[:, c0:c0 + 2 * HEAD_DIM]
            kp = kb[:, c0:c0 + 2 * HEAD_DIM]
            vp = vb[:, c0:c0 + 2 * HEAD_DIM]
            o_pair = []
            for hh in range(2):
                qm = jnp.where(lo_half if hh == 0 else jnp.logical_not(lo_half), qp, jnp.zeros_like(qp))
                sc = lax.dot_general(qm, kp, (((1,), (1,)), ((), ())), preferred_element_type=F32)
                sc = sc + bias_ref[2 * hp + hh]
                if valid is not None:
                    sc = jnp.where(valid, sc, NEG_INF)
                m = jnp.max(sc, axis=-1, keepdims=True)
                p = jnp.exp(sc - m)
                l = jnp.sum(p, axis=-1, keepdims=True)
                o = jnp.dot(p.astype(BF16), vp, preferred_element_type=F32)
                o_pair.append(o / l)
            outs.append(jnp.where(lo_half, o_pair[0], o_pair[1]))
        bo = jnp.concatenate(outs, axis=1)
        bo = bo * _rms(bo) * gob_ref[...]
        merged_ref[r0:r0 + sb, W_A:W_A + W_B] = bo.astype(BF16)

    mix = jnp.dot(merged_ref[...], wout_ref[...], preferred_element_type=F32)
    x1 = x_ref[0] + gt1_ref[0] * mix
    x1_ref[...] = x1
    h2 = x1 * _rms(x1) * gffn_ref[...]
    _store_rows(h2_ref, h2 * (1.0 + sc2_ref[0]) + sh2_ref[0])


def _mixer(x, u, vn, q, kp, vp, bias, wsp, bsp, goa, gob, wout_bf, gt1, gffn, sh2, sc2,
           *, sb, nsb, padded, tail=None):
    b, l, d = x.shape
    rows = sb * nsb
    assert l % rows == 0
    nj = l // rows
    nmain = b * nj
    ntail = 0
    if tail is not None:
        assert tail[0].shape[0] % rows == 0
        ntail = tail[0].shape[0] // rows
    total_rows = (nmain + ntail) * rows
    kw = bias.shape[2]
    kl = kp.shape[1]
    bidx = lambda i: jnp.minimum(i // nj, b - 1)
    row = lambda i: (bidx(i), i % nj, 0)
    bvec = lambda i: (bidx(i), 0, 0)
    c2 = lambda i: (0, 0)
    c3 = lambda i: (0, 0, 0)
    pool = lambda i: (i, 0)
    in_specs = [pl.BlockSpec((1, rows, d), row),
                pl.BlockSpec((1, rows, W_A), row),
                pl.BlockSpec((1, rows, W_A), row),
                pl.BlockSpec((1, rows, W_B), row),
                pl.BlockSpec((1, kl, W_B), bvec),
                pl.BlockSpec((1, kl, W_B), bvec),
                pl.BlockSpec(bias.shape, c3),
                pl.BlockSpec(wsp.shape, c3),
                pl.BlockSpec(bsp.shape, c3),
                pl.BlockSpec((1, W_A), c2),
                pl.BlockSpec((1, W_B), c2),
                pl.BlockSpec((d, d), c2),
                pl.BlockSpec((1, 1, d), bvec),
                pl.BlockSpec((1, d), c2),
                pl.BlockSpec((1, 1, d), bvec),
                pl.BlockSpec((1, 1, d), bvec)]
    args = [x, u, vn, q, kp, vp, bias, wsp, bsp, goa, gob, wout_bf, gt1, gffn, sh2, sc2]
    if tail is not None:
        tmap = lambda i: (jnp.maximum(i - nmain, 0), 0)
        in_specs += [pl.BlockSpec((rows, d), tmap), pl.BlockSpec((rows * NCH, LANES), tmap)]
        args += list(tail)
    return pl.pallas_call(
        functools.partial(_mixer_kernel, sb=sb, nsb=nsb, kw=kw, padded=padded, nj=nj, nmain=nmain),
        grid=(nmain + ntail,),
        in_specs=in_specs,
        out_specs=(pl.BlockSpec((rows, d), pool), pl.BlockSpec((rows * NCH, LANES), pool)),
        out_shape=(jax.ShapeDtypeStruct((total_rows, d), F32),
                   jax.ShapeDtypeStruct((total_rows * NCH, LANES), F32)),
        scratch_shapes=[pltpu.VMEM((rows, d), BF16)],
        compiler_params=_cparams(("arbitrary",)),
        name="mixer",
    )(*args)


def _route_kernel(h2_ref, wr_ref, br_ref, pk_ref, gw_ref, cnt_ref, run_ref, *, tm):
    i = pl.program_id(0)

    @pl.when(i == 0)
    def _():
        run_ref[...] = jnp.zeros_like(run_ref)

    logits = jnp.dot(_load_rows(h2_ref, tm).astype(BF16), wr_ref[...], preferred_element_type=F32)
    aff = jax.nn.sigmoid(logits)
    sel = aff + br_ref[...]
    lane = lax.broadcasted_iota(jnp.int32, (tm, N_EXPERTS), 1)
    out_lane = lax.broadcasted_iota(jnp.int32, (tm, 128), 1)
    idx = jnp.zeros((tm, 128), jnp.int32)
    onehot = jnp.zeros((tm, N_EXPERTS), F32)
    for k in range(TOP_K):
        m = jnp.max(sel, axis=-1, keepdims=True)
        ik = jnp.min(jnp.where(sel == m, lane, N_EXPERTS), axis=-1, keepdims=True)
        oh = lane == ik
        sel = jnp.where(oh, -jnp.inf, sel)
        onehot = jnp.where(oh, 1.0, onehot)
        idx = jnp.where(out_lane == k, ik, idx)
    ri = lax.broadcasted_iota(jnp.int32, (tm, tm), 0)
    ci = lax.broadcasted_iota(jnp.int32, (tm, tm), 1)
    tri = (ci < ri).astype(BF16)
    before = jnp.dot(tri, onehot.astype(BF16), preferred_element_type=F32) + run_ref[...]

    half = N_EXPERTS // 2
    in_hi = idx >= half
    sub = idx & (half - 1)

    def pick(tab):
        return jnp.where(in_hi, jnp.take_along_axis(tab[:, half:], sub, axis=1),
                         jnp.take_along_axis(tab[:, :half], sub, axis=1))

    chosen = out_lane < TOP_K
    a_sel = jnp.where(chosen, pick(aff), 0.0)
    tot = jnp.sum(a_sel, axis=-1, keepdims=True)
    gw_ref[...] = a_sel / tot * ROUTED_SCALE
    rank = pick(before).astype(jnp.int32)
    pk_ref[...] = jnp.where(chosen, idx * (1 << RANK_BITS) + rank, 0)
    run = run_ref[...] + jnp.sum(onehot, axis=0, keepdims=True)
    run_ref[...] = run
    cnt_ref[...] = run.astype(jnp.int32)


def _route(h2, wr_bf, b_router, *, tm):
    t, d = h2.shape[0] // NCH, D_MODEL
    return pl.pallas_call(
        functools.partial(_route_kernel, tm=tm),
        grid=(t // tm,),
        in_specs=[pl.BlockSpec((tm * NCH, LANES), lambda i: (i, 0)),
                  pl.BlockSpec((d, N_EXPERTS), lambda i: (0, 0)),
                  pl.BlockSpec((1, N_EXPERTS), lambda i: (0, 0))],
        out_specs=(pl.BlockSpec((tm, 128), lambda i: (i, 0)),
                   pl.BlockSpec((tm, 128), lambda i: (i, 0)),
                   pl.BlockSpec((1, N_EXPERTS), lambda i: (0, 0))),
        out_shape=(jax.ShapeDtypeStruct((t, 128), jnp.int32),
                   jax.ShapeDtypeStruct((t, 128), F32),
                   jax.ShapeDtypeStruct((1, N_EXPERTS), jnp.int32)),
        scratch_shapes=[pltpu.VMEM((1, N_EXPERTS), F32)],
        compiler_params=_cparams(("arbitrary",)),
        name="route",
    )(h2, wr_bf, b_router.reshape(1, N_EXPERTS))


def _idx_copy(idx_hbm, idx_smem, sem, step, n, half):
    return pltpu.make_async_copy(idx_hbm.at[pl.ds(pl.multiple_of(step * n, n), n)],
                                 idx_smem.at[pl.ds(half * n, n)], sem.at[half])


def _slots_kernel(pk_ref, offs_ref, dst_ref):
    p = pk_ref[...]
    e = p >> RANK_BITS
    half = N_EXPERTS // 2
    lo = jnp.take_along_axis(jnp.broadcast_to(offs_ref[:, 0:half], p.shape), e & (half - 1), axis=1)
    hi = jnp.take_along_axis(jnp.broadcast_to(offs_ref[:, half:N_EXPERTS], p.shape), e & (half - 1), axis=1)
    dst_ref[...] = ((p & ((1 << RANK_BITS) - 1)) + jnp.where(e >= half, hi, lo)) * NCH


def _slots(pk, offs, *, tm):
    t = pk.shape[0]
    return pl.pallas_call(
        _slots_kernel,
        grid=(t // tm,),
        in_specs=[pl.BlockSpec((tm, 128), lambda i: (i, 0)),
                  pl.BlockSpec((1, N_EXPERTS), lambda i: (0, 0))],
        out_specs=pl.BlockSpec((tm, 128), lambda i: (i, 0)),
        out_shape=jax.ShapeDtypeStruct((t, 128), jnp.int32),
        compiler_params=_cparams(("arbitrary",)),
        name="slots",
    )(pk, offs[:N_EXPERTS].reshape(1, N_EXPERTS))


def _dispatch_kernel(h2_ref, dst_hbm, xs_hbm, idx_smem, idx_sem, row_sem, *, tm):
    i = pl.program_id(0)
    n = tm * TOP_K

    @pl.when(i == 0)
    def _():
        _idx_copy(dst_hbm, idx_smem, idx_sem, i, n, 0).start()

    def scatter(half):
        _idx_copy(dst_hbm, idx_smem, idx_sem, i, n, half).wait()

        @pl.when(i + 1 < pl.num_programs(0))
        def _():
            _idx_copy(dst_hbm, idx_smem, idx_sem, i + 1, n, 1 - half).start()

        def body(t8, carry):
            r0 = pl.multiple_of(t8 * (8 * NCH), 8 * NCH)
            b = half * n + t8 * (8 * TOP_K)
            for tt in range(8):
                src = h2_ref.at[pl.ds(r0 + tt * NCH, NCH)]
                for k in range(TOP_K):
                    dst = pl.multiple_of(idx_smem[b + tt * TOP_K + k], NCH)
                    pltpu.make_async_copy(src, xs_hbm.at[pl.ds(dst, NCH)], row_sem).start(priority=k % 2)
            return carry

        lax.fori_loop(0, tm // 8, body, 0)

    for half in range(2):
        pl.when(i % 2 == half)(functools.partial(scatter, half))
    pltpu.make_async_copy(xs_hbm.at[pl.ds(0, n * NCH)], xs_hbm.at[pl.ds(0, n * NCH)], row_sem).wait()


def _dispatch(h2, dst_flat, *, tm):
    t = h2.shape[0] // NCH
    n = tm * TOP_K
    return pl.pallas_call(
        functools.partial(_dispatch_kernel, tm=tm),
        grid=(t // tm,),
        in_specs=[pl.BlockSpec((tm * NCH, LANES), lambda i: (i, 0)),
                  pl.BlockSpec(memory_space=pl.ANY)],
        out_specs=pl.BlockSpec(memory_space=pl.ANY),
        scratch_shapes=[pltpu.SMEM((2 * n,), jnp.int32),
                        pltpu.SemaphoreType.DMA((2,)),
                        pltpu.SemaphoreType.DMA(())],
        out_shape=jax.ShapeDtypeStruct((t * TOP_K * NCH, LANES), F32),
        compiler_params=_cparams(("arbitrary",)),
        name="dispatch",
    )(h2, dst_flat)


def _experts_kernel(vt_ref, vg_ref, offs_ref, nv_ref, xs_ref, wg_ref, wu_ref, wd_ref, ys_ref,
                    wgb_ref, wub_ref, wdb_ref, *, tm):
    v = pl.program_id(0)

    @pl.when(v < nv_ref[0])
    def _():
        g = vg_ref[v]
        tile = vt_ref[v]
        prev = jnp.maximum(v - 1, 0)
        first = jnp.logical_or(v == 0, vt_ref[prev] != tile)

        @pl.when(jnp.logical_or(v == 0, vg_ref[prev] != g))
        def _():
            wgb_ref[...] = wg_ref[0].astype(BF16)
            wub_ref[...] = wu_ref[0].astype(BF16)
            wdb_ref[...] = wd_ref[0].astype(BF16)

        rowid = tile * tm + lax.broadcasted_iota(jnp.int32, (tm, 1), 0)
        mine = jnp.logical_and(rowid >= offs_ref[g], rowid < offs_ref[g + 1])
        x = _load_rows(xs_ref, tm).astype(BF16)
        hg = jnp.dot(x, wgb_ref[...], preferred_element_type=F32)
        hu = jnp.dot(x, wub_ref[...], preferred_element_type=F32)
        hm = (hg * jax.nn.sigmoid(hg) * hu).astype(BF16)

        def down(c2):
            return jnp.dot(hm, wdb_ref[:, c2 * 2 * LANES:(c2 + 1) * 2 * LANES], preferred_element_type=F32)

        @pl.when(first)
        def _():
            for c2 in range(NCH // 2):
                y2 = down(c2)
                for j in range(2):
                    ys_ref[pl.ds(2 * c2 + j, tm, stride=NCH), :] = y2[:, j * LANES:(j + 1) * LANES]

        @pl.when(jnp.logical_not(first))
        def _():
            for c2 in range(NCH // 2):
                y2 = down(c2)
                for j in range(2):
                    old = ys_ref[pl.ds(2 * c2 + j, tm, stride=NCH), :]
                    ys_ref[pl.ds(2 * c2 + j, tm, stride=NCH), :] = jnp.where(
                        mine, y2[:, j * LANES:(j + 1) * LANES], old)


def _experts(vt, vg, offs, nv, xs, w_gate, w_up, w_down, *, tm):
    m, d = xs.shape[0] // NCH, D_MODEL
    nvis = vt.shape[0]
    de = w_gate.shape[2]
    xmap = lambda v, vt, vg, offs, nv: (vt[v], 0)
    wmap = lambda v, vt, vg, offs, nv: (vg[v], 0, 0)
    grid_spec = pltpu.PrefetchScalarGridSpec(
        num_scalar_prefetch=4,
        grid=(nvis,),
        in_specs=[pl.BlockSpec((tm * NCH, LANES), xmap),
                  pl.BlockSpec((1, d, de), wmap),
                  pl.BlockSpec((1, d, de), wmap),
                  pl.BlockSpec((1, de, d), wmap)],
        out_specs=pl.BlockSpec((tm * NCH, LANES), xmap),
        scratch_shapes=[pltpu.VMEM((d, de), BF16), pltpu.VMEM((d, de), BF16), pltpu.VMEM((de, d), BF16)],
    )
    return pl.pallas_call(
        functools.partial(_experts_kernel, tm=tm),
        grid_spec=grid_spec,
        out_shape=jax.ShapeDtypeStruct((m * NCH, LANES), F32),
        compiler_params=_cparams(("arbitrary",)),
        name="experts",
    )(vt, vg, offs, nv, xs, w_gate, w_up, w_down)


def _combine_kernel(x1_ref, h2_ref, gw_ref, gt2_ref, wsg_ref, wsu_ref, wsd_ref, dst_hbm, ys_hbm,
                    y_ref, idx_smem, buf_ref, acc_ref, idx_sem, row_sem, *, tm, step0, per_token_gate):
    i = pl.program_id(0)
    nsteps = pl.num_programs(0)
    n = tm * TOP_K

    def issue(b, r0, half):
        for tt in range(8):
            for k in range(TOP_K):
                src = pl.multiple_of(idx_smem[b + tt * TOP_K + k], NCH)
                pltpu.make_async_copy(ys_hbm.at[pl.ds(src, NCH)],
                                      buf_ref.at[half, k, pl.ds(r0 + tt * NCH, NCH)],
                                      row_sem.at[half]).start(priority=k % 2)

    def finish8(t8, cur):
        t0 = pl.multiple_of(t8 * 8, 8)
        gw = gw_ref[pl.ds(t0, 8), :]
        for c in range(NCH):
            lanes = slice(c * LANES, (c + 1) * LANES)
            ffn = acc_ref[pl.ds(t0, 8), lanes]
            for k in range(TOP_K):
                ffn = ffn + gw[:, k:k + 1] * buf_ref[cur, k, pl.ds(t0 * NCH + c, 8, stride=NCH), :]
            gate = gt2_ref[pl.ds(t0, 8), lanes] if per_token_gate else gt2_ref[0][:, lanes]
            y_ref[pl.ds(t0, 8), lanes] = x1_ref[pl.ds(t0, 8), lanes] + gate * ffn

    @pl.when(i == 0)
    def _():
        _idx_copy(dst_hbm, idx_smem, idx_sem, step0, n, 0).start()
        _idx_copy(dst_hbm, idx_smem, idx_sem, step0, n, 0).wait()

        def body(t8, carry):
            issue(t8 * (8 * TOP_K), pl.multiple_of(t8 * (8 * NCH), 8 * NCH), 0)
            return carry

        lax.fori_loop(0, tm // 8, body, 0)

        @pl.when(nsteps > 1)
        def _():
            _idx_copy(dst_hbm, idx_smem, idx_sem, step0 + 1, n, 1).start()

    h2 = _load_rows(h2_ref, tm).astype(BF16)
    sg = jnp.dot(h2, wsg_ref[...], preferred_element_type=F32)
    su = jnp.dot(h2, wsu_ref[...], preferred_element_type=F32)
    acc_ref[...] = jnp.dot((sg * jax.nn.sigmoid(sg) * su).astype(BF16), wsd_ref[...], preferred_element_type=F32)

    def step(cur, has_next):
        pltpu.make_async_copy(ys_hbm.at[pl.ds(0, n * NCH)], ys_hbm.at[pl.ds(0, n * NCH)], row_sem.at[cur]).wait()
        if has_next:
            _idx_copy(dst_hbm, idx_smem, idx_sem, step0 + i + 1, n, 1 - cur).wait()

        def body(t8, carry):
            if has_next:
                issue((1 - cur) * n + t8 * (8 * TOP_K), pl.multiple_of(t8 * (8 * NCH), 8 * NCH), 1 - cur)
            finish8(t8, cur)
            return carry

        lax.fori_loop(0, tm // 8, body, 0)

        if has_next:
            @pl.when(i + 2 < nsteps)
            def _():
                _idx_copy(dst_hbm, idx_smem, idx_sem, step0 + i + 2, n, cur).start()

    for cur in range(2):
        pl.when(jnp.logical_and(i % 2 == cur, i + 1 < nsteps))(functools.partial(step, cur, True))
        pl.when(jnp.logical_and(i % 2 == cur, i + 1 == nsteps))(functools.partial(step, cur, False))


def _combine(x1, h2, gw, gt2, wsg, wsu, wsd, dst_flat, ys, *, tm, batch, seq, row_offset):
    d = x1.shape[1]
    n = tm * TOP_K
    per_token_gate = gt2.ndim == 2
    assert per_token_gate or seq % tm == 0
    per_b = max(seq // tm, 1)
    step0 = row_offset // tm
    tok = lambda i: (step0 + i, 0)
    c2 = lambda i: (0, 0)
    gate_spec = (pl.BlockSpec((tm, d), lambda i: (i, 0)) if per_token_gate
                 else pl.BlockSpec((1, 1, d), lambda i: (i // per_b, 0, 0)))
    return pl.pallas_call(
        functools.partial(_combine_kernel, tm=tm, step0=step0, per_token_gate=per_token_gate),
        grid=(batch * seq // tm,),
        in_specs=[pl.BlockSpec((tm, d), tok),
                  pl.BlockSpec((tm * NCH, LANES), tok),
                  pl.BlockSpec((tm, 128), tok),
                  gate_spec,
                  pl.BlockSpec(wsg.shape, c2),
                  pl.BlockSpec(wsu.shape, c2),
                  pl.BlockSpec(wsd.shape, c2),
                  pl.BlockSpec(memory_space=pl.ANY),
                  pl.BlockSpec(memory_space=pl.ANY)],
        out_specs=pl.BlockSpec((tm, d), lambda i: (i, 0)),
        scratch_shapes=[pltpu.SMEM((2 * n,), jnp.int32),
                        pltpu.VMEM((2, TOP_K, tm * NCH, LANES), F32),
                        pltpu.VMEM((tm, d), F32),
                        pltpu.SemaphoreType.DMA((2,)),
                        pltpu.SemaphoreType.DMA((2,))],
        out_shape=jax.ShapeDtypeStruct((batch * seq, d), F32),
        compiler_params=_cparams(("arbitrary",)),
        name="combine",
    )(x1, h2, gw, gt2, wsg, wsu, wsd, dst_flat, ys)


def _bias_table(rel_bias, sb, kw):
    h, n_rel = rel_bias.shape
    lo = (kw - 1) - ATT_PAST_WINDOW - (CHUNK - 1)
    ntot = sb + kw - 1
    s = jnp.concatenate([jnp.broadcast_to(rel_bias[:, :1], (h, lo)), rel_bias,
                         jnp.broadcast_to(rel_bias[:, -1:], (h, ntot - lo - n_rel))], axis=1).astype(F32)
    s_rev = jnp.pad(s[:, ::-1], ((0, 0), (0, 1)))
    skew = jnp.broadcast_to(s_rev[:, None, :], (h, sb, ntot + 1)).reshape(h, -1)[:, :sb * ntot]
    toep = skew.reshape(h, sb, ntot)[:, :, sb - 1:sb - 1 + kw]
    r = np.arange(sb)[:, None]
    kk = np.arange(kw)[None, :]
    qc, kc = r // CHUNK, kk // CHUNK
    band = (kc >= qc) & (kc <= qc + LEFT_CHUNKS)
    return jnp.where(band[None], toep, NEG_INF)


def _visit_plan(offs, m, tm, nvis):
    cnt = offs[1:] - offs[:-1]
    first = offs[:-1] // tm
    last = jnp.maximum(offs[1:] - 1, 0) // tm
    per = jnp.where(cnt > 0, last - first + 1, 0)
    vend = jnp.cumsum(per)
    vstart = vend - per
    total = vend[-1]
    v = jnp.arange(nvis, dtype=jnp.int32)
    g = jnp.minimum(jnp.sum(vend[None, :] <= v[:, None], axis=1), N_EXPERTS - 1).astype(jnp.int32)
    tile = (first[g] + v - vstart[g]).astype(jnp.int32)
    live = v < total
    last_tile = m // tm - 1
    tile = jnp.where(live, tile, last_tile)
    g = jnp.where(live, g, N_EXPERTS - 1)
    return tile, g, total.reshape(1).astype(jnp.int32)


def kernel(x_prompt, x_sample, c_prompt, c_sample, cache_k, cache_v, w_ada, b_ada, g_mix, w_in, g_gmlp_v,
           w_spatial, b_spatial, g_q, g_k, rel_bias, g_out_a, g_out_b, w_out, g_ffn, w_router, b_router,
           w_exp_gate, w_exp_up, w_exp_down, w_sh_gate, w_sh_up, w_sh_down):
    depth = w_ada.shape[0]
    assert depth == 1, "single layer"
    bp, sp, d = x_prompt.shape
    bs, ss, _ = x_sample.shape
    tp, ts = bp * sp, bs * ss
    t_all = tp + ts
    l = 0

    c_all = jnp.concatenate([c_prompt, c_sample], axis=0)
    nb = c_all.shape[0]
    nb_pad = -(-nb // 8) * 8
    c_all = jnp.pad(c_all, ((0, nb_pad - nb), (0, 0)))
    mod = _ada(c_all, w_ada[l], b_ada[l])[:nb].reshape(nb, 6, 1, d)
    sh1, sc1, gt1, sh2, sc2, gt2 = [mod[:, i] for i in range(6)]

    w_in_bf = w_in[l].astype(BF16)
    w_out_bf = w_out[l].astype(BF16)
    gv = g_gmlp_v[l].reshape(1, W_A)
    gq = jnp.tile(g_q[l], B_HEADS).reshape(1, W_B)
    gk = jnp.tile(g_k[l], B_HEADS).reshape(1, W_B)
    hid = np.arange(W_B) // HEAD_DIM
    hsum = jnp.asarray(hid[:, None] == hid[None, :], BF16)
    gmix = g_mix[l].reshape(1, d)
    goa = g_out_a[l].reshape(1, W_A)
    gob = g_out_b[l].reshape(1, W_B)
    gffn = g_ffn[l].reshape(1, d)
    bias_p = _bias_table(rel_bias[l], GMLP_BLOCK, ATT_PAST_WINDOW + GMLP_BLOCK)
    bsp_p = jnp.broadcast_to(b_spatial[l][:, :, None], (A_GROUPS, GMLP_BLOCK, A_GROUP_DIM))

    cl = cache_k.shape[2]
    us, vns, qs, ks_bf, vs_bf, k32s, v32s = _in_proj(
        x_sample, sh1[bp:], sc1[bp:], gmix, w_in_bf, gv, gq, gk, hsum,
        tm=ss, pad_rows=0, keep=ss, vn_dtype=F32)
    kcat = jnp.concatenate([cache_k[l].reshape(bs, cl, W_B).astype(BF16), ks_bf], axis=1)
    vcat = jnp.concatenate([cache_v[l].reshape(bs, cl, W_B).astype(BF16), vs_bf], axis=1)
    bias_s = bias_p[:, :ss, ATT_PAST_WINDOW - cl:ATT_PAST_WINDOW + ss]
    x1_s, h2_s = _mixer(x_sample, us, vns, qs, kcat, vcat, bias_s, w_spatial[l][:, :ss, :ss], bsp_p[:, :ss],
                        goa, gob, w_out_bf, gt1[bp:], gffn, sh2[bp:], sc2[bp:],
                        sb=ss, nsb=1, padded=False)
    new_k_sample = k32s.reshape(1, bs, ss, B_HEADS, HEAD_DIM)
    new_v_sample = v32s.reshape(1, bs, ss, B_HEADS, HEAD_DIM)
    new_gmlp_v_sample = vns.reshape(1, bs, ss, A_GROUPS, A_GROUP_DIM)

    keep = min(ATT_PAST_WINDOW, sp)
    u, vn, q, kpad, vpad, k32, v32 = _in_proj(
        x_prompt, sh1[:bp], sc1[:bp], gmix, w_in_bf, gv, gq, gk, hsum,
        tm=512, pad_rows=ATT_PAST_WINDOW, keep=keep, vn_dtype=BF16)
    x1, h2 = _mixer(x_prompt, u, vn, q, kpad, vpad, bias_p, w_spatial[l], bsp_p, goa, gob, w_out_bf,
                    gt1[:bp], gffn, sh2[:bp], sc2[:bp],
                    sb=GMLP_BLOCK, nsb=2, padded=True, tail=(x1_s, h2_s))
    assert x1.shape[0] == t_all
    new_k_prompt = k32.reshape(1, bp, keep, B_HEADS, HEAD_DIM)
    new_v_prompt = v32.reshape(1, bp, keep, B_HEADS, HEAD_DIM)

    tmr = 256
    tme = 512
    assert t_all % tmr == 0 and (t_all * TOP_K) % tme == 0 and sp % 256 == 0 and ts % (2 * ss) == 0
    assert t_all * TOP_K < (1 << RANK_BITS)
    pk, gw, cnt = _route(h2, w_router[l].astype(BF16), b_router[l], tm=tmr)
    offs = jnp.concatenate([jnp.zeros((1,), jnp.int32), jnp.cumsum(cnt[0]).astype(jnp.int32)])
    dst_flat = _slots(pk, offs, tm=tmr)[:, :TOP_K].reshape(-1)
    xs = _dispatch(h2, dst_flat, tm=tmr)
    m = t_all * TOP_K
    nvis = m // tme + N_EXPERTS - 1
    vt, vg, nv = _visit_plan(offs, m, tme, nvis)
    ys = _experts(vt, vg, offs, nv, xs, w_exp_gate[l], w_exp_up[l], w_exp_down[l], tm=tme)

    wsg, wsu, wsd = w_sh_gate[l].astype(BF16), w_sh_up[l].astype(BF16), w_sh_down[l].astype(BF16)
    y_prompt = _combine(x1, h2, gw, gt2[:bp], wsg, wsu, wsd, dst_flat, ys,
                        tm=256, batch=bp, seq=sp, row_offset=0).reshape(bp, sp, d)
    gt2_tok = jnp.broadcast_to(gt2[bp:], (bs, ss, d)).reshape(ts, d)
    y_sample = _combine(x1, h2, gw, gt2_tok, wsg, wsu, wsd, dst_flat, ys,
                        tm=2 * ss, batch=bs, seq=ss, row_offset=tp).reshape(bs, ss, d)
    return (y_prompt, y_sample, new_k_prompt, new_v_prompt, new_k_sample, new_v_sample, new_gmlp_v_sample)
```

```python
import functools

import jax
import jax.numpy as jnp
import numpy as np
from jax import lax
from jax.experimental import pallas as pl
from jax.experimental.pallas import tpu as pltpu

F32 = jnp.float32
BF16 = jnp.bfloat16

D_MODEL = 1024
CHUNK = 64
GMLP_BLOCK = 128
W_A = 512
A_GROUPS = 4
A_GROUP_DIM = 128
W_B = 512
B_HEADS = 8
HEAD_DIM = 64
LEFT_CHUNKS = 8
ATT_PAST_WINDOW = LEFT_CHUNKS * CHUNK
REL_CLIP = 128
D_IN = 2 * W_A + 3 * W_B
N_EXPERTS = 256
TOP_K = 8
D_EXPERT = 256
ROUTED_SCALE = 2.5
EPS = 1e-6
NEG_INF = -1e30

RANK_BITS = 20
VMEM_LIMIT = 56 * 1024 * 1024


def _cparams(sem, vmem=VMEM_LIMIT):
    return pltpu.CompilerParams(dimension_semantics=sem, vmem_limit_bytes=vmem)


def _rms(x, axis=-1):
    return lax.rsqrt(jnp.mean(x * x, axis=axis, keepdims=True) + EPS)


LANES = 128
NCH = D_MODEL // LANES


def _load_rows(ref, n, lead=()):
    return jnp.concatenate([ref[(*lead, pl.ds(c, n, stride=NCH), slice(None))] for c in range(NCH)], axis=1)


def _store_rows(ref, val):
    n = val.shape[0]
    for c in range(NCH):
        ref[pl.ds(c, n, stride=NCH), :] = val[:, c * LANES:(c + 1) * LANES]


def _ada_kernel(c_ref, w_ref, b_ref, o_ref):
    c = c_ref[...]
    s = c * jax.nn.sigmoid(c)
    o_ref[...] = jnp.dot(s.astype(BF16), w_ref[...].astype(BF16), preferred_element_type=F32) + b_ref[...]


def _ada(c, w_ada, b_ada):
    n, d = c.shape
    nout = w_ada.shape[1]
    tn = 1024
    return pl.pallas_call(
        _ada_kernel,
        grid=(nout // tn,),
        in_specs=[pl.BlockSpec((n, d), lambda j: (0, 0)),
                  pl.BlockSpec((d, tn), lambda j: (0, j)),
                  pl.BlockSpec((1, tn), lambda j: (0, j))],
        out_specs=pl.BlockSpec((n, tn), lambda j: (0, j)),
        out_shape=jax.ShapeDtypeStruct((n, nout), F32),
        compiler_params=_cparams(("arbitrary",)),
        name="ada",
    )(c, w_ada, b_ada.reshape(1, nout))


def _in_proj_kernel(x_ref, sh_ref, sc_ref, gmix_ref, win_ref, gv_ref, gq_ref, gk_ref, hsum_ref,
                    u_ref, vn_ref, q_ref, kp_ref, vp_ref, k32_ref, v32_ref, *, npad, first_kept):
    i = pl.program_id(1)

    @pl.when(i < npad)
    def _():
        kp_ref[...] = jnp.zeros_like(kp_ref)
        vp_ref[...] = jnp.zeros_like(vp_ref)

    @pl.when(i >= npad)
    def _():
        x = x_ref[0]
        h = x * _rms(x) * gmix_ref[...]
        h = h * (1.0 + sc_ref[0]) + sh_ref[0]
        proj = jnp.dot(h.astype(BF16), win_ref[...], preferred_element_type=F32)
        u = jax.nn.gelu(proj[:, 0:W_A])
        u_ref[0] = u.astype(u_ref.dtype)
        v = jax.nn.gelu(proj[:, W_A:2 * W_A])
        for g in range(A_GROUPS):
            vg = v[:, g * A_GROUP_DIM:(g + 1) * A_GROUP_DIM]
            vn = vg * _rms(vg) * gv_ref[:, g * A_GROUP_DIM:(g + 1) * A_GROUP_DIM]
            vn_ref[0, :, g * A_GROUP_DIM:(g + 1) * A_GROUP_DIM] = vn.astype(vn_ref.dtype)
        q = proj[:, 2 * W_A:2 * W_A + W_B]
        k = proj[:, 2 * W_A + W_B:2 * W_A + 2 * W_B]
        va = proj[:, 2 * W_A + 2 * W_B:]
        q2 = jnp.dot((q * q).astype(BF16), hsum_ref[...], preferred_element_type=F32) * (1.0 / HEAD_DIM)
        k2 = jnp.dot((k * k).astype(BF16), hsum_ref[...], preferred_element_type=F32) * (1.0 / HEAD_DIM)
        qn = q * lax.rsqrt(q2 + EPS) * gq_ref[...]
        kn = k * lax.rsqrt(k2 + EPS) * gk_ref[...]
        q_ref[0] = (qn * (HEAD_DIM ** -0.5)).astype(q_ref.dtype)
        kp_ref[0] = kn.astype(kp_ref.dtype)
        vp_ref[0] = va.astype(vp_ref.dtype)

        @pl.when(i >= first_kept)
        def _():
            k32_ref[0] = kn
            v32_ref[0] = va


def _in_proj(x, sh1, sc1, g_mix, w_in_bf, gv, gq, gk, hsum, *, tm, pad_rows, keep, vn_dtype):
    b, l, d = x.shape
    assert l % tm == 0 and pad_rows % tm == 0 and keep % tm == 0
    npad = pad_rows // tm
    nt = l // tm
    first_kept = npad + nt - keep // tm
    row = lambda bi, i: (bi, jnp.maximum(i - npad, 0), 0)
    kept = lambda bi, i: (bi, jnp.maximum(i - first_kept, 0), 0)
    bvec = lambda bi, i: (bi, 0, 0)
    const2 = lambda bi, i: (0, 0)
    out_shapes = (
        jax.ShapeDtypeStruct((b, l, W_A), BF16),
        jax.ShapeDtypeStruct((b, l, W_A), vn_dtype),
        jax.ShapeDtypeStruct((b, l, W_B), BF16),
        jax.ShapeDtypeStruct((b, pad_rows + l, W_B), BF16),
        jax.ShapeDtypeStruct((b, pad_rows + l, W_B), BF16),
        jax.ShapeDtypeStruct((b, keep, W_B), F32),
        jax.ShapeDtypeStruct((b, keep, W_B), F32),
    )
    padrow = lambda bi, i: (bi, i, 0)
    return pl.pallas_call(
        functools.partial(_in_proj_kernel, npad=npad, first_kept=first_kept),
        grid=(b, nt + npad),
        in_specs=[pl.BlockSpec((1, tm, d), row),
                  pl.BlockSpec((1, 1, d), bvec),
                  pl.BlockSpec((1, 1, d), bvec),
                  pl.BlockSpec((1, d), const2),
                  pl.BlockSpec((d, D_IN), const2),
                  pl.BlockSpec((1, W_A), const2),
                  pl.BlockSpec((1, W_B), const2),
                  pl.BlockSpec((1, W_B), const2),
                  pl.BlockSpec((W_B, W_B), const2)],
        out_specs=(pl.BlockSpec((1, tm, W_A), row),
                   pl.BlockSpec((1, tm, W_A), row),
                   pl.BlockSpec((1, tm, W_B), row),
                   pl.BlockSpec((1, tm, W_B), padrow),
                   pl.BlockSpec((1, tm, W_B), padrow),
                   pl.BlockSpec((1, tm, W_B), kept),
                   pl.BlockSpec((1, tm, W_B), kept)),
        out_shape=out_shapes,
        compiler_params=_cparams(("arbitrary", "arbitrary")),
        name="in_proj",
    )(x, sh1, sc1, g_mix, w_in_bf, gv, gq, gk, hsum)


def _mixer_kernel(x_ref, u_ref, vn_ref, q_ref, kp_ref, vp_ref, bias_ref, wsp_ref, bsp_ref,
                  goa_ref, gob_ref, wout_ref, gt1_ref, gffn_ref, sh2_ref, sc2_ref, *rest,
                  sb, nsb, kw, padded, nj, nmain):
    x1_ref, h2_ref, merged_ref = rest[-3:]
    i = pl.program_id(0)
    if len(rest) == 5:
        tx1_ref, th2_ref = rest[:2]

        @pl.when(i >= nmain)
        def _():
            x1_ref[...] = tx1_ref[...]
            h2_ref[...] = th2_ref[...]

    @pl.when(i < nmain)
    def _():
        _mixer_body(x_ref, u_ref, vn_ref, q_ref, kp_ref, vp_ref, bias_ref, wsp_ref, bsp_ref,
                    goa_ref, gob_ref, wout_ref, gt1_ref, gffn_ref, sh2_ref, sc2_ref,
                    x1_ref, h2_ref, merged_ref, i % nj, sb=sb, nsb=nsb, kw=kw, padded=padded)


def _mixer_body(x_ref, u_ref, vn_ref, q_ref, kp_ref, vp_ref, bias_ref, wsp_ref, bsp_ref,
                goa_ref, gob_ref, wout_ref, gt1_ref, gffn_ref, sh2_ref, sc2_ref,
                x1_ref, h2_ref, merged_ref, j, *, sb, nsb, kw, padded):
    rows = sb * nsb
    ri = lax.broadcasted_iota(jnp.int32, (sb, sb), 0)
    ci = lax.broadcasted_iota(jnp.int32, (sb, sb), 1)
    gmask = (ci // CHUNK) <= (ri // CHUNK)
    lane = lax.broadcasted_iota(jnp.int32, (sb, 2 * HEAD_DIM), 1)
    lo_half = lane < HEAD_DIM
    kidx = lax.broadcasted_iota(jnp.int32, (sb, kw), 1)

    for s in range(nsb):
        r0 = s * sb
        vn = vn_ref[0, r0:r0 + sb, :].astype(BF16)
        u = u_ref[0, r0:r0 + sb, :].astype(F32)
        zs = []
        for g in range(A_GROUPS):
            w = jnp.where(gmask, wsp_ref[g], 0.0).astype(BF16)
            z = jnp.dot(w, vn[:, g * A_GROUP_DIM:(g + 1) * A_GROUP_DIM], preferred_element_type=F32)
            zs.append(z + bsp_ref[g])
        a = u * jnp.concatenate(zs, axis=1)
        a = a * _rms(a) * goa_ref[...]
        merged_ref[r0:r0 + sb, 0:W_A] = a.astype(BF16)

        if padded:
            blk0 = j * rows + r0
            kstart = pl.multiple_of(blk0, sb)
            kb = kp_ref[0, pl.ds(kstart, kw), :]
            vb = vp_ref[0, pl.ds(kstart, kw), :]
            valid = kidx >= (ATT_PAST_WINDOW - blk0)
        else:
            kb = kp_ref[0]
            vb = vp_ref[0]
            valid = None
        qb = q_ref[0, r0:r0 + sb, :]
        outs = []
        for hp in range(B_HEADS // 2):
            c0 = hp * 2 * HEAD_DIM
            qp = qb[:, c0:c0 + 2 * HEAD_DIM]
            kp = kb[:, c0:c0 + 2 * HEAD_DIM]
            vp = vb[:, c0:c0 + 2 * HEAD_DIM]
            o_pair = []
            for hh in range(2):
                qm = jnp.where(lo_half if hh == 0 else jnp.logical_not(lo_half), qp, jnp.zeros_like(qp))
                sc = lax.dot_general(qm, kp, (((1,), (1,)), ((), ())), preferred_element_type=F32)
                sc = sc + bias_ref[2 * hp + hh]
                if valid is not None:
                    sc = jnp.where(valid, sc, NEG_INF)
                m = jnp.max(sc, axis=-1, keepdims=True)
                p = jnp.exp(sc - m)
                l = jnp.sum(p, axis=-1, keepdims=True)
                o = jnp.dot(p.astype(BF16), vp, preferred_element_type=F32)
                o_pair.append(o / l)
            outs.append(jnp.where(lo_half, o_pair[0], o_pair[1]))
        bo = jnp.concatenate(outs, axis=1)
        bo = bo * _rms(bo) * gob_ref[...]
        merged_ref[r0:r0 + sb, W_A:W_A + W_B] = bo.astype(BF16)

    mix = jnp.dot(merged_ref[...], wout_ref[...], preferred_element_type=F32)
    x1 = x_ref[0] + gt1_ref[0] * mix
    x1_ref[...] = x1
    h2 = x1 * _rms(x1) * gffn_ref[...]
    _store_rows(h2_ref, h2 * (1.0 + sc2_ref[0]) + sh2_ref[0])


def _mixer(x, u, vn, q, kp, vp, bias, wsp, bsp, goa, gob, wout_bf, gt1, gffn, sh2, sc2,
           *, sb, nsb, padded, tail=None):
    b, l, d = x.shape
    rows = sb * nsb
    assert l % rows == 0
    nj = l // rows
    nmain = b * nj
    ntail = 0
    if tail is not None:
        assert tail[0].shape[0] % rows == 0
        ntail = tail[0].shape[0] // rows
    total_rows = (nmain + ntail) * rows
    kw = bias.shape[2]
    kl = kp.shape[1]
    bidx = lambda i: jnp.minimum(i // nj, b - 1)
    row = lambda i: (bidx(i), i % nj, 0)
    bvec = lambda i: (bidx(i), 0, 0)
    c2 = lambda i: (0, 0)
    c3 = lambda i: (0, 0, 0)
    pool = lambda i: (i, 0)
    in_specs = [pl.BlockSpec((1, rows, d), row),
                pl.BlockSpec((1, rows, W_A), row),
                pl.BlockSpec((1, rows, W_A), row),
                pl.BlockSpec((1, rows, W_B), row),
                pl.BlockSpec((1, kl, W_B), bvec),
                pl.BlockSpec((1, kl, W_B), bvec),
                pl.BlockSpec(bias.shape, c3),
                pl.BlockSpec(wsp.shape, c3),
                pl.BlockSpec(bsp.shape, c3),
                pl.BlockSpec((1, W_A), c2),
                pl.BlockSpec((1, W_B), c2),
                pl.BlockSpec((d, d), c2),
                pl.BlockSpec((1, 1, d), bvec),
                pl.BlockSpec((1, d), c2),
                pl.BlockSpec((1, 1, d), bvec),
                pl.BlockSpec((1, 1, d), bvec)]
    args = [x, u, vn, q, kp, vp, bias, wsp, bsp, goa, gob, wout_bf, gt1, gffn, sh2, sc2]
    if tail is not None:
        tmap = lambda i: (jnp.maximum(i - nmain, 0), 0)
        in_specs += [pl.BlockSpec((rows, d), tmap), pl.BlockSpec((rows * NCH, LANES), tmap)]
        args += list(tail)
    return pl.pallas_call(
        functools.partial(_mixer_kernel, sb=sb, nsb=nsb, kw=kw, padded=padded, nj=nj, nmain=nmain),
        grid=(nmain + ntail,),
        in_specs=in_specs,
        out_specs=(pl.BlockSpec((rows, d), pool), pl.BlockSpec((rows * NCH, LANES), pool)),
        out_shape=(jax.ShapeDtypeStruct((total_rows, d), F32),
                   jax.ShapeDtypeStruct((total_rows * NCH, LANES), F32)),
        scratch_shapes=[pltpu.VMEM((rows, d), BF16)],
        compiler_params=_cparams(("arbitrary",)),
        name="mixer",
    )(*args)


def _route_kernel(h2_ref, wrt_ref, br_ref, pk_ref, gw_ref, cnt_ref, run_ref, *, tm):
    i = pl.program_id(0)

    @pl.when(i == 0)
    def _():
        run_ref[...] = jnp.zeros_like(run_ref)

    x = _load_rows(h2_ref, tm).astype(BF16)
    logits = lax.dot_general(wrt_ref[...], x, (((1,), (1,)), ((), ())), preferred_element_type=F32)
    aff = jax.nn.sigmoid(logits)
    sel = aff + br_ref[...]
    row = lax.broadcasted_iota(jnp.int32, (N_EXPERTS, tm), 0)
    picks = []
    onehot = jnp.zeros((N_EXPERTS, tm), F32)
    for _ in range(TOP_K):
        m = jnp.max(sel, axis=0, keepdims=True)
        ik = jnp.min(jnp.where(sel == m, row, N_EXPERTS), axis=0, keepdims=True)
        oh = row == ik
        sel = jnp.where(oh, -jnp.inf, sel)
        onehot = jnp.where(oh, 1.0, onehot)
        picks.append((ik, oh))
    ri = lax.broadcasted_iota(jnp.int32, (tm, tm), 0)
    ci = lax.broadcasted_iota(jnp.int32, (tm, tm), 1)
    earlier = (ri < ci).astype(BF16)
    before = jnp.dot(onehot.astype(BF16), earlier, preferred_element_type=F32) + run_ref[...]
    affs = [jnp.sum(jnp.where(oh, aff, 0.0), axis=0, keepdims=True) for _, oh in picks]
    tot = affs[0]
    for a in affs[1:]:
        tot = tot + a
    for k, (ik, oh) in enumerate(picks):
        rank = jnp.sum(jnp.where(oh, before, 0.0), axis=0, keepdims=True).astype(jnp.int32)
        pk_ref[k:k + 1, :] = ik * (1 << RANK_BITS) + rank
        gw_ref[k:k + 1, :] = affs[k] / tot * ROUTED_SCALE
    run = run_ref[...] + jnp.sum(onehot, axis=1, keepdims=True)
    run_ref[...] = run
    cnt_ref[...] = run.astype(jnp.int32)


def _route(h2, wrt_bf, b_router, *, tm):
    t, d = h2.shape[0] // NCH, D_MODEL
    return pl.pallas_call(
        functools.partial(_route_kernel, tm=tm),
        grid=(t // tm,),
        in_specs=[pl.BlockSpec((tm * NCH, LANES), lambda i: (i, 0)),
                  pl.BlockSpec((N_EXPERTS, d), lambda i: (0, 0)),
                  pl.BlockSpec((N_EXPERTS, 1), lambda i: (0, 0))],
        out_specs=(pl.BlockSpec((TOP_K, tm), lambda i: (0, i)),
                   pl.BlockSpec((TOP_K, tm), lambda i: (0, i)),
                   pl.BlockSpec((N_EXPERTS, 1), lambda i: (0, 0))),
        out_shape=(jax.ShapeDtypeStruct((TOP_K, t), jnp.int32),
                   jax.ShapeDtypeStruct((TOP_K, t), F32),
                   jax.ShapeDtypeStruct((N_EXPERTS, 1), jnp.int32)),
        scratch_shapes=[pltpu.VMEM((N_EXPERTS, 1), F32)],
        compiler_params=_cparams(("arbitrary",)),
        name="route",
    )(h2, wrt_bf, b_router.reshape(N_EXPERTS, 1))


def _idx_copy(idx_hbm, idx_smem, sem, step, n, half):
    return pltpu.make_async_copy(idx_hbm.at[pl.ds(pl.multiple_of(step * n, n), n)],
                                 idx_smem.at[pl.ds(half * n, n)], sem.at[half])


def _slots_kernel(pk_ref, offs_ref, dst_ref):
    half = N_EXPERTS // 2
    shape = (TOP_K, LANES)
    lo_tab = jnp.broadcast_to(offs_ref[:, 0:half], shape)
    hi_tab = jnp.broadcast_to(offs_ref[:, half:N_EXPERTS], shape)
    for j in range(pk_ref.shape[1] // LANES):
        lanes = slice(j * LANES, (j + 1) * LANES)
        p = pk_ref[:, lanes]
        e = p >> RANK_BITS
        start = jnp.where(e >= half, jnp.take_along_axis(hi_tab, e & (half - 1), axis=1),
                          jnp.take_along_axis(lo_tab, e & (half - 1), axis=1))
        dst_ref[:, lanes] = ((p & ((1 << RANK_BITS) - 1)) + start) * NCH


def _slots(pk, offs, *, tm):
    t = pk.shape[1]
    return pl.pallas_call(
        _slots_kernel,
        grid=(t // tm,),
        in_specs=[pl.BlockSpec((TOP_K, tm), lambda i: (0, i)),
                  pl.BlockSpec((1, N_EXPERTS), lambda i: (0, 0))],
        out_specs=pl.BlockSpec((TOP_K, tm), lambda i: (0, i)),
        out_shape=jax.ShapeDtypeStruct((TOP_K, t), jnp.int32),
        compiler_params=_cparams(("arbitrary",)),
        name="slots",
    )(pk, offs[:N_EXPERTS].reshape(1, N_EXPERTS))


def _dispatch_kernel(h2_ref, dst_hbm, xs_hbm, idx_smem, idx_sem, row_sem, *, tm):
    i = pl.program_id(0)
    n = tm * TOP_K

    @pl.when(i == 0)
    def _():
        _idx_copy(dst_hbm, idx_smem, idx_sem, i, n, 0).start()

    def scatter(half):
        _idx_copy(dst_hbm, idx_smem, idx_sem, i, n, half).wait()

        @pl.when(i + 1 < pl.num_programs(0))
        def _():
            _idx_copy(dst_hbm, idx_smem, idx_sem, i + 1, n, 1 - half).start()

        def body(t8, carry):
            r0 = pl.multiple_of(t8 * (8 * NCH), 8 * NCH)
            b = half * n + t8 * (8 * TOP_K)
            for tt in range(8):
                src = h2_ref.at[pl.ds(r0 + tt * NCH, NCH)]
                for k in range(TOP_K):
                    dst = pl.multiple_of(idx_smem[b + tt * TOP_K + k], NCH)
                    pltpu.make_async_copy(src, xs_hbm.at[pl.ds(dst, NCH)], row_sem).start(priority=k % 2)
            return carry

        lax.fori_loop(0, tm // 8, body, 0)

    for half in range(2):
        pl.when(i % 2 == half)(functools.partial(scatter, half))
    pltpu.make_async_copy(xs_hbm.at[pl.ds(0, n * NCH)], xs_hbm.at[pl.ds(0, n * NCH)], row_sem).wait()


def _dispatch(h2, dst_flat, *, tm):
    t = h2.shape[0] // NCH
    n = tm * TOP_K
    return pl.pallas_call(
        functools.partial(_dispatch_kernel, tm=tm),
        grid=(t // tm,),
        in_specs=[pl.BlockSpec((tm * NCH, LANES), lambda i: (i, 0)),
                  pl.BlockSpec(memory_space=pl.ANY)],
        out_specs=pl.BlockSpec(memory_space=pl.ANY),
        scratch_shapes=[pltpu.SMEM((2 * n,), jnp.int32),
                        pltpu.SemaphoreType.DMA((2,)),
                        pltpu.SemaphoreType.DMA(())],
        out_shape=jax.ShapeDtypeStruct((t * TOP_K * NCH, LANES), F32),
        compiler_params=_cparams(("arbitrary",)),
        name="dispatch",
    )(h2, dst_flat)


def _experts_kernel(vt_ref, vg_ref, offs_ref, nv_ref, xs_ref, wg_ref, wu_ref, wd_ref, ys_ref,
                    wgb_ref, wub_ref, wdb_ref, *, tm):
    v = pl.program_id(0)

    @pl.when(v < nv_ref[0])
    def _():
        g = vg_ref[v]
        tile = vt_ref[v]
        prev = jnp.maximum(v - 1, 0)
        first = jnp.logical_or(v == 0, vt_ref[prev] != tile)

        @pl.when(jnp.logical_or(v == 0, vg_ref[prev] != g))
        def _():
            wgb_ref[...] = wg_ref[0].astype(BF16)
            wub_ref[...] = wu_ref[0].astype(BF16)
            wdb_ref[...] = wd_ref[0].astype(BF16)

        def swiglu_rows(r0, nr, masked):
            x = jnp.concatenate([xs_ref[pl.ds(r0 * NCH + c, nr, stride=NCH), :] for c in range(NCH)],
                                axis=1).astype(BF16)
            hg = jnp.dot(x, wgb_ref[...], preferred_element_type=F32)
            hu = jnp.dot(x, wub_ref[...], preferred_element_type=F32)
            hm = (hg * jax.nn.sigmoid(hg) * hu).astype(BF16)
            if masked:
                rowid = tile * tm + r0 + lax.broadcasted_iota(jnp.int32, (nr, 1), 0)
                mine = jnp.logical_and(rowid >= offs_ref[g], rowid < offs_ref[g + 1])
            for c2 in range(NCH // 2):
                y2 = jnp.dot(hm, wdb_ref[:, c2 * 2 * LANES:(c2 + 1) * 2 * LANES], preferred_element_type=F32)
                for j in range(2):
                    dst = pl.ds(r0 * NCH + 2 * c2 + j, nr, stride=NCH)
                    yc = y2[:, j * LANES:(j + 1) * LANES]
                    ys_ref[dst, :] = jnp.where(mine, yc, ys_ref[dst, :]) if masked else yc

        def visit(masked):
            for r0 in range(0, tm, tm // 2):
                swiglu_rows(r0, tm // 2, masked)

        pl.when(first)(functools.partial(visit, False))
        pl.when(jnp.logical_not(first))(functools.partial(visit, True))


def _experts(vt, vg, offs, nv, xs, w_gate, w_up, w_down, *, tm):
    m, d = xs.shape[0] // NCH, D_MODEL
    nvis = vt.shape[0]
    de = w_gate.shape[2]
    xmap = lambda v, vt, vg, offs, nv: (vt[v], 0)
    wmap = lambda v, vt, vg, offs, nv: (vg[v], 0, 0)
    grid_spec = pltpu.PrefetchScalarGridSpec(
        num_scalar_prefetch=4,
        grid=(nvis,),
        in_specs=[pl.BlockSpec((tm * NCH, LANES), xmap),
                  pl.BlockSpec((1, d, de), wmap),
                  pl.BlockSpec((1, d, de), wmap),
                  pl.BlockSpec((1, de, d), wmap)],
        out_specs=pl.BlockSpec((tm * NCH, LANES), xmap),
        scratch_shapes=[pltpu.VMEM((d, de), BF16), pltpu.VMEM((d, de), BF16), pltpu.VMEM((de, d), BF16)],
    )
    return pl.pallas_call(
        functools.partial(_experts_kernel, tm=tm),
        grid_spec=grid_spec,
        out_shape=jax.ShapeDtypeStruct((m * NCH, LANES), F32),
        compiler_params=_cparams(("arbitrary",)),
        name="experts",
    )(vt, vg, offs, nv, xs, w_gate, w_up, w_down)


def _combine_kernel(x1_ref, h2_ref, gw_ref, gt2_ref, wsg_ref, wsu_ref, wsd_ref, dst_hbm, ys_hbm,
                    y_ref, idx_smem, buf_ref, acc_ref, idx_sem, row_sem, *, tm, step0, per_token_gate):
    i = pl.program_id(0)
    nsteps = pl.num_programs(0)
    n = tm * TOP_K

    def issue(b, r0, half):
        for tt in range(8):
            for k in range(TOP_K):
                src = pl.multiple_of(idx_smem[b + tt * TOP_K + k], NCH)
                pltpu.make_async_copy(ys_hbm.at[pl.ds(src, NCH)],
                                      buf_ref.at[half, k, pl.ds(r0 + tt * NCH, NCH)],
                                      row_sem.at[half]).start(priority=k % 2)

    def finish8(t8, cur):
        t0 = pl.multiple_of(t8 * 8, 8)
        gw = gw_ref[pl.ds(t0, 8), :]
        for c in range(NCH):
            lanes = slice(c * LANES, (c + 1) * LANES)
            ffn = acc_ref[pl.ds(t0, 8), lanes]
            for k in range(TOP_K):
                ffn = ffn + gw[:, k:k + 1] * buf_ref[cur, k, pl.ds(t0 * NCH + c, 8, stride=NCH), :]
            gate = gt2_ref[pl.ds(t0, 8), lanes] if per_token_gate else gt2_ref[0][:, lanes]
            y_ref[pl.ds(t0, 8), lanes] = x1_ref[pl.ds(t0, 8), lanes] + gate * ffn

    @pl.when(i == 0)
    def _():
        _idx_copy(dst_hbm, idx_smem, idx_sem, step0, n, 0).start()
        _idx_copy(dst_hbm, idx_smem, idx_sem, step0, n, 0).wait()

        def body(t8, carry):
            issue(t8 * (8 * TOP_K), pl.multiple_of(t8 * (8 * NCH), 8 * NCH), 0)
            return carry

        lax.fori_loop(0, tm // 8, body, 0)

        @pl.when(nsteps > 1)
        def _():
            _idx_copy(dst_hbm, idx_smem, idx_sem, step0 + 1, n, 1).start()

    h2 = _load_rows(h2_ref, tm).astype(BF16)
    sg = jnp.dot(h2, wsg_ref[...], preferred_element_type=F32)
    su = jnp.dot(h2, wsu_ref[...], preferred_element_type=F32)
    acc_ref[...] = jnp.dot((sg * jax.nn.sigmoid(sg) * su).astype(BF16), wsd_ref[...], preferred_element_type=F32)

    def step(cur, has_next):
        pltpu.make_async_copy(ys_hbm.at[pl.ds(0, n * NCH)], ys_hbm.at[pl.ds(0, n * NCH)], row_sem.at[cur]).wait()
        if has_next:
            _idx_copy(dst_hbm, idx_smem, idx_sem, step0 + i + 1, n, 1 - cur).wait()

        def body(t8, carry):
            if has_next:
                issue((1 - cur) * n + t8 * (8 * TOP_K), pl.multiple_of(t8 * (8 * NCH), 8 * NCH), 1 - cur)
            finish8(t8, cur)
            return carry

        lax.fori_loop(0, tm // 8, body, 0)

        if has_next:
            @pl.when(i + 2 < nsteps)
            def _():
                _idx_copy(dst_hbm, idx_smem, idx_sem, step0 + i + 2, n, cur).start()

    for cur in range(2):
        pl.when(jnp.logical_and(i % 2 == cur, i + 1 < nsteps))(functools.partial(step, cur, True))
        pl.when(jnp.logical_and(i % 2 == cur, i + 1 == nsteps))(functools.partial(step, cur, False))


def _combine(x1, h2, gw, gt2, wsg, wsu, wsd, dst_flat, ys, *, tm, batch, seq, row_offset):
    d = x1.shape[1]
    n = tm * TOP_K
    per_token_gate = gt2.ndim == 2
    assert per_token_gate or seq % tm == 0
    per_b = max(seq // tm, 1)
    step0 = row_offset // tm
    tok = lambda i: (step0 + i, 0)
    c2 = lambda i: (0, 0)
    gate_spec = (pl.BlockSpec((tm, d), lambda i: (i, 0)) if per_token_gate
                 else pl.BlockSpec((1, 1, d), lambda i: (i // per_b, 0, 0)))
    return pl.pallas_call(
        functools.partial(_combine_kernel, tm=tm, step0=step0, per_token_gate=per_token_gate),
        grid=(batch * seq // tm,),
        in_specs=[pl.BlockSpec((tm, d), tok),
                  pl.BlockSpec((tm * NCH, LANES), tok),
                  pl.BlockSpec((tm, 128), tok),
                  gate_spec,
                  pl.BlockSpec(wsg.shape, c2),
                  pl.BlockSpec(wsu.shape, c2),
                  pl.BlockSpec(wsd.shape, c2),
                  pl.BlockSpec(memory_space=pl.ANY),
                  pl.BlockSpec(memory_space=pl.ANY)],
        out_specs=pl.BlockSpec((tm, d), lambda i: (i, 0)),
        scratch_shapes=[pltpu.SMEM((2 * n,), jnp.int32),
                        pltpu.VMEM((2, TOP_K, tm * NCH, LANES), F32),
                        pltpu.VMEM((tm, d), F32),
                        pltpu.SemaphoreType.DMA((2,)),
                        pltpu.SemaphoreType.DMA((2,))],
        out_shape=jax.ShapeDtypeStruct((batch * seq, d), F32),
        compiler_params=_cparams(("arbitrary",)),
        name="combine",
    )(x1, h2, gw, gt2, wsg, wsu, wsd, dst_flat, ys)


def _bias_table(rel_bias, sb, kw):
    h, n_rel = rel_bias.shape
    lo = (kw - 1) - ATT_PAST_WINDOW - (CHUNK - 1)
    ntot = sb + kw - 1
    s = jnp.concatenate([jnp.broadcast_to(rel_bias[:, :1], (h, lo)), rel_bias,
                         jnp.broadcast_to(rel_bias[:, -1:], (h, ntot - lo - n_rel))], axis=1).astype(F32)
    s_rev = jnp.pad(s[:, ::-1], ((0, 0), (0, 1)))
    skew = jnp.broadcast_to(s_rev[:, None, :], (h, sb, ntot + 1)).reshape(h, -1)[:, :sb * ntot]
    toep = skew.reshape(h, sb, ntot)[:, :, sb - 1:sb - 1 + kw]
    r = np.arange(sb)[:, None]
    kk = np.arange(kw)[None, :]
    qc, kc = r // CHUNK, kk // CHUNK
    band = (kc >= qc) & (kc <= qc + LEFT_CHUNKS)
    return jnp.where(band[None], toep, NEG_INF)


def _visit_plan(offs, m, tm, nvis):
    cnt = offs[1:] - offs[:-1]
    first = offs[:-1] // tm
    last = jnp.maximum(offs[1:] - 1, 0) // tm
    per = jnp.where(cnt > 0, last - first + 1, 0)
    vend = jnp.cumsum(per)
    vstart = vend - per
    total = vend[-1]
    v = jnp.arange(nvis, dtype=jnp.int32)
    g = jnp.minimum(jnp.sum(vend[None, :] <= v[:, None], axis=1), N_EXPERTS - 1).astype(jnp.int32)
    tile = (first[g] + v - vstart[g]).astype(jnp.int32)
    live = v < total
    last_tile = m // tm - 1
    tile = jnp.where(live, tile, last_tile)
    g = jnp.where(live, g, N_EXPERTS - 1)
    return tile, g, total.reshape(1).astype(jnp.int32)


def kernel(x_prompt, x_sample, c_prompt, c_sample, cache_k, cache_v, w_ada, b_ada, g_mix, w_in, g_gmlp_v,
           w_spatial, b_spatial, g_q, g_k, rel_bias, g_out_a, g_out_b, w_out, g_ffn, w_router, b_router,
           w_exp_gate, w_exp_up, w_exp_down, w_sh_gate, w_sh_up, w_sh_down):
    depth = w_ada.shape[0]
    assert depth == 1, "single layer"
    bp, sp, d = x_prompt.shape
    bs, ss, _ = x_sample.shape
    tp, ts = bp * sp, bs * ss
    t_all = tp + ts
    l = 0

    c_all = jnp.concatenate([c_prompt, c_sample], axis=0)
    nb = c_all.shape[0]
    nb_pad = -(-nb // 8) * 8
    c_all = jnp.pad(c_all, ((0, nb_pad - nb), (0, 0)))
    mod = _ada(c_all, w_ada[l], b_ada[l])[:nb].reshape(nb, 6, 1, d)
    sh1, sc1, gt1, sh2, sc2, gt2 = [mod[:, i] for i in range(6)]

    w_in_bf = w_in[l].astype(BF16)
    w_out_bf = w_out[l].astype(BF16)
    gv = g_gmlp_v[l].reshape(1, W_A)
    gq = jnp.tile(g_q[l], B_HEADS).reshape(1, W_B)
    gk = jnp.tile(g_k[l], B_HEADS).reshape(1, W_B)
    hid = np.arange(W_B) // HEAD_DIM
    hsum = jnp.asarray(hid[:, None] == hid[None, :], BF16)
    gmix = g_mix[l].reshape(1, d)
    goa = g_out_a[l].reshape(1, W_A)
    gob = g_out_b[l].reshape(1, W_B)
    gffn = g_ffn[l].reshape(1, d)
    bias_p = _bias_table(rel_bias[l], GMLP_BLOCK, ATT_PAST_WINDOW + GMLP_BLOCK)
    bsp_p = jnp.broadcast_to(b_spatial[l][:, :, None], (A_GROUPS, GMLP_BLOCK, A_GROUP_DIM))

    cl = cache_k.shape[2]
    us, vns, qs, ks_bf, vs_bf, k32s, v32s = _in_proj(
        x_sample, sh1[bp:], sc1[bp:], gmix, w_in_bf, gv, gq, gk, hsum,
        tm=ss, pad_rows=0, keep=ss, vn_dtype=F32)
    kcat = jnp.concatenate([cache_k[l].reshape(bs, cl, W_B).astype(BF16), ks_bf], axis=1)
    vcat = jnp.concatenate([cache_v[l].reshape(bs, cl, W_B).astype(BF16), vs_bf], axis=1)
    bias_s = bias_p[:, :ss, ATT_PAST_WINDOW - cl:ATT_PAST_WINDOW + ss]
    x1_s, h2_s = _mixer(x_sample, us, vns, qs, kcat, vcat, bias_s, w_spatial[l][:, :ss, :ss], bsp_p[:, :ss],
                        goa, gob, w_out_bf, gt1[bp:], gffn, sh2[bp:], sc2[bp:],
                        sb=ss, nsb=1, padded=False)
    new_k_sample = k32s.reshape(1, bs, ss, B_HEADS, HEAD_DIM)
    new_v_sample = v32s.reshape(1, bs, ss, B_HEADS, HEAD_DIM)
    new_gmlp_v_sample = vns.reshape(1, bs, ss, A_GROUPS, A_GROUP_DIM)

    keep = min(ATT_PAST_WINDOW, sp)
    u, vn, q, kpad, vpad, k32, v32 = _in_proj(
        x_prompt, sh1[:bp], sc1[:bp], gmix, w_in_bf, gv, gq, gk, hsum,
        tm=512, pad_rows=ATT_PAST_WINDOW, keep=keep, vn_dtype=BF16)
    x1, h2 = _mixer(x_prompt, u, vn, q, kpad, vpad, bias_p, w_spatial[l], bsp_p, goa, gob, w_out_bf,
                    gt1[:bp], gffn, sh2[:bp], sc2[:bp],
                    sb=GMLP_BLOCK, nsb=2, padded=True, tail=(x1_s, h2_s))
    assert x1.shape[0] == t_all
    new_k_prompt = k32.reshape(1, bp, keep, B_HEADS, HEAD_DIM)
    new_v_prompt = v32.reshape(1, bp, keep, B_HEADS, HEAD_DIM)

    tmr = 256
    tme = 512
    assert t_all % tmr == 0 and (t_all * TOP_K) % tme == 0 and sp % 256 == 0 and ts % (2 * ss) == 0
    assert t_all * TOP_K < (1 << RANK_BITS)
    pk, gw_t, cnt = _route(h2, w_router[l].T.astype(BF16), b_router[l], tm=tmr)
    offs = jnp.concatenate([jnp.zeros((1,), jnp.int32), jnp.cumsum(cnt[:, 0]).astype(jnp.int32)])
    tms = next(c for c in (2048, 1536, 1024, 512, 256) if t_all % c == 0)
    dst_flat = _slots(pk, offs, tm=tms).T.reshape(-1)
    gw = jnp.pad(gw_t.T, ((0, 0), (0, 128 - TOP_K)))
    xs = _dispatch(h2, dst_flat, tm=tmr)
    m = t_all * TOP_K
    nvis = m // tme + N_EXPERTS - 1
    vt, vg, nv = _visit_plan(offs, m, tme, nvis)
    ys = _experts(vt, vg, offs, nv, xs, w_exp_gate[l], w_exp_up[l], w_exp_down[l], tm=tme)

    wsg, wsu, wsd = w_sh_gate[l].astype(BF16), w_sh_up[l].astype(BF16), w_sh_down[l].astype(BF16)
    y_prompt = _combine(x1, h2, gw, gt2[:bp], wsg, wsu, wsd, dst_flat, ys,
                        tm=256, batch=bp, seq=sp, row_offset=0).reshape(bp, sp, d)
    gt2_tok = jnp.broadcast_to(gt2[bp:], (bs, ss, d)).reshape(ts, d)
    y_sample = _combine(x1, h2, gw, gt2_tok, wsg, wsu, wsd, dst_flat, ys,
                        tm=2 * ss, batch=bs, seq=ss, row_offset=tp).reshape(bs, ss, d)
    return (y_prompt, y_sample, new_k_prompt, new_v_prompt, new_k_sample, new_v_sample, new_gmlp_v_sample)
```

```python
import functools

import jax
import jax.numpy as jnp
import numpy as np
from jax import lax
from jax.experimental import pallas as pl
from jax.experimental.pallas import tpu as pltpu

F32 = jnp.float32
BF16 = jnp.bfloat16

D_MODEL = 1024
CHUNK = 64
GMLP_BLOCK = 128
W_A = 512
A_GROUPS = 4
A_GROUP_DIM = 128
W_B = 512
B_HEADS = 8
HEAD_DIM = 64
LEFT_CHUNKS = 8
ATT_PAST_WINDOW = LEFT_CHUNKS * CHUNK
REL_CLIP = 128
D_IN = 2 * W_A + 3 * W_B
N_EXPERTS = 256
TOP_K = 8
D_EXPERT = 256
ROUTED_SCALE = 2.5
EPS = 1e-6
NEG_INF = -1e30

RANK_BITS = 20
VMEM_LIMIT = 56 * 1024 * 1024


def _cparams(sem, vmem=VMEM_LIMIT):
    return pltpu.CompilerParams(dimension_semantics=sem, vmem_limit_bytes=vmem)


def _rms(x, axis=-1):
    return lax.rsqrt(jnp.mean(x * x, axis=axis, keepdims=True) + EPS)


LANES = 128
NCH = D_MODEL // LANES


def _load_rows(ref, n, lead=()):
    return jnp.concatenate([ref[(*lead, pl.ds(c, n, stride=NCH), slice(None))] for c in range(NCH)], axis=1)


def _store_rows(ref, val):
    n = val.shape[0]
    for c in range(NCH):
        ref[pl.ds(c, n, stride=NCH), :] = val[:, c * LANES:(c + 1) * LANES]


def _ada_kernel(c_ref, w_ref, b_ref, o_ref):
    c = c_ref[...]
    s = c * jax.nn.sigmoid(c)
    o_ref[...] = jnp.dot(s.astype(BF16), w_ref[...].astype(BF16), preferred_element_type=F32) + b_ref[...]


def _ada(c, w_ada, b_ada):
    n, d = c.shape
    nout = w_ada.shape[1]
    tn = 1024
    return pl.pallas_call(
        _ada_kernel,
        grid=(nout // tn,),
        in_specs=[pl.BlockSpec((n, d), lambda j: (0, 0)),
                  pl.BlockSpec((d, tn), lambda j: (0, j)),
                  pl.BlockSpec((1, tn), lambda j: (0, j))],
        out_specs=pl.BlockSpec((n, tn), lambda j: (0, j)),
        out_shape=jax.ShapeDtypeStruct((n, nout), F32),
        compiler_params=_cparams(("arbitrary",)),
        name="ada",
    )(c, w_ada, b_ada.reshape(1, nout))


def _in_proj_kernel(x_ref, sh_ref, sc_ref, gmix_ref, win_ref, gv_ref, gq_ref, gk_ref, hsum_ref,
                    u_ref, vn_ref, q_ref, kp_ref, vp_ref, k32_ref, v32_ref, *, npad, first_kept):
    i = pl.program_id(1)

    @pl.when(i < npad)
    def _():
        kp_ref[...] = jnp.zeros_like(kp_ref)
        vp_ref[...] = jnp.zeros_like(vp_ref)

    @pl.when(i >= npad)
    def _():
        x = x_ref[0]
        h = x * _rms(x) * gmix_ref[...]
        h = h * (1.0 + sc_ref[0]) + sh_ref[0]
        proj = jnp.dot(h.astype(BF16), win_ref[...], preferred_element_type=F32)
        u = jax.nn.gelu(proj[:, 0:W_A])
        u_ref[0] = u.astype(u_ref.dtype)
        v = jax.nn.gelu(proj[:, W_A:2 * W_A])
        for g in range(A_GROUPS):
            vg = v[:, g * A_GROUP_DIM:(g + 1) * A_GROUP_DIM]
            vn = vg * _rms(vg) * gv_ref[:, g * A_GROUP_DIM:(g + 1) * A_GROUP_DIM]
            vn_ref[0, :, g * A_GROUP_DIM:(g + 1) * A_GROUP_DIM] = vn.astype(vn_ref.dtype)
        q = proj[:, 2 * W_A:2 * W_A + W_B]
        k = proj[:, 2 * W_A + W_B:2 * W_A + 2 * W_B]
        va = proj[:, 2 * W_A + 2 * W_B:]
        q2 = jnp.dot((q * q).astype(BF16), hsum_ref[...], preferred_element_type=F32) * (1.0 / HEAD_DIM)
        k2 = jnp.dot((k * k).astype(BF16), hsum_ref[...], preferred_element_type=F32) * (1.0 / HEAD_DIM)
        qn = q * lax.rsqrt(q2 + EPS) * gq_ref[...]
        kn = k * lax.rsqrt(k2 + EPS) * gk_ref[...]
        q_ref[0] = (qn * (HEAD_DIM ** -0.5)).astype(q_ref.dtype)
        kp_ref[0] = kn.astype(kp_ref.dtype)
        vp_ref[0] = va.astype(vp_ref.dtype)

        @pl.when(i >= first_kept)
        def _():
            k32_ref[0] = kn
            v32_ref[0] = va


def _in_proj(x, sh1, sc1, g_mix, w_in_bf, gv, gq, gk, hsum, *, tm, pad_rows, keep, vn_dtype):
    b, l, d = x.shape
    assert l % tm == 0 and pad_rows % tm == 0 and keep % tm == 0
    npad = pad_rows // tm
    nt = l // tm
    first_kept = npad + nt - keep // tm
    row = lambda bi, i: (bi, jnp.maximum(i - npad, 0), 0)
    kept = lambda bi, i: (bi, jnp.maximum(i - first_kept, 0), 0)
    bvec = lambda bi, i: (bi, 0, 0)
    const2 = lambda bi, i: (0, 0)
    out_shapes = (
        jax.ShapeDtypeStruct((b, l, W_A), BF16),
        jax.ShapeDtypeStruct((b, l, W_A), vn_dtype),
        jax.ShapeDtypeStruct((b, l, W_B), BF16),
        jax.ShapeDtypeStruct((b, pad_rows + l, W_B), BF16),
        jax.ShapeDtypeStruct((b, pad_rows + l, W_B), BF16),
        jax.ShapeDtypeStruct((b, keep, W_B), F32),
        jax.ShapeDtypeStruct((b, keep, W_B), F32),
    )
    padrow = lambda bi, i: (bi, i, 0)
    return pl.pallas_call(
        functools.partial(_in_proj_kernel, npad=npad, first_kept=first_kept),
        grid=(b, nt + npad),
        in_specs=[pl.BlockSpec((1, tm, d), row),
                  pl.BlockSpec((1, 1, d), bvec),
                  pl.BlockSpec((1, 1, d), bvec),
                  pl.BlockSpec((1, d), const2),
                  pl.BlockSpec((d, D_IN), const2),
                  pl.BlockSpec((1, W_A), const2),
                  pl.BlockSpec((1, W_B), const2),
                  pl.BlockSpec((1, W_B), const2),
                  pl.BlockSpec((W_B, W_B), const2)],
        out_specs=(pl.BlockSpec((1, tm, W_A), row),
                   pl.BlockSpec((1, tm, W_A), row),
                   pl.BlockSpec((1, tm, W_B), row),
                   pl.BlockSpec((1, tm, W_B), padrow),
                   pl.BlockSpec((1, tm, W_B), padrow),
                   pl.BlockSpec((1, tm, W_B), kept),
                   pl.BlockSpec((1, tm, W_B), kept)),
        out_shape=out_shapes,
        compiler_params=_cparams(("arbitrary", "arbitrary")),
        name="in_proj",
    )(x, sh1, sc1, g_mix, w_in_bf, gv, gq, gk, hsum)


def _mixer_kernel(x_ref, u_ref, vn_ref, q_ref, kp_ref, vp_ref, bias_ref, wsp_ref, bsp_ref,
                  goa_ref, gob_ref, wout_ref, gt1_ref, gffn_ref, sh2_ref, sc2_ref, *rest,
                  sb, nsb, kw, padded, nj, nmain):
    x1_ref, h2_ref, merged_ref = rest[-3:]
    i = pl.program_id(0)
    if len(rest) == 5:
        tx1_ref, th2_ref = rest[:2]

        @pl.when(i >= nmain)
        def _():
            x1_ref[...] = tx1_ref[...]
            h2_ref[...] = th2_ref[...]

    @pl.when(i < nmain)
    def _():
        _mixer_body(x_ref, u_ref, vn_ref, q_ref, kp_ref, vp_ref, bias_ref, wsp_ref, bsp_ref,
                    goa_ref, gob_ref, wout_ref, gt1_ref, gffn_ref, sh2_ref, sc2_ref,
                    x1_ref, h2_ref, merged_ref, i % nj, sb=sb, nsb=nsb, kw=kw, padded=padded)


def _mixer_body(x_ref, u_ref, vn_ref, q_ref, kp_ref, vp_ref, bias_ref, wsp_ref, bsp_ref,
                goa_ref, gob_ref, wout_ref, gt1_ref, gffn_ref, sh2_ref, sc2_ref,
                x1_ref, h2_ref, merged_ref, j, *, sb, nsb, kw, padded):
    rows = sb * nsb
    ri = lax.broadcasted_iota(jnp.int32, (sb, sb), 0)
    ci = lax.broadcasted_iota(jnp.int32, (sb, sb), 1)
    gmask = (ci // CHUNK) <= (ri // CHUNK)
    lane = lax.broadcasted_iota(jnp.int32, (sb, 2 * HEAD_DIM), 1)
    lo_half = lane < HEAD_DIM
    kidx = lax.broadcasted_iota(jnp.int32, (sb, kw), 1)

    for s in range(nsb):
        r0 = s * sb
        vn = vn_ref[0, r0:r0 + sb, :].astype(BF16)
        u = u_ref[0, r0:r0 + sb, :].astype(F32)
        zs = []
        for g in range(A_GROUPS):
            w = jnp.where(gmask, wsp_ref[g], 0.0).astype(BF16)
            z = jnp.dot(w, vn[:, g * A_GROUP_DIM:(g + 1) * A_GROUP_DIM], preferred_element_type=F32)
            zs.append(z + bsp_ref[g])
        a = u * jnp.concatenate(zs, axis=1)
        a = a * _rms(a) * goa_ref[...]
        merged_ref[r0:r0 + sb, 0:W_A] = a.astype(BF16)

        if padded:
            blk0 = j * rows + r0
            kstart = pl.multiple_of(blk0, sb)
            kb = kp_ref[0, pl.ds(kstart, kw), :]
            vb = vp_ref[0, pl.ds(kstart, kw), :]
            valid = kidx >= (ATT_PAST_WINDOW - blk0)
        else:
            kb = kp_ref[0]
            vb = vp_ref[0]
            valid = None
        qb = q_ref[0, r0:r0 + sb, :]
        scs = []
        for h in range(B_HEADS):
            c0 = (h // 2) * 2 * HEAD_DIM
            qp = qb[:, c0:c0 + 2 * HEAD_DIM]
            qm = jnp.where(lo_half if h % 2 == 0 else jnp.logical_not(lo_half), qp, jnp.zeros_like(qp))
            sc = lax.dot_general(qm, kb[:, c0:c0 + 2 * HEAD_DIM], (((1,), (1,)), ((), ())),
                                 preferred_element_type=F32)
            scs.append(sc + bias_ref[h])
        ps, ls = [], []
        for sc in scs:
            if valid is not None:
                sc = jnp.where(valid, sc, NEG_INF)
            p = jnp.exp(sc - jnp.max(sc, axis=-1, keepdims=True))
            ls.append(jnp.sum(p, axis=-1, keepdims=True))
            ps.append(p.astype(BF16))
        os_ = []
        for h in range(B_HEADS):
            c0 = (h // 2) * 2 * HEAD_DIM
            os_.append(jnp.dot(ps[h], vb[:, c0:c0 + 2 * HEAD_DIM], preferred_element_type=F32) / ls[h])
        outs = [jnp.where(lo_half, os_[2 * hp], os_[2 * hp + 1]) for hp in range(B_HEADS // 2)]
        bo = jnp.concatenate(outs, axis=1)
        bo = bo * _rms(bo) * gob_ref[...]
        merged_ref[r0:r0 + sb, W_A:W_A + W_B] = bo.astype(BF16)

    mix = jnp.dot(merged_ref[...], wout_ref[...], preferred_element_type=F32)
    x1 = x_ref[0] + gt1_ref[0] * mix
    x1_ref[...] = x1
    h2 = x1 * _rms(x1) * gffn_ref[...]
    _store_rows(h2_ref, h2 * (1.0 + sc2_ref[0]) + sh2_ref[0])


def _mixer(x, u, vn, q, kp, vp, bias, wsp, bsp, goa, gob, wout_bf, gt1, gffn, sh2, sc2,
           *, sb, nsb, padded, tail=None):
    b, l, d = x.shape
    rows = sb * nsb
    assert l % rows == 0
    nj = l // rows
    nmain = b * nj
    ntail = 0
    if tail is not None:
        assert tail[0].shape[0] % rows == 0
        ntail = tail[0].shape[0] // rows
    total_rows = (nmain + ntail) * rows
    kw = bias.shape[2]
    kl = kp.shape[1]
    bidx = lambda i: jnp.minimum(i // nj, b - 1)
    row = lambda i: (bidx(i), i % nj, 0)
    bvec = lambda i: (bidx(i), 0, 0)
    c2 = lambda i: (0, 0)
    c3 = lambda i: (0, 0, 0)
    pool = lambda i: (i, 0)
    in_specs = [pl.BlockSpec((1, rows, d), row),
                pl.BlockSpec((1, rows, W_A), row),
                pl.BlockSpec((1, rows, W_A), row),
                pl.BlockSpec((1, rows, W_B), row),
                pl.BlockSpec((1, kl, W_B), bvec),
                pl.BlockSpec((1, kl, W_B), bvec),
                pl.BlockSpec(bias.shape, c3),
                pl.BlockSpec(wsp.shape, c3),
                pl.BlockSpec(bsp.shape, c3),
                pl.BlockSpec((1, W_A), c2),
                pl.BlockSpec((1, W_B), c2),
                pl.BlockSpec((d, d), c2),
                pl.BlockSpec((1, 1, d), bvec),
                pl.BlockSpec((1, d), c2),
                pl.BlockSpec((1, 1, d), bvec),
                pl.BlockSpec((1, 1, d), bvec)]
    args = [x, u, vn, q, kp, vp, bias, wsp, bsp, goa, gob, wout_bf, gt1, gffn, sh2, sc2]
    if tail is not None:
        tmap = lambda i: (jnp.maximum(i - nmain, 0), 0)
        in_specs += [pl.BlockSpec((rows, d), tmap), pl.BlockSpec((rows * NCH, LANES), tmap)]
        args += list(tail)
    return pl.pallas_call(
        functools.partial(_mixer_kernel, sb=sb, nsb=nsb, kw=kw, padded=padded, nj=nj, nmain=nmain),
        grid=(nmain + ntail,),
        in_specs=in_specs,
        out_specs=(pl.BlockSpec((rows, d), pool), pl.BlockSpec((rows * NCH, LANES), pool)),
        out_shape=(jax.ShapeDtypeStruct((total_rows, d), F32),
                   jax.ShapeDtypeStruct((total_rows * NCH, LANES), F32)),
        scratch_shapes=[pltpu.VMEM((rows, d), BF16)],
        compiler_params=_cparams(("arbitrary",)),
        name="mixer",
    )(*args)


def _route_kernel(h2_ref, wrt_ref, br_ref, pk_ref, gw_ref, cnt_ref, run_ref, *, tm):
    i = pl.program_id(0)

    @pl.when(i == 0)
    def _():
        run_ref[...] = jnp.zeros_like(run_ref)

    x = _load_rows(h2_ref, tm).astype(BF16)
    logits = lax.dot_general(wrt_ref[...], x, (((1,), (1,)), ((), ())), preferred_element_type=F32)
    aff = jax.nn.sigmoid(logits)
    sel = aff + br_ref[...]
    row = lax.broadcasted_iota(jnp.int32, (N_EXPERTS, tm), 0)
    picks = []
    onehot = jnp.zeros((N_EXPERTS, tm), F32)
    for _ in range(TOP_K):
        m = jnp.max(sel, axis=0, keepdims=True)
        ik = jnp.min(jnp.where(sel == m, row, N_EXPERTS), axis=0, keepdims=True)
        oh = row == ik
        sel = jnp.where(oh, -jnp.inf, sel)
        onehot = jnp.where(oh, 1.0, onehot)
        picks.append((ik, oh))
    ri = lax.broadcasted_iota(jnp.int32, (tm, tm), 0)
    ci = lax.broadcasted_iota(jnp.int32, (tm, tm), 1)
    earlier = (ri < ci).astype(BF16)
    before = jnp.dot(onehot.astype(BF16), earlier, preferred_element_type=F32) + run_ref[...]
    affs = [jnp.sum(jnp.where(oh, aff, 0.0), axis=0, keepdims=True) for _, oh in picks]
    tot = affs[0]
    for a in affs[1:]:
        tot = tot + a
    for k, (ik, oh) in enumerate(picks):
        rank = jnp.sum(jnp.where(oh, before, 0.0), axis=0, keepdims=True).astype(jnp.int32)
        pk_ref[k:k + 1, :] = ik * (1 << RANK_BITS) + rank
        gw_ref[k:k + 1, :] = affs[k] / tot * ROUTED_SCALE
    run = run_ref[...] + jnp.sum(onehot, axis=1, keepdims=True)
    run_ref[...] = run
    cnt_ref[...] = run.astype(jnp.int32)


def _route(h2, wrt_bf, b_router, *, tm):
    t, d = h2.shape[0] // NCH, D_MODEL
    return pl.pallas_call(
        functools.partial(_route_kernel, tm=tm),
        grid=(t // tm,),
        in_specs=[pl.BlockSpec((tm * NCH, LANES), lambda i: (i, 0)),
                  pl.BlockSpec((N_EXPERTS, d), lambda i: (0, 0)),
                  pl.BlockSpec((N_EXPERTS, 1), lambda i: (0, 0))],
        out_specs=(pl.BlockSpec((TOP_K, tm), lambda i: (0, i)),
                   pl.BlockSpec((TOP_K, tm), lambda i: (0, i)),
                   pl.BlockSpec((N_EXPERTS, 1), lambda i: (0, 0))),
        out_shape=(jax.ShapeDtypeStruct((TOP_K, t), jnp.int32),
                   jax.ShapeDtypeStruct((TOP_K, t), F32),
                   jax.ShapeDtypeStruct((N_EXPERTS, 1), jnp.int32)),
        scratch_shapes=[pltpu.VMEM((N_EXPERTS, 1), F32)],
        compiler_params=_cparams(("arbitrary",)),
        name="route",
    )(h2, wrt_bf, b_router.reshape(N_EXPERTS, 1))


def _idx_copy(idx_hbm, idx_smem, sem, step, n, half):
    return pltpu.make_async_copy(idx_hbm.at[pl.ds(pl.multiple_of(step * n, n), n)],
                                 idx_smem.at[pl.ds(half * n, n)], sem.at[half])


def _slots_kernel(pk_ref, offs_ref, dst_ref):
    half = N_EXPERTS // 2
    shape = (TOP_K, LANES)
    lo_tab = jnp.broadcast_to(offs_ref[:, 0:half], shape)
    hi_tab = jnp.broadcast_to(offs_ref[:, half:N_EXPERTS], shape)
    for j in range(pk_ref.shape[1] // LANES):
        lanes = slice(j * LANES, (j + 1) * LANES)
        p = pk_ref[:, lanes]
        e = p >> RANK_BITS
        start = jnp.where(e >= half, jnp.take_along_axis(hi_tab, e & (half - 1), axis=1),
                          jnp.take_along_axis(lo_tab, e & (half - 1), axis=1))
        dst_ref[:, lanes] = ((p & ((1 << RANK_BITS) - 1)) + start) * NCH


def _slots(pk, offs, *, tm):
    t = pk.shape[1]
    return pl.pallas_call(
        _slots_kernel,
        grid=(t // tm,),
        in_specs=[pl.BlockSpec((TOP_K, tm), lambda i: (0, i)),
                  pl.BlockSpec((1, N_EXPERTS), lambda i: (0, 0))],
        out_specs=pl.BlockSpec((TOP_K, tm), lambda i: (0, i)),
        out_shape=jax.ShapeDtypeStruct((TOP_K, t), jnp.int32),
        compiler_params=_cparams(("arbitrary",)),
        name="slots",
    )(pk, offs[:N_EXPERTS].reshape(1, N_EXPERTS))


def _dispatch_kernel(h2_ref, dst_hbm, xs_hbm, idx_smem, idx_sem, row_sem, *, tm):
    i = pl.program_id(0)
    n = tm * TOP_K

    @pl.when(i == 0)
    def _():
        _idx_copy(dst_hbm, idx_smem, idx_sem, i, n, 0).start()

    def scatter(half):
        _idx_copy(dst_hbm, idx_smem, idx_sem, i, n, half).wait()

        @pl.when(i + 1 < pl.num_programs(0))
        def _():
            _idx_copy(dst_hbm, idx_smem, idx_sem, i + 1, n, 1 - half).start()

        def body(t8, carry):
            r0 = pl.multiple_of(t8 * (8 * NCH), 8 * NCH)
            b = half * n + t8 * (8 * TOP_K)
            for tt in range(8):
                src = h2_ref.at[pl.ds(r0 + tt * NCH, NCH)]
                for k in range(TOP_K):
                    dst = pl.multiple_of(idx_smem[b + tt * TOP_K + k], NCH)
                    pltpu.make_async_copy(src, xs_hbm.at[pl.ds(dst, NCH)], row_sem).start(priority=k % 2)
            return carry

        lax.fori_loop(0, tm // 8, body, 0)

    for half in range(2):
        pl.when(i % 2 == half)(functools.partial(scatter, half))
    pltpu.make_async_copy(xs_hbm.at[pl.ds(0, n * NCH)], xs_hbm.at[pl.ds(0, n * NCH)], row_sem).wait()


def _dispatch(h2, dst_flat, *, tm):
    t = h2.shape[0] // NCH
    n = tm * TOP_K
    return pl.pallas_call(
        functools.partial(_dispatch_kernel, tm=tm),
        grid=(t // tm,),
        in_specs=[pl.BlockSpec((tm * NCH, LANES), lambda i: (i, 0)),
                  pl.BlockSpec(memory_space=pl.ANY)],
        out_specs=pl.BlockSpec(memory_space=pl.ANY),
        scratch_shapes=[pltpu.SMEM((2 * n,), jnp.int32),
                        pltpu.SemaphoreType.DMA((2,)),
                        pltpu.SemaphoreType.DMA(())],
        out_shape=jax.ShapeDtypeStruct((t * TOP_K * NCH, LANES), F32),
        compiler_params=_cparams(("arbitrary",)),
        name="dispatch",
    )(h2, dst_flat)


def _experts_kernel(vt_ref, vg_ref, offs_ref, nv_ref, xs_ref, wg_ref, wu_ref, wd_ref, ys_ref,
                    wgb_ref, wub_ref, wdb_ref, *, tm):
    v = pl.program_id(0)

    @pl.when(v < nv_ref[0])
    def _():
        g = vg_ref[v]
        tile = vt_ref[v]
        prev = jnp.maximum(v - 1, 0)
        first = jnp.logical_or(v == 0, vt_ref[prev] != tile)

        @pl.when(jnp.logical_or(v == 0, vg_ref[prev] != g))
        def _():
            wgb_ref[...] = wg_ref[0].astype(BF16)
            wub_ref[...] = wu_ref[0].astype(BF16)
            wdb_ref[...] = wd_ref[0].astype(BF16)

        def swiglu_rows(r0, nr, masked):
            x = jnp.concatenate([xs_ref[pl.ds(r0 * NCH + c, nr, stride=NCH), :] for c in range(NCH)],
                                axis=1).astype(BF16)
            hg = jnp.dot(x, wgb_ref[...], preferred_element_type=F32)
            hu = jnp.dot(x, wub_ref[...], preferred_element_type=F32)
            hm = (hg * jax.nn.sigmoid(hg) * hu).astype(BF16)
            if masked:
                rowid = tile * tm + r0 + lax.broadcasted_iota(jnp.int32, (nr, 1), 0)
                mine = jnp.logical_and(rowid >= offs_ref[g], rowid < offs_ref[g + 1])
            for c2 in range(NCH // 2):
                y2 = jnp.dot(hm, wdb_ref[:, c2 * 2 * LANES:(c2 + 1) * 2 * LANES], preferred_element_type=F32)
                for j in range(2):
                    dst = pl.ds(r0 * NCH + 2 * c2 + j, nr, stride=NCH)
                    yc = y2[:, j * LANES:(j + 1) * LANES]
                    ys_ref[dst, :] = jnp.where(mine, yc, ys_ref[dst, :]) if masked else yc

        def visit(masked):
            for r0 in range(0, tm, tm // 2):
                swiglu_rows(r0, tm // 2, masked)

        pl.when(first)(functools.partial(visit, False))
        pl.when(jnp.logical_not(first))(functools.partial(visit, True))


def _experts(vt, vg, offs, nv, xs, w_gate, w_up, w_down, *, tm):
    m, d = xs.shape[0] // NCH, D_MODEL
    nvis = vt.shape[0]
    de = w_gate.shape[2]
    xmap = lambda v, vt, vg, offs, nv: (vt[v], 0)
    wmap = lambda v, vt, vg, offs, nv: (vg[v], 0, 0)
    grid_spec = pltpu.PrefetchScalarGridSpec(
        num_scalar_prefetch=4,
        grid=(nvis,),
        in_specs=[pl.BlockSpec((tm * NCH, LANES), xmap),
                  pl.BlockSpec((1, d, de), wmap),
                  pl.BlockSpec((1, d, de), wmap),
                  pl.BlockSpec((1, de, d), wmap)],
        out_specs=pl.BlockSpec((tm * NCH, LANES), xmap),
        scratch_shapes=[pltpu.VMEM((d, de), BF16), pltpu.VMEM((d, de), BF16), pltpu.VMEM((de, d), BF16)],
    )
    return pl.pallas_call(
        functools.partial(_experts_kernel, tm=tm),
        grid_spec=grid_spec,
        out_shape=jax.ShapeDtypeStruct((m * NCH, LANES), F32),
        compiler_params=_cparams(("arbitrary",)),
        name="experts",
    )(vt, vg, offs, nv, xs, w_gate, w_up, w_down)


def _combine_kernel(x1_ref, h2_ref, gw_ref, gt2_ref, wsg_ref, wsu_ref, wsd_ref, dst_hbm, ys_hbm,
                    y_ref, idx_smem, buf_ref, acc_ref, idx_sem, row_sem, *, tm, step0, per_token_gate):
    i = pl.program_id(0)
    nsteps = pl.num_programs(0)
    n = tm * TOP_K

    def issue(b, r0, half):
        for tt in range(8):
            for k in range(TOP_K):
                src = pl.multiple_of(idx_smem[b + tt * TOP_K + k], NCH)
                pltpu.make_async_copy(ys_hbm.at[pl.ds(src, NCH)],
                                      buf_ref.at[half, k, pl.ds(r0 + tt * NCH, NCH)],
                                      row_sem.at[half]).start(priority=k % 2)

    def finish8(t8, cur):
        t0 = pl.multiple_of(t8 * 8, 8)
        gw = gw_ref[pl.ds(t0, 8), :]
        for c in range(NCH):
            lanes = slice(c * LANES, (c + 1) * LANES)
            ffn = acc_ref[pl.ds(t0, 8), lanes]
            for k in range(TOP_K):
                ffn = ffn + gw[:, k:k + 1] * buf_ref[cur, k, pl.ds(t0 * NCH + c, 8, stride=NCH), :]
            gate = gt2_ref[pl.ds(t0, 8), lanes] if per_token_gate else gt2_ref[0][:, lanes]
            y_ref[pl.ds(t0, 8), lanes] = x1_ref[pl.ds(t0, 8), lanes] + gate * ffn

    @pl.when(i == 0)
    def _():
        _idx_copy(dst_hbm, idx_smem, idx_sem, step0, n, 0).start()
        _idx_copy(dst_hbm, idx_smem, idx_sem, step0, n, 0).wait()

        def body(t8, carry):
            issue(t8 * (8 * TOP_K), pl.multiple_of(t8 * (8 * NCH), 8 * NCH), 0)
            return carry

        lax.fori_loop(0, tm // 8, body, 0)

        @pl.when(nsteps > 1)
        def _():
            _idx_copy(dst_hbm, idx_smem, idx_sem, step0 + 1, n, 1).start()

    h2 = _load_rows(h2_ref, tm).astype(BF16)
    sg = jnp.dot(h2, wsg_ref[...], preferred_element_type=F32)
    su = jnp.dot(h2, wsu_ref[...], preferred_element_type=F32)
    acc_ref[...] = jnp.dot((sg * jax.nn.sigmoid(sg) * su).astype(BF16), wsd_ref[...], preferred_element_type=F32)

    def step(cur, has_next):
        pltpu.make_async_copy(ys_hbm.at[pl.ds(0, n * NCH)], ys_hbm.at[pl.ds(0, n * NCH)], row_sem.at[cur]).wait()
        if has_next:
            _idx_copy(dst_hbm, idx_smem, idx_sem, step0 + i + 1, n, 1 - cur).wait()

        def body(t8, carry):
            if has_next:
                issue((1 - cur) * n + t8 * (8 * TOP_K), pl.multiple_of(t8 * (8 * NCH), 8 * NCH), 1 - cur)
            finish8(t8, cur)
            return carry

        lax.fori_loop(0, tm // 8, body, 0)

        if has_next:
            @pl.when(i + 2 < nsteps)
            def _():
                _idx_copy(dst_hbm, idx_smem, idx_sem, step0 + i + 2, n, cur).start()

    for cur in range(2):
        pl.when(jnp.logical_and(i % 2 == cur, i + 1 < nsteps))(functools.partial(step, cur, True))
        pl.when(jnp.logical_and(i % 2 == cur, i + 1 == nsteps))(functools.partial(step, cur, False))


def _combine(x1, h2, gw, gt2, wsg, wsu, wsd, dst_flat, ys, *, tm, batch, seq, row_offset):
    d = x1.shape[1]
    n = tm * TOP_K
    per_token_gate = gt2.ndim == 2
    assert per_token_gate or seq % tm == 0
    per_b = max(seq // tm, 1)
    step0 = row_offset // tm
    tok = lambda i: (step0 + i, 0)
    c2 = lambda i: (0, 0)
    gate_spec = (pl.BlockSpec((tm, d), lambda i: (i, 0)) if per_token_gate
                 else pl.BlockSpec((1, 1, d), lambda i: (i // per_b, 0, 0)))
    return pl.pallas_call(
        functools.partial(_combine_kernel, tm=tm, step0=step0, per_token_gate=per_token_gate),
        grid=(batch * seq // tm,),
        in_specs=[pl.BlockSpec((tm, d), tok),
                  pl.BlockSpec((tm * NCH, LANES), tok),
                  pl.BlockSpec((tm, 128), tok),
                  gate_spec,
                  pl.BlockSpec(wsg.shape, c2),
                  pl.BlockSpec(wsu.shape, c2),
                  pl.BlockSpec(wsd.shape, c2),
                  pl.BlockSpec(memory_space=pl.ANY),
                  pl.BlockSpec(memory_space=pl.ANY)],
        out_specs=pl.BlockSpec((tm, d), lambda i: (i, 0)),
        scratch_shapes=[pltpu.SMEM((2 * n,), jnp.int32),
                        pltpu.VMEM((2, TOP_K, tm * NCH, LANES), F32),
                        pltpu.VMEM((tm, d), F32),
                        pltpu.SemaphoreType.DMA((2,)),
                        pltpu.SemaphoreType.DMA((2,))],
        out_shape=jax.ShapeDtypeStruct((batch * seq, d), F32),
        compiler_params=_cparams(("arbitrary",)),
        name="combine",
    )(x1, h2, gw, gt2, wsg, wsu, wsd, dst_flat, ys)


def _bias_table(rel_bias, sb, kw):
    h, n_rel = rel_bias.shape
    lo = (kw - 1) - ATT_PAST_WINDOW - (CHUNK - 1)
    ntot = sb + kw - 1
    s = jnp.concatenate([jnp.broadcast_to(rel_bias[:, :1], (h, lo)), rel_bias,
                         jnp.broadcast_to(rel_bias[:, -1:], (h, ntot - lo - n_rel))], axis=1).astype(F32)
    s_rev = jnp.pad(s[:, ::-1], ((0, 0), (0, 1)))
    skew = jnp.broadcast_to(s_rev[:, None, :], (h, sb, ntot + 1)).reshape(h, -1)[:, :sb * ntot]
    toep = skew.reshape(h, sb, ntot)[:, :, sb - 1:sb - 1 + kw]
    r = np.arange(sb)[:, None]
    kk = np.arange(kw)[None, :]
    qc, kc = r // CHUNK, kk // CHUNK
    band = (kc >= qc) & (kc <= qc + LEFT_CHUNKS)
    return jnp.where(band[None], toep, NEG_INF)


def _visit_plan(offs, m, tm, nvis):
    cnt = offs[1:] - offs[:-1]
    first = offs[:-1] // tm
    last = jnp.maximum(offs[1:] - 1, 0) // tm
    per = jnp.where(cnt > 0, last - first + 1, 0)
    vend = jnp.cumsum(per)
    vstart = vend - per
    total = vend[-1]
    v = jnp.arange(nvis, dtype=jnp.int32)
    g = jnp.minimum(jnp.sum(vend[None, :] <= v[:, None], axis=1), N_EXPERTS - 1).astype(jnp.int32)
    tile = (first[g] + v - vstart[g]).astype(jnp.int32)
    live = v < total
    last_tile = m // tm - 1
    tile = jnp.where(live, tile, last_tile)
    g = jnp.where(live, g, N_EXPERTS - 1)
    return tile, g, total.reshape(1).astype(jnp.int32)


def kernel(x_prompt, x_sample, c_prompt, c_sample, cache_k, cache_v, w_ada, b_ada, g_mix, w_in, g_gmlp_v,
           w_spatial, b_spatial, g_q, g_k, rel_bias, g_out_a, g_out_b, w_out, g_ffn, w_router, b_router,
           w_exp_gate, w_exp_up, w_exp_down, w_sh_gate, w_sh_up, w_sh_down):
    depth = w_ada.shape[0]
    assert depth == 1, "single layer"
    bp, sp, d = x_prompt.shape
    bs, ss, _ = x_sample.shape
    tp, ts = bp * sp, bs * ss
    t_all = tp + ts
    l = 0

    c_all = jnp.concatenate([c_prompt, c_sample], axis=0)
    nb = c_all.shape[0]
    nb_pad = -(-nb // 8) * 8
    c_all = jnp.pad(c_all, ((0, nb_pad - nb), (0, 0)))
    mod = _ada(c_all, w_ada[l], b_ada[l])[:nb].reshape(nb, 6, 1, d)
    sh1, sc1, gt1, sh2, sc2, gt2 = [mod[:, i] for i in range(6)]

    w_in_bf = w_in[l].astype(BF16)
    w_out_bf = w_out[l].astype(BF16)
    gv = g_gmlp_v[l].reshape(1, W_A)
    gq = jnp.tile(g_q[l], B_HEADS).reshape(1, W_B)
    gk = jnp.tile(g_k[l], B_HEADS).reshape(1, W_B)
    hid = np.arange(W_B) // HEAD_DIM
    hsum = jnp.asarray(hid[:, None] == hid[None, :], BF16)
    gmix = g_mix[l].reshape(1, d)
    goa = g_out_a[l].reshape(1, W_A)
    gob = g_out_b[l].reshape(1, W_B)
    gffn = g_ffn[l].reshape(1, d)
    bias_p = _bias_table(rel_bias[l], GMLP_BLOCK, ATT_PAST_WINDOW + GMLP_BLOCK)
    bsp_p = jnp.broadcast_to(b_spatial[l][:, :, None], (A_GROUPS, GMLP_BLOCK, A_GROUP_DIM))

    cl = cache_k.shape[2]
    us, vns, qs, ks_bf, vs_bf, k32s, v32s = _in_proj(
        x_sample, sh1[bp:], sc1[bp:], gmix, w_in_bf, gv, gq, gk, hsum,
        tm=ss, pad_rows=0, keep=ss, vn_dtype=F32)
    kcat = jnp.concatenate([cache_k[l].reshape(bs, cl, W_B).astype(BF16), ks_bf], axis=1)
    vcat = jnp.concatenate([cache_v[l].reshape(bs, cl, W_B).astype(BF16), vs_bf], axis=1)
    bias_s = bias_p[:, :ss, ATT_PAST_WINDOW - cl:ATT_PAST_WINDOW + ss]
    x1_s, h2_s = _mixer(x_sample, us, vns, qs, kcat, vcat, bias_s, w_spatial[l][:, :ss, :ss], bsp_p[:, :ss],
                        goa, gob, w_out_bf, gt1[bp:], gffn, sh2[bp:], sc2[bp:],
                        sb=ss, nsb=1, padded=False)
    new_k_sample = k32s.reshape(1, bs, ss, B_HEADS, HEAD_DIM)
    new_v_sample = v32s.reshape(1, bs, ss, B_HEADS, HEAD_DIM)
    new_gmlp_v_sample = vns.reshape(1, bs, ss, A_GROUPS, A_GROUP_DIM)

    keep = min(ATT_PAST_WINDOW, sp)
    u, vn, q, kpad, vpad, k32, v32 = _in_proj(
        x_prompt, sh1[:bp], sc1[:bp], gmix, w_in_bf, gv, gq, gk, hsum,
        tm=512, pad_rows=ATT_PAST_WINDOW, keep=keep, vn_dtype=BF16)
    x1, h2 = _mixer(x_prompt, u, vn, q, kpad, vpad, bias_p, w_spatial[l], bsp_p, goa, gob, w_out_bf,
                    gt1[:bp], gffn, sh2[:bp], sc2[:bp],
                    sb=GMLP_BLOCK, nsb=2, padded=True, tail=(x1_s, h2_s))
    assert x1.shape[0] == t_all
    new_k_prompt = k32.reshape(1, bp, keep, B_HEADS, HEAD_DIM)
    new_v_prompt = v32.reshape(1, bp, keep, B_HEADS, HEAD_DIM)

    tmr = 256
    tme = 512
    assert t_all % tmr == 0 and (t_all * TOP_K) % tme == 0 and sp % 256 == 0 and ts % (2 * ss) == 0
    assert t_all * TOP_K < (1 << RANK_BITS)
    pk, gw_t, cnt = _route(h2, w_router[l].T.astype(BF16), b_router[l], tm=tmr)
    offs = jnp.concatenate([jnp.zeros((1,), jnp.int32), jnp.cumsum(cnt[:, 0]).astype(jnp.int32)])
    tms = next(c for c in (2048, 1536, 1024, 512, 256) if t_all % c == 0)
    dst_flat = _slots(pk, offs, tm=tms).T.reshape(-1)
    gw = jnp.pad(gw_t.T, ((0, 0), (0, 128 - TOP_K)))
    xs = _dispatch(h2, dst_flat, tm=tmr)
    m = t_all * TOP_K
    nvis = m // tme + N_EXPERTS - 1
    vt, vg, nv = _visit_plan(offs, m, tme, nvis)
    ys = _experts(vt, vg, offs, nv, xs, w_exp_gate[l], w_exp_up[l], w_exp_down[l], tm=tme)

    wsg, wsu, wsd = w_sh_gate[l].astype(BF16), w_sh_up[l].astype(BF16), w_sh_down[l].astype(BF16)
    y_prompt = _combine(x1, h2, gw, gt2[:bp], wsg, wsu, wsd, dst_flat, ys,
                        tm=256, batch=bp, seq=sp, row_offset=0).reshape(bp, sp, d)
    gt2_tok = jnp.broadcast_to(gt2[bp:], (bs, ss, d)).reshape(ts, d)
    y_sample = _combine(x1, h2, gw, gt2_tok, wsg, wsu, wsd, dst_flat, ys,
                        tm=2 * ss, batch=bs, seq=ss, row_offset=tp).reshape(bs, ss, d)
    return (y_prompt, y_sample, new_k_prompt, new_v_prompt, new_k_sample, new_v_sample, new_gmlp_v_sample)
```

```python
import functools

import jax
import jax.numpy as jnp
import numpy as np
from jax import lax
from jax.experimental import pallas as pl
from jax.experimental.pallas import tpu as pltpu

F32 = jnp.float32
BF16 = jnp.bfloat16

D_MODEL = 1024
CHUNK = 64
GMLP_BLOCK = 128
W_A = 512
A_GROUPS = 4
A_GROUP_DIM = 128
W_B = 512
B_HEADS = 8
HEAD_DIM = 64
LEFT_CHUNKS = 8
ATT_PAST_WINDOW = LEFT_CHUNKS * CHUNK
REL_CLIP = 128
D_IN = 2 * W_A + 3 * W_B
N_EXPERTS = 256
TOP_K = 8
D_EXPERT = 256
ROUTED_SCALE = 2.5
EPS = 1e-6
NEG_INF = -1e30

RANK_BITS = 20
VMEM_LIMIT = 56 * 1024 * 1024


def _cparams(sem, vmem=VMEM_LIMIT):
    return pltpu.CompilerParams(dimension_semantics=sem, vmem_limit_bytes=vmem)


def _rms(x, axis=-1):
    return lax.rsqrt(jnp.mean(x * x, axis=axis, keepdims=True) + EPS)


LANES = 128
NCH = D_MODEL // LANES


def _load_rows(ref, n, lead=()):
    return jnp.concatenate([ref[(*lead, pl.ds(c, n, stride=NCH), slice(None))] for c in range(NCH)], axis=1)


def _store_rows(ref, val):
    n = val.shape[0]
    for c in range(NCH):
        ref[pl.ds(c, n, stride=NCH), :] = val[:, c * LANES:(c + 1) * LANES]


def _ada_kernel(c_ref, w_ref, b_ref, o_ref):
    c = c_ref[...]
    s = c * jax.nn.sigmoid(c)
    o_ref[...] = jnp.dot(s.astype(BF16), w_ref[...].astype(BF16), preferred_element_type=F32) + b_ref[...]


def _ada(c, w_ada, b_ada):
    n, d = c.shape
    nout = w_ada.shape[1]
    tn = 1024
    return pl.pallas_call(
        _ada_kernel,
        grid=(nout // tn,),
        in_specs=[pl.BlockSpec((n, d), lambda j: (0, 0)),
                  pl.BlockSpec((d, tn), lambda j: (0, j)),
                  pl.BlockSpec((1, tn), lambda j: (0, j))],
        out_specs=pl.BlockSpec((n, tn), lambda j: (0, j)),
        out_shape=jax.ShapeDtypeStruct((n, nout), F32),
        compiler_params=_cparams(("arbitrary",)),
        name="ada",
    )(c, w_ada, b_ada.reshape(1, nout))


def _in_proj_kernel(x_ref, sh_ref, sc_ref, gmix_ref, win_ref, gv_ref, gq_ref, gk_ref, hsum_ref,
                    u_ref, vn_ref, q_ref, kp_ref, vp_ref, k32_ref, v32_ref, *, npad, first_kept):
    i = pl.program_id(1)

    @pl.when(i < npad)
    def _():
        kp_ref[...] = jnp.zeros_like(kp_ref)
        vp_ref[...] = jnp.zeros_like(vp_ref)

    @pl.when(i >= npad)
    def _():
        x = x_ref[0]
        h = x * _rms(x) * gmix_ref[...]
        h = h * (1.0 + sc_ref[0]) + sh_ref[0]
        proj = jnp.dot(h.astype(BF16), win_ref[...], preferred_element_type=F32)
        u = jax.nn.gelu(proj[:, 0:W_A])
        u_ref[0] = u.astype(u_ref.dtype)
        v = jax.nn.gelu(proj[:, W_A:2 * W_A])
        for g in range(A_GROUPS):
            vg = v[:, g * A_GROUP_DIM:(g + 1) * A_GROUP_DIM]
            vn = vg * _rms(vg) * gv_ref[:, g * A_GROUP_DIM:(g + 1) * A_GROUP_DIM]
            vn_ref[0, :, g * A_GROUP_DIM:(g + 1) * A_GROUP_DIM] = vn.astype(vn_ref.dtype)
        q = proj[:, 2 * W_A:2 * W_A + W_B]
        k = proj[:, 2 * W_A + W_B:2 * W_A + 2 * W_B]
        va = proj[:, 2 * W_A + 2 * W_B:]
        q2 = jnp.dot((q * q).astype(BF16), hsum_ref[...], preferred_element_type=F32) * (1.0 / HEAD_DIM)
        k2 = jnp.dot((k * k).astype(BF16), hsum_ref[...], preferred_element_type=F32) * (1.0 / HEAD_DIM)
        qn = q * lax.rsqrt(q2 + EPS) * gq_ref[...]
        kn = k * lax.rsqrt(k2 + EPS) * gk_ref[...]
        q_ref[0] = (qn * (HEAD_DIM ** -0.5)).astype(q_ref.dtype)
        kp_ref[0] = kn.astype(kp_ref.dtype)
        vp_ref[0] = va.astype(vp_ref.dtype)

        @pl.when(i >= first_kept)
        def _():
            k32_ref[0] = kn
            v32_ref[0] = va


def _in_proj(x, sh1, sc1, g_mix, w_in_bf, gv, gq, gk, hsum, *, tm, pad_rows, keep, vn_dtype):
    b, l, d = x.shape
    assert l % tm == 0 and pad_rows % tm == 0 and keep % tm == 0
    npad = pad_rows // tm
    nt = l // tm
    first_kept = npad + nt - keep // tm
    row = lambda bi, i: (bi, jnp.maximum(i - npad, 0), 0)
    kept = lambda bi, i: (bi, jnp.maximum(i - first_kept, 0), 0)
    bvec = lambda bi, i: (bi, 0, 0)
    const2 = lambda bi, i: (0, 0)
    out_shapes = (
        jax.ShapeDtypeStruct((b, l, W_A), BF16),
        jax.ShapeDtypeStruct((b, l, W_A), vn_dtype),
        jax.ShapeDtypeStruct((b, l, W_B), BF16),
        jax.ShapeDtypeStruct((b, pad_rows + l, W_B), BF16),
        jax.ShapeDtypeStruct((b, pad_rows + l, W_B), BF16),
        jax.ShapeDtypeStruct((b, keep, W_B), F32),
        jax.ShapeDtypeStruct((b, keep, W_B), F32),
    )
    padrow = lambda bi, i: (bi, i, 0)
    return pl.pallas_call(
        functools.partial(_in_proj_kernel, npad=npad, first_kept=first_kept),
        grid=(b, nt + npad),
        in_specs=[pl.BlockSpec((1, tm, d), row),
                  pl.BlockSpec((1, 1, d), bvec),
                  pl.BlockSpec((1, 1, d), bvec),
                  pl.BlockSpec((1, d), const2),
                  pl.BlockSpec((d, D_IN), const2),
                  pl.BlockSpec((1, W_A), const2),
                  pl.BlockSpec((1, W_B), const2),
                  pl.BlockSpec((1, W_B), const2),
                  pl.BlockSpec((W_B, W_B), const2)],
        out_specs=(pl.BlockSpec((1, tm, W_A), row),
                   pl.BlockSpec((1, tm, W_A), row),
                   pl.BlockSpec((1, tm, W_B), row),
                   pl.BlockSpec((1, tm, W_B), padrow),
                   pl.BlockSpec((1, tm, W_B), padrow),
                   pl.BlockSpec((1, tm, W_B), kept),
                   pl.BlockSpec((1, tm, W_B), kept)),
        out_shape=out_shapes,
        compiler_params=_cparams(("arbitrary", "arbitrary")),
        name="in_proj",
    )(x, sh1, sc1, g_mix, w_in_bf, gv, gq, gk, hsum)


def _mixer_kernel(x_ref, u_ref, vn_ref, q_ref, kp_ref, vp_ref, bias_ref, wsp_ref, bsp_ref,
                  goa_ref, gob_ref, wout_ref, gt1_ref, gffn_ref, sh2_ref, sc2_ref, *rest,
                  sb, nsb, kw, padded, nj, nmain):
    x1_ref, h2_ref, merged_ref = rest[-3:]
    i = pl.program_id(0)
    if len(rest) == 5:
        tx1_ref, th2_ref = rest[:2]

        @pl.when(i >= nmain)
        def _():
            x1_ref[...] = tx1_ref[...]
            h2_ref[...] = th2_ref[...]

    @pl.when(i < nmain)
    def _():
        _mixer_body(x_ref, u_ref, vn_ref, q_ref, kp_ref, vp_ref, bias_ref, wsp_ref, bsp_ref,
                    goa_ref, gob_ref, wout_ref, gt1_ref, gffn_ref, sh2_ref, sc2_ref,
                    x1_ref, h2_ref, merged_ref, i % nj, sb=sb, nsb=nsb, kw=kw, padded=padded)


def _mixer_body(x_ref, u_ref, vn_ref, q_ref, kp_ref, vp_ref, bias_ref, wsp_ref, bsp_ref,
                goa_ref, gob_ref, wout_ref, gt1_ref, gffn_ref, sh2_ref, sc2_ref,
                x1_ref, h2_ref, merged_ref, j, *, sb, nsb, kw, padded):
    rows = sb * nsb
    ri = lax.broadcasted_iota(jnp.int32, (sb, sb), 0)
    ci = lax.broadcasted_iota(jnp.int32, (sb, sb), 1)
    gmask = (ci // CHUNK) <= (ri // CHUNK)
    lane = lax.broadcasted_iota(jnp.int32, (sb, 2 * HEAD_DIM), 1)
    lo_half = lane < HEAD_DIM
    kidx = lax.broadcasted_iota(jnp.int32, (sb, kw), 1)

    for s in range(nsb):
        r0 = s * sb
        vn = vn_ref[0, r0:r0 + sb, :].astype(BF16)
        u = u_ref[0, r0:r0 + sb, :].astype(F32)
        zs = []
        for g in range(A_GROUPS):
            w = jnp.where(gmask, wsp_ref[g], 0.0).astype(BF16)
            z = jnp.dot(w, vn[:, g * A_GROUP_DIM:(g + 1) * A_GROUP_DIM], preferred_element_type=F32)
            zs.append(z + bsp_ref[g])
        a = u * jnp.concatenate(zs, axis=1)
        a = a * _rms(a) * goa_ref[...]
        merged_ref[r0:r0 + sb, 0:W_A] = a.astype(BF16)

        if padded:
            blk0 = j * rows + r0
            kstart = pl.multiple_of(blk0, sb)
            kb = kp_ref[0, pl.ds(kstart, kw), :]
            vb = vp_ref[0, pl.ds(kstart, kw), :]
            valid = kidx >= (ATT_PAST_WINDOW - blk0)
        else:
            kb = kp_ref[0]
            vb = vp_ref[0]
            valid = None
        qb = q_ref[0, r0:r0 + sb, :]
        scs = []
        for h in range(B_HEADS):
            c0 = (h // 2) * 2 * HEAD_DIM
            qp = qb[:, c0:c0 + 2 * HEAD_DIM]
            qm = jnp.where(lo_half if h % 2 == 0 else jnp.logical_not(lo_half), qp, jnp.zeros_like(qp))
            sc = lax.dot_general(qm, kb[:, c0:c0 + 2 * HEAD_DIM], (((1,), (1,)), ((), ())),
                                 preferred_element_type=F32)
            scs.append(sc + bias_ref[h])
        ps, ls = [], []
        for sc in scs:
            if valid is not None:
                sc = jnp.where(valid, sc, NEG_INF)
            p = jnp.exp(sc - jnp.max(sc, axis=-1, keepdims=True))
            ls.append(jnp.sum(p, axis=-1, keepdims=True))
            ps.append(p.astype(BF16))
        os_ = []
        for h in range(B_HEADS):
            c0 = (h // 2) * 2 * HEAD_DIM
            os_.append(jnp.dot(ps[h], vb[:, c0:c0 + 2 * HEAD_DIM], preferred_element_type=F32) / ls[h])
        outs = [jnp.where(lo_half, os_[2 * hp], os_[2 * hp + 1]) for hp in range(B_HEADS // 2)]
        bo = jnp.concatenate(outs, axis=1)
        bo = bo * _rms(bo) * gob_ref[...]
        merged_ref[r0:r0 + sb, W_A:W_A + W_B] = bo.astype(BF16)

    mix = jnp.dot(merged_ref[...], wout_ref[...], preferred_element_type=F32)
    x1 = x_ref[0] + gt1_ref[0] * mix
    x1_ref[...] = x1
    h2 = x1 * _rms(x1) * gffn_ref[...]
    _store_rows(h2_ref, h2 * (1.0 + sc2_ref[0]) + sh2_ref[0])


def _mixer(x, u, vn, q, kp, vp, bias, wsp, bsp, goa, gob, wout_bf, gt1, gffn, sh2, sc2,
           *, sb, nsb, padded, tail=None):
    b, l, d = x.shape
    rows = sb * nsb
    assert l % rows == 0
    nj = l // rows
    nmain = b * nj
    ntail = 0
    if tail is not None:
        assert tail[0].shape[0] % rows == 0
        ntail = tail[0].shape[0] // rows
    total_rows = (nmain + ntail) * rows
    kw = bias.shape[2]
    kl = kp.shape[1]
    bidx = lambda i: jnp.minimum(i // nj, b - 1)
    row = lambda i: (bidx(i), i % nj, 0)
    bvec = lambda i: (bidx(i), 0, 0)
    c2 = lambda i: (0, 0)
    c3 = lambda i: (0, 0, 0)
    pool = lambda i: (i, 0)
    in_specs = [pl.BlockSpec((1, rows, d), row),
                pl.BlockSpec((1, rows, W_A), row),
                pl.BlockSpec((1, rows, W_A), row),
                pl.BlockSpec((1, rows, W_B), row),
                pl.BlockSpec((1, kl, W_B), bvec),
                pl.BlockSpec((1, kl, W_B), bvec),
                pl.BlockSpec(bias.shape, c3),
                pl.BlockSpec(wsp.shape, c3),
                pl.BlockSpec(bsp.shape, c3),
                pl.BlockSpec((1, W_A), c2),
                pl.BlockSpec((1, W_B), c2),
                pl.BlockSpec((d, d), c2),
                pl.BlockSpec((1, 1, d), bvec),
                pl.BlockSpec((1, d), c2),
                pl.BlockSpec((1, 1, d), bvec),
                pl.BlockSpec((1, 1, d), bvec)]
    args = [x, u, vn, q, kp, vp, bias, wsp, bsp, goa, gob, wout_bf, gt1, gffn, sh2, sc2]
    if tail is not None:
        tmap = lambda i: (jnp.maximum(i - nmain, 0), 0)
        in_specs += [pl.BlockSpec((rows, d), tmap), pl.BlockSpec((rows * NCH, LANES), tmap)]
        args += list(tail)
    return pl.pallas_call(
        functools.partial(_mixer_kernel, sb=sb, nsb=nsb, kw=kw, padded=padded, nj=nj, nmain=nmain),
        grid=(nmain + ntail,),
        in_specs=in_specs,
        out_specs=(pl.BlockSpec((rows, d), pool), pl.BlockSpec((rows * NCH, LANES), pool)),
        out_shape=(jax.ShapeDtypeStruct((total_rows, d), F32),
                   jax.ShapeDtypeStruct((total_rows * NCH, LANES), F32)),
        scratch_shapes=[pltpu.VMEM((rows, d), BF16)],
        compiler_params=_cparams(("arbitrary",)),
        name="mixer",
    )(*args)


def _route_kernel(h2_ref, wrt_ref, br_ref, pk_ref, gw_ref, cnt_ref, run_ref, *, tm):
    i = pl.program_id(0)

    @pl.when(i == 0)
    def _():
        run_ref[...] = jnp.zeros_like(run_ref)

    x = _load_rows(h2_ref, tm).astype(BF16)
    logits = lax.dot_general(wrt_ref[...], x, (((1,), (1,)), ((), ())), preferred_element_type=F32)
    aff = jax.nn.sigmoid(logits)
    sel = aff + br_ref[...]
    row = lax.broadcasted_iota(jnp.int32, (N_EXPERTS, tm), 0)
    picks = []
    onehot = jnp.zeros((N_EXPERTS, tm), F32)
    for _ in range(TOP_K):
        m = jnp.max(sel, axis=0, keepdims=True)
        ik = jnp.min(jnp.where(sel == m, row, N_EXPERTS), axis=0, keepdims=True)
        oh = row == ik
        sel = jnp.where(oh, -jnp.inf, sel)
        onehot = jnp.where(oh, 1.0, onehot)
        picks.append((ik, oh))
    ri = lax.broadcasted_iota(jnp.int32, (tm, tm), 0)
    ci = lax.broadcasted_iota(jnp.int32, (tm, tm), 1)
    earlier = (ri < ci).astype(BF16)
    before = jnp.dot(onehot.astype(BF16), earlier, preferred_element_type=F32) + run_ref[...]
    affs = [jnp.sum(jnp.where(oh, aff, 0.0), axis=0, keepdims=True) for _, oh in picks]
    tot = affs[0]
    for a in affs[1:]:
        tot = tot + a
    for k, (ik, oh) in enumerate(picks):
        rank = jnp.sum(jnp.where(oh, before, 0.0), axis=0, keepdims=True).astype(jnp.int32)
        pk_ref[k:k + 1, :] = ik * (1 << RANK_BITS) + rank
        gw_ref[k:k + 1, :] = affs[k] / tot * ROUTED_SCALE
    run = run_ref[...] + jnp.sum(onehot, axis=1, keepdims=True)
    run_ref[...] = run
    cnt_ref[...] = run.astype(jnp.int32)


def _route(h2, wrt_bf, b_router, *, tm):
    t, d = h2.shape[0] // NCH, D_MODEL
    return pl.pallas_call(
        functools.partial(_route_kernel, tm=tm),
        grid=(t // tm,),
        in_specs=[pl.BlockSpec((tm * NCH, LANES), lambda i: (i, 0)),
                  pl.BlockSpec((N_EXPERTS, d), lambda i: (0, 0)),
                  pl.BlockSpec((N_EXPERTS, 1), lambda i: (0, 0))],
        out_specs=(pl.BlockSpec((TOP_K, tm), lambda i: (0, i)),
                   pl.BlockSpec((TOP_K, tm), lambda i: (0, i)),
                   pl.BlockSpec((N_EXPERTS, 1), lambda i: (0, 0))),
        out_shape=(jax.ShapeDtypeStruct((TOP_K, t), jnp.int32),
                   jax.ShapeDtypeStruct((TOP_K, t), F32),
                   jax.ShapeDtypeStruct((N_EXPERTS, 1), jnp.int32)),
        scratch_shapes=[pltpu.VMEM((N_EXPERTS, 1), F32)],
        compiler_params=_cparams(("arbitrary",)),
        name="route",
    )(h2, wrt_bf, b_router.reshape(N_EXPERTS, 1))


def _idx_copy(idx_hbm, idx_smem, sem, step, n, half):
    return pltpu.make_async_copy(idx_hbm.at[pl.ds(pl.multiple_of(step * n, n), n)],
                                 idx_smem.at[pl.ds(half * n, n)], sem.at[half])


def _slots_kernel(pk_ref, offs_ref, dst_ref):
    half = N_EXPERTS // 2
    shape = (TOP_K, LANES)
    lo_tab = jnp.broadcast_to(offs_ref[:, 0:half], shape)
    hi_tab = jnp.broadcast_to(offs_ref[:, half:N_EXPERTS], shape)
    for j in range(pk_ref.shape[1] // LANES):
        lanes = slice(j * LANES, (j + 1) * LANES)
        p = pk_ref[:, lanes]
        e = p >> RANK_BITS
        start = jnp.where(e >= half, jnp.take_along_axis(hi_tab, e & (half - 1), axis=1),
                          jnp.take_along_axis(lo_tab, e & (half - 1), axis=1))
        dst_ref[:, lanes] = ((p & ((1 << RANK_BITS) - 1)) + start) * NCH


def _slots(pk, offs, *, tm):
    t = pk.shape[1]
    return pl.pallas_call(
        _slots_kernel,
        grid=(t // tm,),
        in_specs=[pl.BlockSpec((TOP_K, tm), lambda i: (0, i)),
                  pl.BlockSpec((1, N_EXPERTS), lambda i: (0, 0))],
        out_specs=pl.BlockSpec((TOP_K, tm), lambda i: (0, i)),
        out_shape=jax.ShapeDtypeStruct((TOP_K, t), jnp.int32),
        compiler_params=_cparams(("arbitrary",)),
        name="slots",
    )(pk, offs[:N_EXPERTS].reshape(1, N_EXPERTS))


def _dispatch_kernel(h2_ref, dst_hbm, xs_hbm, idx_smem, idx_sem, row_sem, *, tm):
    i = pl.program_id(0)
    n = tm * TOP_K

    @pl.when(i == 0)
    def _():
        _idx_copy(dst_hbm, idx_smem, idx_sem, i, n, 0).start()

    def scatter(half):
        _idx_copy(dst_hbm, idx_smem, idx_sem, i, n, half).wait()

        @pl.when(i + 1 < pl.num_programs(0))
        def _():
            _idx_copy(dst_hbm, idx_smem, idx_sem, i + 1, n, 1 - half).start()

        def body(t8, carry):
            r0 = pl.multiple_of(t8 * (8 * NCH), 8 * NCH)
            b = half * n + t8 * (8 * TOP_K)
            for tt in range(8):
                src = h2_ref.at[pl.ds(r0 + tt * NCH, NCH)]
                for k in range(TOP_K):
                    dst = pl.multiple_of(idx_smem[b + tt * TOP_K + k], NCH)
                    pltpu.make_async_copy(src, xs_hbm.at[pl.ds(dst, NCH)], row_sem).start(priority=k % 2)
            return carry

        lax.fori_loop(0, tm // 8, body, 0)

    for half in range(2):
        pl.when(i % 2 == half)(functools.partial(scatter, half))
    pltpu.make_async_copy(xs_hbm.at[pl.ds(0, n * NCH)], xs_hbm.at[pl.ds(0, n * NCH)], row_sem).wait()


def _dispatch(h2, dst_flat, *, tm):
    t = h2.shape[0] // NCH
    n = tm * TOP_K
    return pl.pallas_call(
        functools.partial(_dispatch_kernel, tm=tm),
        grid=(t // tm,),
        in_specs=[pl.BlockSpec((tm * NCH, LANES), lambda i: (i, 0)),
                  pl.BlockSpec(memory_space=pl.ANY)],
        out_specs=pl.BlockSpec(memory_space=pl.ANY),
        scratch_shapes=[pltpu.SMEM((2 * n,), jnp.int32),
                        pltpu.SemaphoreType.DMA((2,)),
                        pltpu.SemaphoreType.DMA(())],
        out_shape=jax.ShapeDtypeStruct((t * TOP_K * NCH, LANES), F32),
        compiler_params=_cparams(("arbitrary",)),
        name="dispatch",
    )(h2, dst_flat)


def _experts_kernel(vt_ref, vg_ref, offs_ref, nv_ref, xs_ref, wg_ref, wu_ref, wd_ref, ys_ref,
                    wgb_ref, wub_ref, wdb_ref, *, tm):
    v = pl.program_id(0)

    @pl.when(v < nv_ref[0])
    def _():
        g = vg_ref[v]
        tile = vt_ref[v]
        prev = jnp.maximum(v - 1, 0)
        first = jnp.logical_or(v == 0, vt_ref[prev] != tile)

        @pl.when(jnp.logical_or(v == 0, vg_ref[prev] != g))
        def _():
            wgb_ref[...] = wg_ref[0].astype(BF16)
            wub_ref[...] = wu_ref[0].astype(BF16)
            wdb_ref[...] = wd_ref[0].astype(BF16)

        def swiglu_rows(r0, nr, masked):
            x = jnp.concatenate([xs_ref[pl.ds(r0 * NCH + c, nr, stride=NCH), :] for c in range(NCH)],
                                axis=1).astype(BF16)
            hg = jnp.dot(x, wgb_ref[...], preferred_element_type=F32)
            hu = jnp.dot(x, wub_ref[...], preferred_element_type=F32)
            hm = (hg * jax.nn.sigmoid(hg) * hu).astype(BF16)
            if masked:
                rowid = tile * tm + r0 + lax.broadcasted_iota(jnp.int32, (nr, 1), 0)
                mine = jnp.logical_and(rowid >= offs_ref[g], rowid < offs_ref[g + 1])
            for c2 in range(NCH // 2):
                y2 = jnp.dot(hm, wdb_ref[:, c2 * 2 * LANES:(c2 + 1) * 2 * LANES], preferred_element_type=F32)
                for j in range(2):
                    dst = pl.ds(r0 * NCH + 2 * c2 + j, nr, stride=NCH)
                    yc = y2[:, j * LANES:(j + 1) * LANES]
                    ys_ref[dst, :] = jnp.where(mine, yc, ys_ref[dst, :]) if masked else yc

        def visit(masked):
            for r0 in range(0, tm, tm // 2):
                swiglu_rows(r0, tm // 2, masked)

        pl.when(first)(functools.partial(visit, False))
        pl.when(jnp.logical_not(first))(functools.partial(visit, True))


def _experts(vt, vg, offs, nv, xs, w_gate, w_up, w_down, *, tm):
    m, d = xs.shape[0] // NCH, D_MODEL
    nvis = vt.shape[0]
    de = w_gate.shape[2]
    xmap = lambda v, vt, vg, offs, nv: (vt[v], 0)
    wmap = lambda v, vt, vg, offs, nv: (vg[v], 0, 0)
    grid_spec = pltpu.PrefetchScalarGridSpec(
        num_scalar_prefetch=4,
        grid=(nvis,),
        in_specs=[pl.BlockSpec((tm * NCH, LANES), xmap),
                  pl.BlockSpec((1, d, de), wmap),
                  pl.BlockSpec((1, d, de), wmap),
                  pl.BlockSpec((1, de, d), wmap)],
        out_specs=pl.BlockSpec((tm * NCH, LANES), xmap),
        scratch_shapes=[pltpu.VMEM((d, de), BF16), pltpu.VMEM((d, de), BF16), pltpu.VMEM((de, d), BF16)],
    )
    return pl.pallas_call(
        functools.partial(_experts_kernel, tm=tm),
        grid_spec=grid_spec,
        out_shape=jax.ShapeDtypeStruct((m * NCH, LANES), F32),
        compiler_params=_cparams(("arbitrary",)),
        name="experts",
    )(vt, vg, offs, nv, xs, w_gate, w_up, w_down)


def _combine_kernel(x1_ref, h2_ref, gw_ref, gt2_ref, wsg_ref, wsu_ref, wsd_ref, dst_hbm, ys_hbm,
                    y_ref, idx_smem, buf_ref, acc_ref, idx_sem, row_sem, *, tm, step0, per_token_gate):
    i = pl.program_id(0)
    nsteps = pl.num_programs(0)
    n = tm * TOP_K

    def issue(b, r0, half):
        for tt in range(8):
            for k in range(TOP_K):
                src = pl.multiple_of(idx_smem[b + tt * TOP_K + k], NCH)
                pltpu.make_async_copy(ys_hbm.at[pl.ds(src, NCH)],
                                      buf_ref.at[half, k, pl.ds(r0 + tt * NCH, NCH)],
                                      row_sem.at[half]).start(priority=k % 2)

    def finish8(t8, cur):
        t0 = pl.multiple_of(t8 * 8, 8)
        gw = gw_ref[pl.ds(t0, 8), :]
        for c in range(NCH):
            lanes = slice(c * LANES, (c + 1) * LANES)
            ffn = acc_ref[pl.ds(t0, 8), lanes]
            for k in range(TOP_K):
                ffn = ffn + gw[:, k:k + 1] * buf_ref[cur, k, pl.ds(t0 * NCH + c, 8, stride=NCH), :]
            gate = gt2_ref[pl.ds(t0, 8), lanes] if per_token_gate else gt2_ref[0][:, lanes]
            y_ref[pl.ds(t0, 8), lanes] = x1_ref[pl.ds(t0, 8), lanes] + gate * ffn

    @pl.when(i == 0)
    def _():
        _idx_copy(dst_hbm, idx_smem, idx_sem, step0, n, 0).start()
        _idx_copy(dst_hbm, idx_smem, idx_sem, step0, n, 0).wait()

        def body(t8, carry):
            issue(t8 * (8 * TOP_K), pl.multiple_of(t8 * (8 * NCH), 8 * NCH), 0)
            return carry

        lax.fori_loop(0, tm // 8, body, 0)

        @pl.when(nsteps > 1)
        def _():
            _idx_copy(dst_hbm, idx_smem, idx_sem, step0 + 1, n, 1).start()

    h2 = _load_rows(h2_ref, tm).astype(BF16)
    sg = jnp.dot(h2, wsg_ref[...], preferred_element_type=F32)
    su = jnp.dot(h2, wsu_ref[...], preferred_element_type=F32)
    acc_ref[...] = jnp.dot((sg * jax.nn.sigmoid(sg) * su).astype(BF16), wsd_ref[...], preferred_element_type=F32)

    def step(cur, has_next):
        pltpu.make_async_copy(ys_hbm.at[pl.ds(0, n * NCH)], ys_hbm.at[pl.ds(0, n * NCH)], row_sem.at[cur]).wait()
        if has_next:
            _idx_copy(dst_hbm, idx_smem, idx_sem, step0 + i + 1, n, 1 - cur).wait()

        def body(t8, carry):
            if has_next:
                issue((1 - cur) * n + t8 * (8 * TOP_K), pl.multiple_of(t8 * (8 * NCH), 8 * NCH), 1 - cur)
            finish8(t8, cur)
            return carry

        lax.fori_loop(0, tm // 8, body, 0)

        if has_next:
            @pl.when(i + 2 < nsteps)
            def _():
                _idx_copy(dst_hbm, idx_smem, idx_sem, step0 + i + 2, n, cur).start()

    for cur in range(2):
        pl.when(jnp.logical_and(i % 2 == cur, i + 1 < nsteps))(functools.partial(step, cur, True))
        pl.when(jnp.logical_and(i % 2 == cur, i + 1 == nsteps))(functools.partial(step, cur, False))


def _combine(x1, h2, gw, gt2, wsg, wsu, wsd, dst_flat, ys, *, tm, batch, seq, row_offset):
    d = x1.shape[1]
    n = tm * TOP_K
    per_token_gate = gt2.ndim == 2
    assert per_token_gate or seq % tm == 0
    per_b = max(seq // tm, 1)
    step0 = row_offset // tm
    tok = lambda i: (step0 + i, 0)
    c2 = lambda i: (0, 0)
    gate_spec = (pl.BlockSpec((tm, d), lambda i: (i, 0)) if per_token_gate
                 else pl.BlockSpec((1, 1, d), lambda i: (i // per_b, 0, 0)))
    return pl.pallas_call(
        functools.partial(_combine_kernel, tm=tm, step0=step0, per_token_gate=per_token_gate),
        grid=(batch * seq // tm,),
        in_specs=[pl.BlockSpec((tm, d), tok),
                  pl.BlockSpec((tm * NCH, LANES), tok),
                  pl.BlockSpec((tm, 128), tok),
                  gate_spec,
                  pl.BlockSpec(wsg.shape, c2),
                  pl.BlockSpec(wsu.shape, c2),
                  pl.BlockSpec(wsd.shape, c2),
                  pl.BlockSpec(memory_space=pl.ANY),
                  pl.BlockSpec(memory_space=pl.ANY)],
        out_specs=pl.BlockSpec((tm, d), lambda i: (i, 0)),
        scratch_shapes=[pltpu.SMEM((2 * n,), jnp.int32),
                        pltpu.VMEM((2, TOP_K, tm * NCH, LANES), F32),
                        pltpu.VMEM((tm, d), F32),
                        pltpu.SemaphoreType.DMA((2,)),
                        pltpu.SemaphoreType.DMA((2,))],
        out_shape=jax.ShapeDtypeStruct((batch * seq, d), F32),
        compiler_params=_cparams(("arbitrary",)),
        name="combine",
    )(x1, h2, gw, gt2, wsg, wsu, wsd, dst_flat, ys)


def _bias_table(rel_bias, sb, kw):
    h, n_rel = rel_bias.shape
    lo = (kw - 1) - ATT_PAST_WINDOW - (CHUNK - 1)
    ntot = sb + kw - 1
    s = jnp.concatenate([jnp.broadcast_to(rel_bias[:, :1], (h, lo)), rel_bias,
                         jnp.broadcast_to(rel_bias[:, -1:], (h, ntot - lo - n_rel))], axis=1).astype(F32)
    s_rev = jnp.pad(s[:, ::-1], ((0, 0), (0, 1)))
    skew = jnp.broadcast_to(s_rev[:, None, :], (h, sb, ntot + 1)).reshape(h, -1)[:, :sb * ntot]
    toep = skew.reshape(h, sb, ntot)[:, :, sb - 1:sb - 1 + kw]
    r = np.arange(sb)[:, None]
    kk = np.arange(kw)[None, :]
    qc, kc = r // CHUNK, kk // CHUNK
    band = (kc >= qc) & (kc <= qc + LEFT_CHUNKS)
    return jnp.where(band[None], toep, NEG_INF)


def _visit_plan(offs, m, tm, nvis):
    cnt = offs[1:] - offs[:-1]
    first = offs[:-1] // tm
    last = jnp.maximum(offs[1:] - 1, 0) // tm
    per = jnp.where(cnt > 0, last - first + 1, 0)
    vend = jnp.cumsum(per)
    vstart = vend - per
    total = vend[-1]
    v = jnp.arange(nvis, dtype=jnp.int32)
    g = jnp.minimum(jnp.sum(vend[None, :] <= v[:, None], axis=1), N_EXPERTS - 1).astype(jnp.int32)
    tile = (first[g] + v - vstart[g]).astype(jnp.int32)
    live = v < total
    last_tile = m // tm - 1
    tile = jnp.where(live, tile, last_tile)
    g = jnp.where(live, g, N_EXPERTS - 1)
    return tile, g, total.reshape(1).astype(jnp.int32)


def kernel(x_prompt, x_sample, c_prompt, c_sample, cache_k, cache_v, w_ada, b_ada, g_mix, w_in, g_gmlp_v,
           w_spatial, b_spatial, g_q, g_k, rel_bias, g_out_a, g_out_b, w_out, g_ffn, w_router, b_router,
           w_exp_gate, w_exp_up, w_exp_down, w_sh_gate, w_sh_up, w_sh_down):
    depth = w_ada.shape[0]
    assert depth == 1, "single layer"
    bp, sp, d = x_prompt.shape
    bs, ss, _ = x_sample.shape
    tp, ts = bp * sp, bs * ss
    t_all = tp + ts
    l = 0

    c_all = jnp.concatenate([c_prompt, c_sample], axis=0)
    nb = c_all.shape[0]
    nb_pad = -(-nb // 8) * 8
    c_all = jnp.pad(c_all, ((0, nb_pad - nb), (0, 0)))
    mod = _ada(c_all, w_ada[l], b_ada[l])[:nb].reshape(nb, 6, 1, d)
    sh1, sc1, gt1, sh2, sc2, gt2 = [mod[:, i] for i in range(6)]

    w_in_bf = w_in[l].astype(BF16)
    w_out_bf = w_out[l].astype(BF16)
    gv = g_gmlp_v[l].reshape(1, W_A)
    gq = jnp.tile(g_q[l], B_HEADS).reshape(1, W_B)
    gk = jnp.tile(g_k[l], B_HEADS).reshape(1, W_B)
    hid = np.arange(W_B) // HEAD_DIM
    hsum = jnp.asarray(hid[:, None] == hid[None, :], BF16)
    gmix = g_mix[l].reshape(1, d)
    goa = g_out_a[l].reshape(1, W_A)
    gob = g_out_b[l].reshape(1, W_B)
    gffn = g_ffn[l].reshape(1, d)
    bias_p = _bias_table(rel_bias[l], GMLP_BLOCK, ATT_PAST_WINDOW + GMLP_BLOCK)
    bsp_p = jnp.broadcast_to(b_spatial[l][:, :, None], (A_GROUPS, GMLP_BLOCK, A_GROUP_DIM))

    cl = cache_k.shape[2]
    us, vns, qs, ks_bf, vs_bf, k32s, v32s = _in_proj(
        x_sample, sh1[bp:], sc1[bp:], gmix, w_in_bf, gv, gq, gk, hsum,
        tm=ss, pad_rows=0, keep=ss, vn_dtype=F32)
    kcat = jnp.concatenate([cache_k[l].reshape(bs, cl, W_B).astype(BF16), ks_bf], axis=1)
    vcat = jnp.concatenate([cache_v[l].reshape(bs, cl, W_B).astype(BF16), vs_bf], axis=1)
    bias_s = bias_p[:, :ss, ATT_PAST_WINDOW - cl:ATT_PAST_WINDOW + ss]
    x1_s, h2_s = _mixer(x_sample, us, vns, qs, kcat, vcat, bias_s, w_spatial[l][:, :ss, :ss], bsp_p[:, :ss],
                        goa, gob, w_out_bf, gt1[bp:], gffn, sh2[bp:], sc2[bp:],
                        sb=ss, nsb=1, padded=False)
    new_k_sample = k32s.reshape(1, bs, ss, B_HEADS, HEAD_DIM)
    new_v_sample = v32s.reshape(1, bs, ss, B_HEADS, HEAD_DIM)
    new_gmlp_v_sample = vns.reshape(1, bs, ss, A_GROUPS, A_GROUP_DIM)

    keep = min(ATT_PAST_WINDOW, sp)
    u, vn, q, kpad, vpad, k32, v32 = _in_proj(
        x_prompt, sh1[:bp], sc1[:bp], gmix, w_in_bf, gv, gq, gk, hsum,
        tm=512, pad_rows=ATT_PAST_WINDOW, keep=keep, vn_dtype=BF16)
    x1, h2 = _mixer(x_prompt, u, vn, q, kpad, vpad, bias_p, w_spatial[l], bsp_p, goa, gob, w_out_bf,
                    gt1[:bp], gffn, sh2[:bp], sc2[:bp],
                    sb=GMLP_BLOCK, nsb=2, padded=True, tail=(x1_s, h2_s))
    assert x1.shape[0] == t_all
    new_k_prompt = k32.reshape(1, bp, keep, B_HEADS, HEAD_DIM)
    new_v_prompt = v32.reshape(1, bp, keep, B_HEADS, HEAD_DIM)

    tmr = 256
    tmd = 512 if t_all % 512 == 0 else 256
    tme = 1024 if (t_all * TOP_K) % 1024 == 0 else 512
    assert t_all % tmr == 0 and (t_all * TOP_K) % tme == 0 and sp % 256 == 0 and ts % (2 * ss) == 0
    assert t_all * TOP_K < (1 << RANK_BITS)
    pk, gw_t, cnt = _route(h2, w_router[l].T.astype(BF16), b_router[l], tm=tmr)
    offs = jnp.concatenate([jnp.zeros((1,), jnp.int32), jnp.cumsum(cnt[:, 0]).astype(jnp.int32)])
    tms = next(c for c in (2048, 1536, 1024, 512, 256) if t_all % c == 0)
    dst_flat = _slots(pk, offs, tm=tms).T.reshape(-1)
    gw = jnp.pad(gw_t.T, ((0, 0), (0, 128 - TOP_K)))
    xs = _dispatch(h2, dst_flat, tm=tmd)
    m = t_all * TOP_K
    nvis = m // tme + N_EXPERTS - 1
    vt, vg, nv = _visit_plan(offs, m, tme, nvis)
    ys = _experts(vt, vg, offs, nv, xs, w_exp_gate[l], w_exp_up[l], w_exp_down[l], tm=tme)

    wsg, wsu, wsd = w_sh_gate[l].astype(BF16), w_sh_up[l].astype(BF16), w_sh_down[l].astype(BF16)
    y_prompt = _combine(x1, h2, gw, gt2[:bp], wsg, wsu, wsd, dst_flat, ys,
                        tm=256, batch=bp, seq=sp, row_offset=0).reshape(bp, sp, d)
    gt2_tok = jnp.broadcast_to(gt2[bp:], (bs, ss, d)).reshape(ts, d)
    y_sample = _combine(x1, h2, gw, gt2_tok, wsg, wsu, wsd, dst_flat, ys,
                        tm=2 * ss, batch=bs, seq=ss, row_offset=tp).reshape(bs, ss, d)
    return (y_prompt, y_sample, new_k_prompt, new_v_prompt, new_k_sample, new_v_sample, new_gmlp_v_sample)
```

```python
import functools

import jax
import jax.numpy as jnp
import numpy as np
from jax import lax
from jax.experimental import pallas as pl
from jax.experimental.pallas import tpu as pltpu

F32 = jnp.float32
BF16 = jnp.bfloat16

D_MODEL = 1024
CHUNK = 64
GMLP_BLOCK = 128
W_A = 512
A_GROUPS = 4
A_GROUP_DIM = 128
W_B = 512
B_HEADS = 8
HEAD_DIM = 64
LEFT_CHUNKS = 8
ATT_PAST_WINDOW = LEFT_CHUNKS * CHUNK
REL_CLIP = 128
D_IN = 2 * W_A + 3 * W_B
N_EXPERTS = 256
TOP_K = 8
D_EXPERT = 256
ROUTED_SCALE = 2.5
EPS = 1e-6
NEG_INF = -1e30

RANK_BITS = 20
VMEM_LIMIT = 56 * 1024 * 1024


def _cparams(sem, vmem=VMEM_LIMIT):
    return pltpu.CompilerParams(dimension_semantics=sem, vmem_limit_bytes=vmem)


def _rms(x, axis=-1):
    return lax.rsqrt(jnp.mean(x * x, axis=axis, keepdims=True) + EPS)


LANES = 128
NCH = D_MODEL // LANES


def _load_rows(ref, n, lead=()):
    return jnp.concatenate([ref[(*lead, pl.ds(c, n, stride=NCH), slice(None))] for c in range(NCH)], axis=1)


def _store_rows(ref, val):
    n = val.shape[0]
    for c in range(NCH):
        ref[pl.ds(c, n, stride=NCH), :] = val[:, c * LANES:(c + 1) * LANES]


def _ada_kernel(c_ref, w_ref, b_ref, o_ref):
    c = c_ref[...]
    s = c * jax.nn.sigmoid(c)
    o_ref[...] = jnp.dot(s.astype(BF16), w_ref[...].astype(BF16), preferred_element_type=F32) + b_ref[...]


def _ada(c, w_ada, b_ada):
    n, d = c.shape
    nout = w_ada.shape[1]
    tn = 1024
    return pl.pallas_call(
        _ada_kernel,
        grid=(nout // tn,),
        in_specs=[pl.BlockSpec((n, d), lambda j: (0, 0)),
                  pl.BlockSpec((d, tn), lambda j: (0, j)),
                  pl.BlockSpec((1, tn), lambda j: (0, j))],
        out_specs=pl.BlockSpec((n, tn), lambda j: (0, j)),
        out_shape=jax.ShapeDtypeStruct((n, nout), F32),
        compiler_params=_cparams(("arbitrary",)),
        name="ada",
    )(c, w_ada, b_ada.reshape(1, nout))


def _in_proj_kernel(x_ref, sh_ref, sc_ref, gmix_ref, win_ref, gv_ref, gq_ref, gk_ref, hsum_ref,
                    u_ref, vn_ref, q_ref, kp_ref, vp_ref, k32_ref, v32_ref, *, npad, first_kept):
    i = pl.program_id(1)

    @pl.when(i < npad)
    def _():
        kp_ref[...] = jnp.zeros_like(kp_ref)
        vp_ref[...] = jnp.zeros_like(vp_ref)

    @pl.when(i >= npad)
    def _():
        x = x_ref[0]
        h = x * _rms(x) * gmix_ref[...]
        h = h * (1.0 + sc_ref[0]) + sh_ref[0]
        proj = jnp.dot(h.astype(BF16), win_ref[...], preferred_element_type=F32)
        u = jax.nn.gelu(proj[:, 0:W_A])
        u_ref[0] = u.astype(u_ref.dtype)
        v = jax.nn.gelu(proj[:, W_A:2 * W_A])
        for g in range(A_GROUPS):
            vg = v[:, g * A_GROUP_DIM:(g + 1) * A_GROUP_DIM]
            vn = vg * _rms(vg) * gv_ref[:, g * A_GROUP_DIM:(g + 1) * A_GROUP_DIM]
            vn_ref[0, :, g * A_GROUP_DIM:(g + 1) * A_GROUP_DIM] = vn.astype(vn_ref.dtype)
        q = proj[:, 2 * W_A:2 * W_A + W_B]
        k = proj[:, 2 * W_A + W_B:2 * W_A + 2 * W_B]
        va = proj[:, 2 * W_A + 2 * W_B:]
        q2 = jnp.dot((q * q).astype(BF16), hsum_ref[...], preferred_element_type=F32) * (1.0 / HEAD_DIM)
        k2 = jnp.dot((k * k).astype(BF16), hsum_ref[...], preferred_element_type=F32) * (1.0 / HEAD_DIM)
        qn = q * lax.rsqrt(q2 + EPS) * gq_ref[...]
        kn = k * lax.rsqrt(k2 + EPS) * gk_ref[...]
        q_ref[0] = (qn * (HEAD_DIM ** -0.5)).astype(q_ref.dtype)
        kp_ref[0] = kn.astype(kp_ref.dtype)
        vp_ref[0] = va.astype(vp_ref.dtype)

        @pl.when(i >= first_kept)
        def _():
            k32_ref[0] = kn
            v32_ref[0] = va


def _in_proj(x, sh1, sc1, g_mix, w_in_bf, gv, gq, gk, hsum, *, tm, pad_rows, keep, vn_dtype):
    b, l, d = x.shape
    assert l % tm == 0 and pad_rows % tm == 0 and keep % tm == 0
    npad = pad_rows // tm
    nt = l // tm
    first_kept = npad + nt - keep // tm
    row = lambda bi, i: (bi, jnp.maximum(i - npad, 0), 0)
    kept = lambda bi, i: (bi, jnp.maximum(i - first_kept, 0), 0)
    bvec = lambda bi, i: (bi, 0, 0)
    const2 = lambda bi, i: (0, 0)
    out_shapes = (
        jax.ShapeDtypeStruct((b, l, W_A), BF16),
        jax.ShapeDtypeStruct((b, l, W_A), vn_dtype),
        jax.ShapeDtypeStruct((b, l, W_B), BF16),
        jax.ShapeDtypeStruct((b, pad_rows + l, W_B), BF16),
        jax.ShapeDtypeStruct((b, pad_rows + l, W_B), BF16),
        jax.ShapeDtypeStruct((b, keep, W_B), F32),
        jax.ShapeDtypeStruct((b, keep, W_B), F32),
    )
    padrow = lambda bi, i: (bi, i, 0)
    return pl.pallas_call(
        functools.partial(_in_proj_kernel, npad=npad, first_kept=first_kept),
        grid=(b, nt + npad),
        in_specs=[pl.BlockSpec((1, tm, d), row),
                  pl.BlockSpec((1, 1, d), bvec),
                  pl.BlockSpec((1, 1, d), bvec),
                  pl.BlockSpec((1, d), const2),
                  pl.BlockSpec((d, D_IN), const2),
                  pl.BlockSpec((1, W_A), const2),
                  pl.BlockSpec((1, W_B), const2),
                  pl.BlockSpec((1, W_B), const2),
                  pl.BlockSpec((W_B, W_B), const2)],
        out_specs=(pl.BlockSpec((1, tm, W_A), row),
                   pl.BlockSpec((1, tm, W_A), row),
                   pl.BlockSpec((1, tm, W_B), row),
                   pl.BlockSpec((1, tm, W_B), padrow),
                   pl.BlockSpec((1, tm, W_B), padrow),
                   pl.BlockSpec((1, tm, W_B), kept),
                   pl.BlockSpec((1, tm, W_B), kept)),
        out_shape=out_shapes,
        compiler_params=_cparams(("arbitrary", "arbitrary")),
        name="in_proj",
    )(x, sh1, sc1, g_mix, w_in_bf, gv, gq, gk, hsum)


def _mixer_kernel(x_ref, u_ref, vn_ref, q_ref, kp_ref, vp_ref, bias_ref, wsp_ref, bsp_ref,
                  goa_ref, gob_ref, wout_ref, gt1_ref, gffn_ref, sh2_ref, sc2_ref, *rest,
                  sb, nsb, kw, padded, nj, nmain):
    x1_ref, h2_ref, merged_ref = rest[-3:]
    i = pl.program_id(0)
    if len(rest) == 5:
        tx1_ref, th2_ref = rest[:2]

        @pl.when(i >= nmain)
        def _():
            x1_ref[...] = tx1_ref[...]
            h2_ref[...] = th2_ref[...]

    @pl.when(i < nmain)
    def _():
        _mixer_body(x_ref, u_ref, vn_ref, q_ref, kp_ref, vp_ref, bias_ref, wsp_ref, bsp_ref,
                    goa_ref, gob_ref, wout_ref, gt1_ref, gffn_ref, sh2_ref, sc2_ref,
                    x1_ref, h2_ref, merged_ref, i % nj, sb=sb, nsb=nsb, kw=kw, padded=padded)


def _mixer_body(x_ref, u_ref, vn_ref, q_ref, kp_ref, vp_ref, bias_ref, wsp_ref, bsp_ref,
                goa_ref, gob_ref, wout_ref, gt1_ref, gffn_ref, sh2_ref, sc2_ref,
                x1_ref, h2_ref, merged_ref, j, *, sb, nsb, kw, padded):
    rows = sb * nsb
    ri = lax.broadcasted_iota(jnp.int32, (sb, sb), 0)
    ci = lax.broadcasted_iota(jnp.int32, (sb, sb), 1)
    gmask = (ci // CHUNK) <= (ri // CHUNK)
    lane = lax.broadcasted_iota(jnp.int32, (sb, 2 * HEAD_DIM), 1)
    lo_half = lane < HEAD_DIM
    kidx = lax.broadcasted_iota(jnp.int32, (sb, kw), 1)

    for s in range(nsb):
        r0 = s * sb
        vn = vn_ref[0, r0:r0 + sb, :].astype(BF16)
        u = u_ref[0, r0:r0 + sb, :].astype(F32)
        zs = []
        for g in range(A_GROUPS):
            w = jnp.where(gmask, wsp_ref[g], 0.0).astype(BF16)
            z = jnp.dot(w, vn[:, g * A_GROUP_DIM:(g + 1) * A_GROUP_DIM], preferred_element_type=F32)
            zs.append(z + bsp_ref[g])
        a = u * jnp.concatenate(zs, axis=1)
        a = a * _rms(a) * goa_ref[...]
        merged_ref[r0:r0 + sb, 0:W_A] = a.astype(BF16)

        if padded:
            blk0 = j * rows + r0
            kstart = pl.multiple_of(blk0, sb)
            kb = kp_ref[0, pl.ds(kstart, kw), :]
            vb = vp_ref[0, pl.ds(kstart, kw), :]
            valid = kidx >= (ATT_PAST_WINDOW - blk0)
        else:
            kb = kp_ref[0]
            vb = vp_ref[0]
            valid = None
        qb = q_ref[0, r0:r0 + sb, :]
        scs = []
        for h in range(B_HEADS):
            c0 = (h // 2) * 2 * HEAD_DIM
            qp = qb[:, c0:c0 + 2 * HEAD_DIM]
            qm = jnp.where(lo_half if h % 2 == 0 else jnp.logical_not(lo_half), qp, jnp.zeros_like(qp))
            sc = lax.dot_general(qm, kb[:, c0:c0 + 2 * HEAD_DIM], (((1,), (1,)), ((), ())),
                                 preferred_element_type=F32)
            scs.append(sc + bias_ref[h])
        ps, ls = [], []
        for sc in scs:
            if valid is not None:
                sc = jnp.where(valid, sc, NEG_INF)
            p = jnp.exp(sc - jnp.max(sc, axis=-1, keepdims=True))
            ls.append(jnp.sum(p, axis=-1, keepdims=True))
            ps.append(p.astype(BF16))
        os_ = []
        for h in range(B_HEADS):
            c0 = (h // 2) * 2 * HEAD_DIM
            os_.append(jnp.dot(ps[h], vb[:, c0:c0 + 2 * HEAD_DIM], preferred_element_type=F32) / ls[h])
        outs = [jnp.where(lo_half, os_[2 * hp], os_[2 * hp + 1]) for hp in range(B_HEADS // 2)]
        bo = jnp.concatenate(outs, axis=1)
        bo = bo * _rms(bo) * gob_ref[...]
        merged_ref[r0:r0 + sb, W_A:W_A + W_B] = bo.astype(BF16)

    mix = jnp.dot(merged_ref[...], wout_ref[...], preferred_element_type=F32)
    x1 = x_ref[0] + gt1_ref[0] * mix
    x1_ref[...] = x1
    h2 = x1 * _rms(x1) * gffn_ref[...]
    _store_rows(h2_ref, h2 * (1.0 + sc2_ref[0]) + sh2_ref[0])


def _mixer(x, u, vn, q, kp, vp, bias, wsp, bsp, goa, gob, wout_bf, gt1, gffn, sh2, sc2,
           *, sb, nsb, padded, tail=None):
    b, l, d = x.shape
    rows = sb * nsb
    assert l % rows == 0
    nj = l // rows
    nmain = b * nj
    ntail = 0
    if tail is not None:
        assert tail[0].shape[0] % rows == 0
        ntail = tail[0].shape[0] // rows
    total_rows = (nmain + ntail) * rows
    kw = bias.shape[2]
    kl = kp.shape[1]
    bidx = lambda i: jnp.minimum(i // nj, b - 1)
    row = lambda i: (bidx(i), i % nj, 0)
    bvec = lambda i: (bidx(i), 0, 0)
    c2 = lambda i: (0, 0)
    c3 = lambda i: (0, 0, 0)
    pool = lambda i: (i, 0)
    in_specs = [pl.BlockSpec((1, rows, d), row),
                pl.BlockSpec((1, rows, W_A), row),
                pl.BlockSpec((1, rows, W_A), row),
                pl.BlockSpec((1, rows, W_B), row),
                pl.BlockSpec((1, kl, W_B), bvec),
                pl.BlockSpec((1, kl, W_B), bvec),
                pl.BlockSpec(bias.shape, c3),
                pl.BlockSpec(wsp.shape, c3),
                pl.BlockSpec(bsp.shape, c3),
                pl.BlockSpec((1, W_A), c2),
                pl.BlockSpec((1, W_B), c2),
                pl.BlockSpec((d, d), c2),
                pl.BlockSpec((1, 1, d), bvec),
                pl.BlockSpec((1, d), c2),
                pl.BlockSpec((1, 1, d), bvec),
                pl.BlockSpec((1, 1, d), bvec)]
    args = [x, u, vn, q, kp, vp, bias, wsp, bsp, goa, gob, wout_bf, gt1, gffn, sh2, sc2]
    if tail is not None:
        tmap = lambda i: (jnp.maximum(i - nmain, 0), 0)
        in_specs += [pl.BlockSpec((rows, d), tmap), pl.BlockSpec((rows * NCH, LANES), tmap)]
        args += list(tail)
    return pl.pallas_call(
        functools.partial(_mixer_kernel, sb=sb, nsb=nsb, kw=kw, padded=padded, nj=nj, nmain=nmain),
        grid=(nmain + ntail,),
        in_specs=in_specs,
        out_specs=(pl.BlockSpec((rows, d), pool), pl.BlockSpec((rows * NCH, LANES), pool)),
        out_shape=(jax.ShapeDtypeStruct((total_rows, d), F32),
                   jax.ShapeDtypeStruct((total_rows * NCH, LANES), F32)),
        scratch_shapes=[pltpu.VMEM((rows, d), BF16)],
        compiler_params=_cparams(("arbitrary",)),
        name="mixer",
    )(*args)


def _route_kernel(h2_ref, wrt_ref, br_ref, pk_ref, gw_ref, cnt_ref, run_ref, *, tm):
    i = pl.program_id(0)

    @pl.when(i == 0)
    def _():
        run_ref[...] = jnp.zeros_like(run_ref)

    x = _load_rows(h2_ref, tm).astype(BF16)
    logits = lax.dot_general(wrt_ref[...], x, (((1,), (1,)), ((), ())), preferred_element_type=F32)
    aff = jax.nn.sigmoid(logits)
    sel = aff + br_ref[...]
    row = lax.broadcasted_iota(jnp.int32, (N_EXPERTS, tm), 0)
    picks = []
    onehot = jnp.zeros((N_EXPERTS, tm), F32)
    for _ in range(TOP_K):
        m = jnp.max(sel, axis=0, keepdims=True)
        ik = jnp.min(jnp.where(sel == m, row, N_EXPERTS), axis=0, keepdims=True)
        oh = row == ik
        sel = jnp.where(oh, -jnp.inf, sel)
        onehot = jnp.where(oh, 1.0, onehot)
        picks.append((ik, oh))
    ri = lax.broadcasted_iota(jnp.int32, (tm, tm), 0)
    ci = lax.broadcasted_iota(jnp.int32, (tm, tm), 1)
    earlier = (ri < ci).astype(BF16)
    before = jnp.dot(onehot.astype(BF16), earlier, preferred_element_type=F32) + run_ref[...]
    affs = [jnp.sum(jnp.where(oh, aff, 0.0), axis=0, keepdims=True) for _, oh in picks]
    tot = affs[0]
    for a in affs[1:]:
        tot = tot + a
    for k, (ik, oh) in enumerate(picks):
        rank = jnp.sum(jnp.where(oh, before, 0.0), axis=0, keepdims=True).astype(jnp.int32)
        pk_ref[k:k + 1, :] = ik * (1 << RANK_BITS) + rank
        gw_ref[k:k + 1, :] = affs[k] / tot * ROUTED_SCALE
    run = run_ref[...] + jnp.sum(onehot, axis=1, keepdims=True)
    run_ref[...] = run
    cnt_ref[...] = run.astype(jnp.int32)


def _route(h2, wrt_bf, b_router, *, tm):
    t, d = h2.shape[0] // NCH, D_MODEL
    return pl.pallas_call(
        functools.partial(_route_kernel, tm=tm),
        grid=(t // tm,),
        in_specs=[pl.BlockSpec((tm * NCH, LANES), lambda i: (i, 0)),
                  pl.BlockSpec((N_EXPERTS, d), lambda i: (0, 0)),
                  pl.BlockSpec((N_EXPERTS, 1), lambda i: (0, 0))],
        out_specs=(pl.BlockSpec((TOP_K, tm), lambda i: (0, i)),
                   pl.BlockSpec((TOP_K, tm), lambda i: (0, i)),
                   pl.BlockSpec((N_EXPERTS, 1), lambda i: (0, 0))),
        out_shape=(jax.ShapeDtypeStruct((TOP_K, t), jnp.int32),
                   jax.ShapeDtypeStruct((TOP_K, t), F32),
                   jax.ShapeDtypeStruct((N_EXPERTS, 1), jnp.int32)),
        scratch_shapes=[pltpu.VMEM((N_EXPERTS, 1), F32)],
        compiler_params=_cparams(("arbitrary",)),
        name="route",
    )(h2, wrt_bf, b_router.reshape(N_EXPERTS, 1))


def _idx_copy(idx_hbm, idx_smem, sem, step, n, half):
    return pltpu.make_async_copy(idx_hbm.at[pl.ds(pl.multiple_of(step * n, n), n)],
                                 idx_smem.at[pl.ds(half * n, n)], sem.at[half])


def _slots_kernel(pk_ref, offs_ref, dst_ref):
    half = N_EXPERTS // 2
    shape = (TOP_K, LANES)
    lo_tab = jnp.broadcast_to(offs_ref[:, 0:half], shape)
    hi_tab = jnp.broadcast_to(offs_ref[:, half:N_EXPERTS], shape)
    for j in range(pk_ref.shape[1] // LANES):
        lanes = slice(j * LANES, (j + 1) * LANES)
        p = pk_ref[:, lanes]
        e = p >> RANK_BITS
        start = jnp.where(e >= half, jnp.take_along_axis(hi_tab, e & (half - 1), axis=1),
                          jnp.take_along_axis(lo_tab, e & (half - 1), axis=1))
        dst_ref[:, lanes] = ((p & ((1 << RANK_BITS) - 1)) + start) * NCH


def _slots(pk, offs, *, tm):
    t = pk.shape[1]
    return pl.pallas_call(
        _slots_kernel,
        grid=(t // tm,),
        in_specs=[pl.BlockSpec((TOP_K, tm), lambda i: (0, i)),
                  pl.BlockSpec((1, N_EXPERTS), lambda i: (0, 0))],
        out_specs=pl.BlockSpec((TOP_K, tm), lambda i: (0, i)),
        out_shape=jax.ShapeDtypeStruct((TOP_K, t), jnp.int32),
        compiler_params=_cparams(("arbitrary",)),
        name="slots",
    )(pk, offs[:N_EXPERTS].reshape(1, N_EXPERTS))


def _dispatch_kernel(h2_ref, dst_hbm, xs_hbm, idx_smem, idx_sem, row_sem, *, tm):
    i = pl.program_id(0)
    n = tm * TOP_K

    @pl.when(i == 0)
    def _():
        _idx_copy(dst_hbm, idx_smem, idx_sem, i, n, 0).start()

    def scatter(half):
        _idx_copy(dst_hbm, idx_smem, idx_sem, i, n, half).wait()

        @pl.when(i + 1 < pl.num_programs(0))
        def _():
            _idx_copy(dst_hbm, idx_smem, idx_sem, i + 1, n, 1 - half).start()

        def body(t8, carry):
            r0 = pl.multiple_of(t8 * (8 * NCH), 8 * NCH)
            b = half * n + t8 * (8 * TOP_K)
            for tt in range(8):
                src = h2_ref.at[pl.ds(r0 + tt * NCH, NCH)]
                for k in range(TOP_K):
                    dst = pl.multiple_of(idx_smem[b + tt * TOP_K + k], NCH)
                    pltpu.make_async_copy(src, xs_hbm.at[pl.ds(dst, NCH)], row_sem).start(priority=k % 2)
            return carry

        lax.fori_loop(0, tm // 8, body, 0)

    for half in range(2):
        pl.when(i % 2 == half)(functools.partial(scatter, half))
    pltpu.make_async_copy(xs_hbm.at[pl.ds(0, n * NCH)], xs_hbm.at[pl.ds(0, n * NCH)], row_sem).wait()


def _dispatch(h2, dst_flat, *, tm):
    t = h2.shape[0] // NCH
    n = tm * TOP_K
    return pl.pallas_call(
        functools.partial(_dispatch_kernel, tm=tm),
        grid=(t // tm,),
        in_specs=[pl.BlockSpec((tm * NCH, LANES), lambda i: (i, 0)),
                  pl.BlockSpec(memory_space=pl.ANY)],
        out_specs=pl.BlockSpec(memory_space=pl.ANY),
        scratch_shapes=[pltpu.SMEM((2 * n,), jnp.int32),
                        pltpu.SemaphoreType.DMA((2,)),
                        pltpu.SemaphoreType.DMA(())],
        out_shape=jax.ShapeDtypeStruct((t * TOP_K * NCH, LANES), F32),
        compiler_params=_cparams(("arbitrary",)),
        name="dispatch",
    )(h2, dst_flat)


def _experts_kernel(vt_ref, vg_ref, offs_ref, nv_ref, xs_ref, wg_ref, wu_ref, wd_ref, ys_ref,
                    wgb_ref, wub_ref, wdb_ref, *, tm):
    v = pl.program_id(0)

    @pl.when(v < nv_ref[0])
    def _():
        g = vg_ref[v]
        tile = vt_ref[v]
        prev = jnp.maximum(v - 1, 0)
        first = jnp.logical_or(v == 0, vt_ref[prev] != tile)

        @pl.when(jnp.logical_or(v == 0, vg_ref[prev] != g))
        def _():
            wgb_ref[...] = wg_ref[0].astype(BF16)
            wub_ref[...] = wu_ref[0].astype(BF16)
            wdb_ref[...] = wd_ref[0].astype(BF16)

        def swiglu_rows(r0, nr, masked):
            x = jnp.concatenate([xs_ref[pl.ds(r0 * NCH + c, nr, stride=NCH), :] for c in range(NCH)],
                                axis=1).astype(BF16)
            hg = jnp.dot(x, wgb_ref[...], preferred_element_type=F32)
            hu = jnp.dot(x, wub_ref[...], preferred_element_type=F32)
            hm = (hg * jax.nn.sigmoid(hg) * hu).astype(BF16)
            if masked:
                rowid = tile * tm + r0 + lax.broadcasted_iota(jnp.int32, (nr, 1), 0)
                mine = jnp.logical_and(rowid >= offs_ref[g], rowid < offs_ref[g + 1])
            for c2 in range(NCH // 2):
                y2 = jnp.dot(hm, wdb_ref[:, c2 * 2 * LANES:(c2 + 1) * 2 * LANES], preferred_element_type=F32)
                for j in range(2):
                    dst = pl.ds(r0 * NCH + 2 * c2 + j, nr, stride=NCH)
                    yc = y2[:, j * LANES:(j + 1) * LANES]
                    ys_ref[dst, :] = jnp.where(mine, yc, ys_ref[dst, :]) if masked else yc

        def visit(masked):
            for r0 in range(0, tm, tm // 2):
                swiglu_rows(r0, tm // 2, masked)

        pl.when(first)(functools.partial(visit, False))
        pl.when(jnp.logical_not(first))(functools.partial(visit, True))


def _experts(vt, vg, offs, nv, xs, w_gate, w_up, w_down, *, tm):
    m, d = xs.shape[0] // NCH, D_MODEL
    nvis = vt.shape[0]
    de = w_gate.shape[2]
    xmap = lambda v, vt, vg, offs, nv: (vt[v], 0)
    wmap = lambda v, vt, vg, offs, nv: (vg[v], 0, 0)
    grid_spec = pltpu.PrefetchScalarGridSpec(
        num_scalar_prefetch=4,
        grid=(nvis,),
        in_specs=[pl.BlockSpec((tm * NCH, LANES), xmap),
                  pl.BlockSpec((1, d, de), wmap),
                  pl.BlockSpec((1, d, de), wmap),
                  pl.BlockSpec((1, de, d), wmap)],
        out_specs=pl.BlockSpec((tm * NCH, LANES), xmap),
        scratch_shapes=[pltpu.VMEM((d, de), BF16), pltpu.VMEM((d, de), BF16), pltpu.VMEM((de, d), BF16)],
    )
    return pl.pallas_call(
        functools.partial(_experts_kernel, tm=tm),
        grid_spec=grid_spec,
        out_shape=jax.ShapeDtypeStruct((m * NCH, LANES), F32),
        compiler_params=_cparams(("arbitrary",)),
        name="experts",
    )(vt, vg, offs, nv, xs, w_gate, w_up, w_down)


def _combine_kernel(x1_ref, h2_ref, gw_ref, gt2_ref, wsg_ref, wsu_ref, wsd_ref, dst_hbm, ys_hbm,
                    y_ref, idx_smem, buf_ref, acc_ref, idx_sem, row_sem, *, tm, step0, per_token_gate):
    i = pl.program_id(0)
    nsteps = pl.num_programs(0)
    n = tm * TOP_K

    def issue(b, r0, half):
        for tt in range(8):
            for k in range(TOP_K):
                src = pl.multiple_of(idx_smem[b + tt * TOP_K + k], NCH)
                pltpu.make_async_copy(ys_hbm.at[pl.ds(src, NCH)],
                                      buf_ref.at[half, k, pl.ds(r0 + tt * NCH, NCH)],
                                      row_sem.at[half]).start(priority=k % 2)

    def finish8(t8, cur):
        t0 = pl.multiple_of(t8 * 8, 8)
        gw = gw_ref[pl.ds(t0, 8), :]
        for c in range(NCH):
            lanes = slice(c * LANES, (c + 1) * LANES)
            ffn = acc_ref[pl.ds(t0, 8), lanes]
            for k in range(TOP_K):
                ffn = ffn + gw[:, k:k + 1] * buf_ref[cur, k, pl.ds(t0 * NCH + c, 8, stride=NCH), :]
            gate = gt2_ref[pl.ds(t0, 8), lanes] if per_token_gate else gt2_ref[0][:, lanes]
            y_ref[pl.ds(t0, 8), lanes] = x1_ref[pl.ds(t0, 8), lanes] + gate * ffn

    @pl.when(i == 0)
    def _():
        _idx_copy(dst_hbm, idx_smem, idx_sem, step0, n, 0).start()
        _idx_copy(dst_hbm, idx_smem, idx_sem, step0, n, 0).wait()

        def body(t8, carry):
            issue(t8 * (8 * TOP_K), pl.multiple_of(t8 * (8 * NCH), 8 * NCH), 0)
            return carry

        lax.fori_loop(0, tm // 8, body, 0)

        @pl.when(nsteps > 1)
        def _():
            _idx_copy(dst_hbm, idx_smem, idx_sem, step0 + 1, n, 1).start()

    h2 = _load_rows(h2_ref, tm).astype(BF16)
    sg = jnp.dot(h2, wsg_ref[...], preferred_element_type=F32)
    su = jnp.dot(h2, wsu_ref[...], preferred_element_type=F32)
    acc_ref[...] = jnp.dot((sg * jax.nn.sigmoid(sg) * su).astype(BF16), wsd_ref[...], preferred_element_type=F32)

    def step(cur, has_next):
        pltpu.make_async_copy(ys_hbm.at[pl.ds(0, n * NCH)], ys_hbm.at[pl.ds(0, n * NCH)], row_sem.at[cur]).wait()
        if has_next:
            _idx_copy(dst_hbm, idx_smem, idx_sem, step0 + i + 1, n, 1 - cur).wait()

        def body(t8, carry):
            if has_next:
                issue((1 - cur) * n + t8 * (8 * TOP_K), pl.multiple_of(t8 * (8 * NCH), 8 * NCH), 1 - cur)
            finish8(t8, cur)
            return carry

        lax.fori_loop(0, tm // 8, body, 0)

        if has_next:
            @pl.when(i + 2 < nsteps)
            def _():
                _idx_copy(dst_hbm, idx_smem, idx_sem, step0 + i + 2, n, cur).start()

    for cur in range(2):
        pl.when(jnp.logical_and(i % 2 == cur, i + 1 < nsteps))(functools.partial(step, cur, True))
        pl.when(jnp.logical_and(i % 2 == cur, i + 1 == nsteps))(functools.partial(step, cur, False))


def _combine(x1, h2, gw, gt2, wsg, wsu, wsd, dst_flat, ys, *, tm, batch, seq, row_offset):
    d = x1.shape[1]
    n = tm * TOP_K
    per_token_gate = gt2.ndim == 2
    assert per_token_gate or seq % tm == 0
    per_b = max(seq // tm, 1)
    step0 = row_offset // tm
    tok = lambda i: (step0 + i, 0)
    c2 = lambda i: (0, 0)
    gate_spec = (pl.BlockSpec((tm, d), lambda i: (i, 0)) if per_token_gate
                 else pl.BlockSpec((1, 1, d), lambda i: (i // per_b, 0, 0)))
    return pl.pallas_call(
        functools.partial(_combine_kernel, tm=tm, step0=step0, per_token_gate=per_token_gate),
        grid=(batch * seq // tm,),
        in_specs=[pl.BlockSpec((tm, d), tok),
                  pl.BlockSpec((tm * NCH, LANES), tok),
                  pl.BlockSpec((tm, LANES), tok),
                  gate_spec,
                  pl.BlockSpec(wsg.shape, c2),
                  pl.BlockSpec(wsu.shape, c2),
                  pl.BlockSpec(wsd.shape, c2),
                  pl.BlockSpec(memory_space=pl.ANY),
                  pl.BlockSpec(memory_space=pl.ANY)],
        out_specs=pl.BlockSpec((tm, d), lambda i: (i, 0)),
        scratch_shapes=[pltpu.SMEM((2 * n,), jnp.int32),
                        pltpu.VMEM((2, TOP_K, tm * NCH, LANES), F32),
                        pltpu.VMEM((tm, d), F32),
                        pltpu.SemaphoreType.DMA((2,)),
                        pltpu.SemaphoreType.DMA((2,))],
        out_shape=jax.ShapeDtypeStruct((batch * seq, d), F32),
        compiler_params=_cparams(("arbitrary",)),
        name="combine",
    )(x1, h2, gw, gt2, wsg, wsu, wsd, dst_flat, ys)


def _bias_table(rel_bias, sb, kw):
    h, n_rel = rel_bias.shape
    lo = (kw - 1) - ATT_PAST_WINDOW - (CHUNK - 1)
    ntot = sb + kw - 1
    s = jnp.concatenate([jnp.broadcast_to(rel_bias[:, :1], (h, lo)), rel_bias,
                         jnp.broadcast_to(rel_bias[:, -1:], (h, ntot - lo - n_rel))], axis=1).astype(F32)
    s_rev = jnp.pad(s[:, ::-1], ((0, 0), (0, 1)))
    skew = jnp.broadcast_to(s_rev[:, None, :], (h, sb, ntot + 1)).reshape(h, -1)[:, :sb * ntot]
    toep = skew.reshape(h, sb, ntot)[:, :, sb - 1:sb - 1 + kw]
    r = np.arange(sb)[:, None]
    kk = np.arange(kw)[None, :]
    qc, kc = r // CHUNK, kk // CHUNK
    band = (kc >= qc) & (kc <= qc + LEFT_CHUNKS)
    return jnp.where(band[None], toep, NEG_INF)


def _visit_plan(offs, m, tm, nvis):
    cnt = offs[1:] - offs[:-1]
    first = offs[:-1] // tm
    last = jnp.maximum(offs[1:] - 1, 0) // tm
    per = jnp.where(cnt > 0, last - first + 1, 0)
    vend = jnp.cumsum(per)
    vstart = vend - per
    total = vend[-1]
    v = jnp.arange(nvis, dtype=jnp.int32)
    g = jnp.minimum(jnp.sum(vend[None, :] <= v[:, None], axis=1), N_EXPERTS - 1).astype(jnp.int32)
    tile = (first[g] + v - vstart[g]).astype(jnp.int32)
    live = v < total
    last_tile = m // tm - 1
    tile = jnp.where(live, tile, last_tile)
    g = jnp.where(live, g, N_EXPERTS - 1)
    return tile, g, total.reshape(1).astype(jnp.int32)


def kernel(x_prompt, x_sample, c_prompt, c_sample, cache_k, cache_v, w_ada, b_ada, g_mix, w_in, g_gmlp_v,
           w_spatial, b_spatial, g_q, g_k, rel_bias, g_out_a, g_out_b, w_out, g_ffn, w_router, b_router,
           w_exp_gate, w_exp_up, w_exp_down, w_sh_gate, w_sh_up, w_sh_down):
    depth = w_ada.shape[0]
    assert depth == 1, "single layer"
    bp, sp, d = x_prompt.shape
    bs, ss, _ = x_sample.shape
    tp, ts = bp * sp, bs * ss
    t_all = tp + ts
    l = 0

    c_all = jnp.concatenate([c_prompt, c_sample], axis=0)
    nb = c_all.shape[0]
    nb_pad = -(-nb // 8) * 8
    c_all = jnp.pad(c_all, ((0, nb_pad - nb), (0, 0)))
    mod = _ada(c_all, w_ada[l], b_ada[l])[:nb].reshape(nb, 6, 1, d)
    sh1, sc1, gt1, sh2, sc2, gt2 = [mod[:, i] for i in range(6)]

    w_in_bf = w_in[l].astype(BF16)
    w_out_bf = w_out[l].astype(BF16)
    gv = g_gmlp_v[l].reshape(1, W_A)
    gq = jnp.tile(g_q[l], B_HEADS).reshape(1, W_B)
    gk = jnp.tile(g_k[l], B_HEADS).reshape(1, W_B)
    hid = np.arange(W_B) // HEAD_DIM
    hsum = jnp.asarray(hid[:, None] == hid[None, :], BF16)
    gmix = g_mix[l].reshape(1, d)
    goa = g_out_a[l].reshape(1, W_A)
    gob = g_out_b[l].reshape(1, W_B)
    gffn = g_ffn[l].reshape(1, d)
    bias_p = _bias_table(rel_bias[l], GMLP_BLOCK, ATT_PAST_WINDOW + GMLP_BLOCK)
    bsp_p = jnp.broadcast_to(b_spatial[l][:, :, None], (A_GROUPS, GMLP_BLOCK, A_GROUP_DIM))

    cl = cache_k.shape[2]
    us, vns, qs, ks_bf, vs_bf, k32s, v32s = _in_proj(
        x_sample, sh1[bp:], sc1[bp:], gmix, w_in_bf, gv, gq, gk, hsum,
        tm=ss, pad_rows=0, keep=ss, vn_dtype=F32)
    kcat = jnp.concatenate([cache_k[l].reshape(bs, cl, W_B).astype(BF16), ks_bf], axis=1)
    vcat = jnp.concatenate([cache_v[l].reshape(bs, cl, W_B).astype(BF16), vs_bf], axis=1)
    bias_s = bias_p[:, :ss, ATT_PAST_WINDOW - cl:ATT_PAST_WINDOW + ss]
    x1_s, h2_s = _mixer(x_sample, us, vns, qs, kcat, vcat, bias_s, w_spatial[l][:, :ss, :ss], bsp_p[:, :ss],
                        goa, gob, w_out_bf, gt1[bp:], gffn, sh2[bp:], sc2[bp:],
                        sb=ss, nsb=1, padded=False)
    new_k_sample = k32s.reshape(1, bs, ss, B_HEADS, HEAD_DIM)
    new_v_sample = v32s.reshape(1, bs, ss, B_HEADS, HEAD_DIM)
    new_gmlp_v_sample = vns.reshape(1, bs, ss, A_GROUPS, A_GROUP_DIM)

    keep = min(ATT_PAST_WINDOW, sp)
    u, vn, q, kpad, vpad, k32, v32 = _in_proj(
        x_prompt, sh1[:bp], sc1[:bp], gmix, w_in_bf, gv, gq, gk, hsum,
        tm=512, pad_rows=ATT_PAST_WINDOW, keep=keep, vn_dtype=BF16)
    x1, h2 = _mixer(x_prompt, u, vn, q, kpad, vpad, bias_p, w_spatial[l], bsp_p, goa, gob, w_out_bf,
                    gt1[:bp], gffn, sh2[:bp], sc2[:bp],
                    sb=GMLP_BLOCK, nsb=2, padded=True, tail=(x1_s, h2_s))
    assert x1.shape[0] == t_all
    new_k_prompt = k32.reshape(1, bp, keep, B_HEADS, HEAD_DIM)
    new_v_prompt = v32.reshape(1, bp, keep, B_HEADS, HEAD_DIM)

    tmr = 256
    tmd = 512 if t_all % 512 == 0 else 256
    tme = 1024 if (t_all * TOP_K) % 1024 == 0 else 512
    assert t_all % tmr == 0 and (t_all * TOP_K) % tme == 0 and sp % 256 == 0 and ts % (2 * ss) == 0
    assert t_all * TOP_K < (1 << RANK_BITS)
    pk, gw_t, cnt = _route(h2, w_router[l].T.astype(BF16), b_router[l], tm=tmr)
    offs = jnp.concatenate([jnp.zeros((1,), jnp.int32), jnp.cumsum(cnt[:, 0]).astype(jnp.int32)])
    tms = next(c for c in (2048, 1536, 1024, 512, 256) if t_all % c == 0)
    dst_flat = _slots(pk, offs, tm=tms).T.reshape(-1)
    gw = jnp.pad(gw_t.T, ((0, 0), (0, LANES - TOP_K)))
    xs = _dispatch(h2, dst_flat, tm=tmd)
    m = t_all * TOP_K
    nvis = m // tme + N_EXPERTS - 1
    vt, vg, nv = _visit_plan(offs, m, tme, nvis)
    ys = _experts(vt, vg, offs, nv, xs, w_exp_gate[l], w_exp_up[l], w_exp_down[l], tm=tme)

    wsg, wsu, wsd = w_sh_gate[l].astype(BF16), w_sh_up[l].astype(BF16), w_sh_down[l].astype(BF16)
    y_prompt = _combine(x1, h2, gw, gt2[:bp], wsg, wsu, wsd, dst_flat, ys,
                        tm=512 if sp % 512 == 0 else 256, batch=bp, seq=sp, row_offset=0).reshape(bp, sp, d)
    gt2_tok = jnp.broadcast_to(gt2[bp:], (bs, ss, d)).reshape(ts, d)
    y_sample = _combine(x1, h2, gw, gt2_tok, wsg, wsu, wsd, dst_flat, ys,
                        tm=2 * ss, batch=bs, seq=ss, row_offset=tp).reshape(bs, ss, d)
    return (y_prompt, y_sample, new_k_prompt, new_v_prompt, new_k_sample, new_v_sample, new_gmlp_v_sample)
```

```python
import functools

import jax
import jax.numpy as jnp
import numpy as np
from jax import lax
from jax.experimental import pallas as pl
from jax.experimental.pallas import tpu as pltpu

F32 = jnp.float32
BF16 = jnp.bfloat16

D_MODEL = 1024
CHUNK = 64
GMLP_BLOCK = 128
W_A = 512
A_GROUPS = 4
A_GROUP_DIM = 128
W_B = 512
B_HEADS = 8
HEAD_DIM = 64
LEFT_CHUNKS = 8
ATT_PAST_WINDOW = LEFT_CHUNKS * CHUNK
REL_CLIP = 128
D_IN = 2 * W_A + 3 * W_B
N_EXPERTS = 256
TOP_K = 8
D_EXPERT = 256
ROUTED_SCALE = 2.5
EPS = 1e-6
NEG_INF = -1e30

RANK_BITS = 20
VMEM_LIMIT = 56 * 1024 * 1024


def _cparams(sem, vmem=VMEM_LIMIT):
    return pltpu.CompilerParams(dimension_semantics=sem, vmem_limit_bytes=vmem)


def _rms(x, axis=-1):
    return lax.rsqrt(jnp.mean(x * x, axis=axis, keepdims=True) + EPS)


LANES = 128
NCH = D_MODEL // LANES


def _load_rows(ref, n, lead=()):
    return jnp.concatenate([ref[(*lead, pl.ds(c, n, stride=NCH), slice(None))] for c in range(NCH)], axis=1)


def _store_rows(ref, val):
    n = val.shape[0]
    for c in range(NCH):
        ref[pl.ds(c, n, stride=NCH), :] = val[:, c * LANES:(c + 1) * LANES]


def _ada_kernel(c_ref, w_ref, b_ref, o_ref):
    c = c_ref[...]
    s = c * jax.nn.sigmoid(c)
    o_ref[...] = jnp.dot(s.astype(BF16), w_ref[...].astype(BF16), preferred_element_type=F32) + b_ref[...]


def _ada(c, w_ada, b_ada):
    n, d = c.shape
    nout = w_ada.shape[1]
    tn = 1024
    return pl.pallas_call(
        _ada_kernel,
        grid=(nout // tn,),
        in_specs=[pl.BlockSpec((n, d), lambda j: (0, 0)),
                  pl.BlockSpec((d, tn), lambda j: (0, j)),
                  pl.BlockSpec((1, tn), lambda j: (0, j))],
        out_specs=pl.BlockSpec((n, tn), lambda j: (0, j)),
        out_shape=jax.ShapeDtypeStruct((n, nout), F32),
        compiler_params=_cparams(("arbitrary",)),
        name="ada",
    )(c, w_ada, b_ada.reshape(1, nout))


def _in_proj_kernel(x_ref, sh_ref, sc_ref, gmix_ref, win_ref, gv_ref, gq_ref, gk_ref, hsum_ref,
                    u_ref, vn_ref, q_ref, kp_ref, vp_ref, k32_ref, v32_ref, *, npad, first_kept):
    i = pl.program_id(1)

    @pl.when(i < npad)
    def _():
        kp_ref[...] = jnp.zeros_like(kp_ref)
        vp_ref[...] = jnp.zeros_like(vp_ref)

    @pl.when(i >= npad)
    def _():
        x = x_ref[0]
        h = x * _rms(x) * gmix_ref[...]
        h = h * (1.0 + sc_ref[0]) + sh_ref[0]
        proj = jnp.dot(h.astype(BF16), win_ref[...], preferred_element_type=F32)
        u = jax.nn.gelu(proj[:, 0:W_A])
        u_ref[0] = u.astype(u_ref.dtype)
        v = jax.nn.gelu(proj[:, W_A:2 * W_A])
        for g in range(A_GROUPS):
            vg = v[:, g * A_GROUP_DIM:(g + 1) * A_GROUP_DIM]
            vn = vg * _rms(vg) * gv_ref[:, g * A_GROUP_DIM:(g + 1) * A_GROUP_DIM]
            vn_ref[0, :, g * A_GROUP_DIM:(g + 1) * A_GROUP_DIM] = vn.astype(vn_ref.dtype)
        q = proj[:, 2 * W_A:2 * W_A + W_B]
        k = proj[:, 2 * W_A + W_B:2 * W_A + 2 * W_B]
        va = proj[:, 2 * W_A + 2 * W_B:]
        q2 = jnp.dot((q * q).astype(BF16), hsum_ref[...], preferred_element_type=F32) * (1.0 / HEAD_DIM)
        k2 = jnp.dot((k * k).astype(BF16), hsum_ref[...], preferred_element_type=F32) * (1.0 / HEAD_DIM)
        qn = q * lax.rsqrt(q2 + EPS) * gq_ref[...]
        kn = k * lax.rsqrt(k2 + EPS) * gk_ref[...]
        q_ref[0] = (qn * (HEAD_DIM ** -0.5)).astype(q_ref.dtype)
        kp_ref[0] = kn.astype(kp_ref.dtype)
        vp_ref[0] = va.astype(vp_ref.dtype)

        @pl.when(i >= first_kept)
        def _():
            k32_ref[0] = kn
            v32_ref[0] = va


def _in_proj(x, sh1, sc1, g_mix, w_in_bf, gv, gq, gk, hsum, *, tm, pad_rows, keep, vn_dtype):
    b, l, d = x.shape
    assert l % tm == 0 and pad_rows % tm == 0 and keep % tm == 0
    npad = pad_rows // tm
    nt = l // tm
    first_kept = npad + nt - keep // tm
    row = lambda bi, i: (bi, jnp.maximum(i - npad, 0), 0)
    kept = lambda bi, i: (bi, jnp.maximum(i - first_kept, 0), 0)
    bvec = lambda bi, i: (bi, 0, 0)
    const2 = lambda bi, i: (0, 0)
    out_shapes = (
        jax.ShapeDtypeStruct((b, l, W_A), BF16),
        jax.ShapeDtypeStruct((b, l, W_A), vn_dtype),
        jax.ShapeDtypeStruct((b, l, W_B), BF16),
        jax.ShapeDtypeStruct((b, pad_rows + l, W_B), BF16),
        jax.ShapeDtypeStruct((b, pad_rows + l, W_B), BF16),
        jax.ShapeDtypeStruct((b, keep, W_B), F32),
        jax.ShapeDtypeStruct((b, keep, W_B), F32),
    )
    padrow = lambda bi, i: (bi, i, 0)
    return pl.pallas_call(
        functools.partial(_in_proj_kernel, npad=npad, first_kept=first_kept),
        grid=(b, nt + npad),
        in_specs=[pl.BlockSpec((1, tm, d), row),
                  pl.BlockSpec((1, 1, d), bvec),
                  pl.BlockSpec((1, 1, d), bvec),
                  pl.BlockSpec((1, d), const2),
                  pl.BlockSpec((d, D_IN), const2),
                  pl.BlockSpec((1, W_A), const2),
                  pl.BlockSpec((1, W_B), const2),
                  pl.BlockSpec((1, W_B), const2),
                  pl.BlockSpec((W_B, W_B), const2)],
        out_specs=(pl.BlockSpec((1, tm, W_A), row),
                   pl.BlockSpec((1, tm, W_A), row),
                   pl.BlockSpec((1, tm, W_B), row),
                   pl.BlockSpec((1, tm, W_B), padrow),
                   pl.BlockSpec((1, tm, W_B), padrow),
                   pl.BlockSpec((1, tm, W_B), kept),
                   pl.BlockSpec((1, tm, W_B), kept)),
        out_shape=out_shapes,
        compiler_params=_cparams(("arbitrary", "arbitrary")),
        name="in_proj",
    )(x, sh1, sc1, g_mix, w_in_bf, gv, gq, gk, hsum)


def _mixer_kernel(x_ref, u_ref, vn_ref, q_ref, kp_ref, vp_ref, bias_ref, wsp_ref, bsp_ref,
                  goa_ref, gob_ref, wout_ref, gt1_ref, gffn_ref, sh2_ref, sc2_ref, *rest,
                  sb, nsb, kw, padded, nj, nmain):
    x1_ref, h2_ref, merged_ref = rest[-3:]
    i = pl.program_id(0)
    if len(rest) == 5:
        tx1_ref, th2_ref = rest[:2]

        @pl.when(i >= nmain)
        def _():
            x1_ref[...] = tx1_ref[...]
            h2_ref[...] = th2_ref[...]

    @pl.when(i < nmain)
    def _():
        _mixer_body(x_ref, u_ref, vn_ref, q_ref, kp_ref, vp_ref, bias_ref, wsp_ref, bsp_ref,
                    goa_ref, gob_ref, wout_ref, gt1_ref, gffn_ref, sh2_ref, sc2_ref,
                    x1_ref, h2_ref, merged_ref, i % nj, sb=sb, nsb=nsb, kw=kw, padded=padded)


def _mixer_body(x_ref, u_ref, vn_ref, q_ref, kp_ref, vp_ref, bias_ref, wsp_ref, bsp_ref,
                goa_ref, gob_ref, wout_ref, gt1_ref, gffn_ref, sh2_ref, sc2_ref,
                x1_ref, h2_ref, merged_ref, j, *, sb, nsb, kw, padded):
    rows = sb * nsb
    ri = lax.broadcasted_iota(jnp.int32, (sb, sb), 0)
    ci = lax.broadcasted_iota(jnp.int32, (sb, sb), 1)
    gmask = (ci // CHUNK) <= (ri // CHUNK)
    lane = lax.broadcasted_iota(jnp.int32, (sb, 2 * HEAD_DIM), 1)
    lo_half = lane < HEAD_DIM
    kidx = lax.broadcasted_iota(jnp.int32, (sb, kw), 1)

    for s in range(nsb):
        r0 = s * sb
        vn = vn_ref[0, r0:r0 + sb, :].astype(BF16)
        u = u_ref[0, r0:r0 + sb, :].astype(F32)
        zs = []
        for g in range(A_GROUPS):
            w = jnp.where(gmask, wsp_ref[g], 0.0).astype(BF16)
            z = jnp.dot(w, vn[:, g * A_GROUP_DIM:(g + 1) * A_GROUP_DIM], preferred_element_type=F32)
            zs.append(z + bsp_ref[g])
        a = u * jnp.concatenate(zs, axis=1)
        a = a * _rms(a) * goa_ref[...]
        merged_ref[r0:r0 + sb, 0:W_A] = a.astype(BF16)

        if padded:
            blk0 = j * rows + r0
            kstart = pl.multiple_of(blk0, sb)
            kb = kp_ref[0, pl.ds(kstart, kw), :]
            vb = vp_ref[0, pl.ds(kstart, kw), :]
            valid = kidx >= (ATT_PAST_WINDOW - blk0)
        else:
            kb = kp_ref[0]
            vb = vp_ref[0]
            valid = None
        qb = q_ref[0, r0:r0 + sb, :]
        scs = []
        for h in range(B_HEADS):
            c0 = (h // 2) * 2 * HEAD_DIM
            qp = qb[:, c0:c0 + 2 * HEAD_DIM]
            qm = jnp.where(lo_half if h % 2 == 0 else jnp.logical_not(lo_half), qp, jnp.zeros_like(qp))
            sc = lax.dot_general(qm, kb[:, c0:c0 + 2 * HEAD_DIM], (((1,), (1,)), ((), ())),
                                 preferred_element_type=F32)
            scs.append(sc + bias_ref[h])
        ps, ls = [], []
        for sc in scs:
            if valid is not None:
                sc = jnp.where(valid, sc, NEG_INF)
            p = jnp.exp(sc - jnp.max(sc, axis=-1, keepdims=True))
            ls.append(jnp.sum(p, axis=-1, keepdims=True))
            ps.append(p.astype(BF16))
        os_ = []
        for h in range(B_HEADS):
            c0 = (h // 2) * 2 * HEAD_DIM
            os_.append(jnp.dot(ps[h], vb[:, c0:c0 + 2 * HEAD_DIM], preferred_element_type=F32) / ls[h])
        outs = [jnp.where(lo_half, os_[2 * hp], os_[2 * hp + 1]) for hp in range(B_HEADS // 2)]
        bo = jnp.concatenate(outs, axis=1)
        bo = bo * _rms(bo) * gob_ref[...]
        merged_ref[r0:r0 + sb, W_A:W_A + W_B] = bo.astype(BF16)

    mix = jnp.dot(merged_ref[...], wout_ref[...], preferred_element_type=F32)
    x1 = x_ref[0] + gt1_ref[0] * mix
    x1_ref[...] = x1
    h2 = x1 * _rms(x1) * gffn_ref[...]
    _store_rows(h2_ref, h2 * (1.0 + sc2_ref[0]) + sh2_ref[0])


def _mixer(x, u, vn, q, kp, vp, bias, wsp, bsp, goa, gob, wout_bf, gt1, gffn, sh2, sc2,
           *, sb, nsb, padded, tail=None):
    b, l, d = x.shape
    rows = sb * nsb
    assert l % rows == 0
    nj = l // rows
    nmain = b * nj
    ntail = 0
    if tail is not None:
        assert tail[0].shape[0] % rows == 0
        ntail = tail[0].shape[0] // rows
    total_rows = (nmain + ntail) * rows
    kw = bias.shape[2]
    kl = kp.shape[1]
    bidx = lambda i: jnp.minimum(i // nj, b - 1)
    row = lambda i: (bidx(i), i % nj, 0)
    bvec = lambda i: (bidx(i), 0, 0)
    c2 = lambda i: (0, 0)
    c3 = lambda i: (0, 0, 0)
    pool = lambda i: (i, 0)
    in_specs = [pl.BlockSpec((1, rows, d), row),
                pl.BlockSpec((1, rows, W_A), row),
                pl.BlockSpec((1, rows, W_A), row),
                pl.BlockSpec((1, rows, W_B), row),
                pl.BlockSpec((1, kl, W_B), bvec),
                pl.BlockSpec((1, kl, W_B), bvec),
                pl.BlockSpec(bias.shape, c3),
                pl.BlockSpec(wsp.shape, c3),
                pl.BlockSpec(bsp.shape, c3),
                pl.BlockSpec((1, W_A), c2),
                pl.BlockSpec((1, W_B), c2),
                pl.BlockSpec((d, d), c2),
                pl.BlockSpec((1, 1, d), bvec),
                pl.BlockSpec((1, d), c2),
                pl.BlockSpec((1, 1, d), bvec),
                pl.BlockSpec((1, 1, d), bvec)]
    args = [x, u, vn, q, kp, vp, bias, wsp, bsp, goa, gob, wout_bf, gt1, gffn, sh2, sc2]
    if tail is not None:
        tmap = lambda i: (jnp.maximum(i - nmain, 0), 0)
        in_specs += [pl.BlockSpec((rows, d), tmap), pl.BlockSpec((rows * NCH, LANES), tmap)]
        args += list(tail)
    return pl.pallas_call(
        functools.partial(_mixer_kernel, sb=sb, nsb=nsb, kw=kw, padded=padded, nj=nj, nmain=nmain),
        grid=(nmain + ntail,),
        in_specs=in_specs,
        out_specs=(pl.BlockSpec((rows, d), pool), pl.BlockSpec((rows * NCH, LANES), pool)),
        out_shape=(jax.ShapeDtypeStruct((total_rows, d), F32),
                   jax.ShapeDtypeStruct((total_rows * NCH, LANES), F32)),
        scratch_shapes=[pltpu.VMEM((rows, d), BF16)],
        compiler_params=_cparams(("arbitrary",)),
        name="mixer",
    )(*args)


def _route_kernel(h2_ref, wrt_ref, br_ref, pk_ref, gw_ref, cnt_ref, run_ref, *, tm):
    i = pl.program_id(0)

    @pl.when(i == 0)
    def _():
        run_ref[...] = jnp.zeros_like(run_ref)

    x = _load_rows(h2_ref, tm).astype(BF16)
    logits = lax.dot_general(wrt_ref[...], x, (((1,), (1,)), ((), ())), preferred_element_type=F32)
    aff = jax.nn.sigmoid(logits)
    sel = aff + br_ref[...]
    row = lax.broadcasted_iota(jnp.int32, (N_EXPERTS, tm), 0)
    picks = []
    onehot = jnp.zeros((N_EXPERTS, tm), F32)
    for _ in range(TOP_K):
        m = jnp.max(sel, axis=0, keepdims=True)
        ik = jnp.min(jnp.where(sel == m, row, N_EXPERTS), axis=0, keepdims=True)
        oh = row == ik
        sel = jnp.where(oh, -jnp.inf, sel)
        onehot = jnp.where(oh, 1.0, onehot)
        picks.append((ik, oh))
    ri = lax.broadcasted_iota(jnp.int32, (tm, tm), 0)
    ci = lax.broadcasted_iota(jnp.int32, (tm, tm), 1)
    earlier = (ri < ci).astype(BF16)
    before = jnp.dot(onehot.astype(BF16), earlier, preferred_element_type=F32) + run_ref[...]
    affs = [jnp.sum(jnp.where(oh, aff, 0.0), axis=0, keepdims=True) for _, oh in picks]
    tot = affs[0]
    for a in affs[1:]:
        tot = tot + a
    for k, (ik, oh) in enumerate(picks):
        rank = jnp.sum(jnp.where(oh, before, 0.0), axis=0, keepdims=True).astype(jnp.int32)
        pk_ref[k:k + 1, :] = ik * (1 << RANK_BITS) + rank
        gw_ref[k:k + 1, :] = affs[k] / tot * ROUTED_SCALE
    run = run_ref[...] + jnp.sum(onehot, axis=1, keepdims=True)
    run_ref[...] = run
    cnt_ref[...] = run.astype(jnp.int32)


def _route(h2, wrt_bf, b_router, *, tm):
    t, d = h2.shape[0] // NCH, D_MODEL
    return pl.pallas_call(
        functools.partial(_route_kernel, tm=tm),
        grid=(t // tm,),
        in_specs=[pl.BlockSpec((tm * NCH, LANES), lambda i: (i, 0)),
                  pl.BlockSpec((N_EXPERTS, d), lambda i: (0, 0)),
                  pl.BlockSpec((N_EXPERTS, 1), lambda i: (0, 0))],
        out_specs=(pl.BlockSpec((TOP_K, tm), lambda i: (0, i)),
                   pl.BlockSpec((TOP_K, tm), lambda i: (0, i)),
                   pl.BlockSpec((N_EXPERTS, 1), lambda i: (0, 0))),
        out_shape=(jax.ShapeDtypeStruct((TOP_K, t), jnp.int32),
                   jax.ShapeDtypeStruct((TOP_K, t), F32),
                   jax.ShapeDtypeStruct((N_EXPERTS, 1), jnp.int32)),
        scratch_shapes=[pltpu.VMEM((N_EXPERTS, 1), F32)],
        compiler_params=_cparams(("arbitrary",)),
        name="route",
    )(h2, wrt_bf, b_router.reshape(N_EXPERTS, 1))


def _idx_copy(idx_hbm, idx_smem, sem, step, n, half):
    return pltpu.make_async_copy(idx_hbm.at[pl.ds(pl.multiple_of(step * n, n), n)],
                                 idx_smem.at[pl.ds(half * n, n)], sem.at[half])


def _slots_kernel(pk_ref, offs_ref, dst_ref):
    half = N_EXPERTS // 2
    shape = (TOP_K, LANES)
    lo_tab = jnp.broadcast_to(offs_ref[:, 0:half], shape)
    hi_tab = jnp.broadcast_to(offs_ref[:, half:N_EXPERTS], shape)
    for j in range(pk_ref.shape[1] // LANES):
        lanes = slice(j * LANES, (j + 1) * LANES)
        p = pk_ref[:, lanes]
        e = p >> RANK_BITS
        start = jnp.where(e >= half, jnp.take_along_axis(hi_tab, e & (half - 1), axis=1),
                          jnp.take_along_axis(lo_tab, e & (half - 1), axis=1))
        dst_ref[:, lanes] = ((p & ((1 << RANK_BITS) - 1)) + start) * NCH


def _slots(pk, offs, *, tm):
    t = pk.shape[1]
    return pl.pallas_call(
        _slots_kernel,
        grid=(t // tm,),
        in_specs=[pl.BlockSpec((TOP_K, tm), lambda i: (0, i)),
                  pl.BlockSpec((1, N_EXPERTS), lambda i: (0, 0))],
        out_specs=pl.BlockSpec((TOP_K, tm), lambda i: (0, i)),
        out_shape=jax.ShapeDtypeStruct((TOP_K, t), jnp.int32),
        compiler_params=_cparams(("arbitrary",)),
        name="slots",
    )(pk, offs[:N_EXPERTS].reshape(1, N_EXPERTS))


def _dispatch_kernel(h2_ref, dst_hbm, xs_hbm, idx_smem, idx_sem, row_sem, *, tm):
    i = pl.program_id(0)
    n = tm * TOP_K

    @pl.when(i == 0)
    def _():
        _idx_copy(dst_hbm, idx_smem, idx_sem, i, n, 0).start()

    def scatter(half):
        _idx_copy(dst_hbm, idx_smem, idx_sem, i, n, half).wait()

        @pl.when(i + 1 < pl.num_programs(0))
        def _():
            _idx_copy(dst_hbm, idx_smem, idx_sem, i + 1, n, 1 - half).start()

        def body(t8, carry):
            r0 = pl.multiple_of(t8 * (8 * NCH), 8 * NCH)
            b = half * n + t8 * (8 * TOP_K)
            for tt in range(8):
                src = h2_ref.at[pl.ds(r0 + tt * NCH, NCH)]
                for k in range(TOP_K):
                    dst = pl.multiple_of(idx_smem[b + tt * TOP_K + k], NCH)
                    pltpu.make_async_copy(src, xs_hbm.at[pl.ds(dst, NCH)], row_sem).start(priority=k % 2)
            return carry

        lax.fori_loop(0, tm // 8, body, 0)

    for half in range(2):
        pl.when(i % 2 == half)(functools.partial(scatter, half))
    pltpu.make_async_copy(xs_hbm.at[pl.ds(0, n * NCH)], xs_hbm.at[pl.ds(0, n * NCH)], row_sem).wait()


def _dispatch(h2, dst_flat, *, tm):
    t = h2.shape[0] // NCH
    n = tm * TOP_K
    return pl.pallas_call(
        functools.partial(_dispatch_kernel, tm=tm),
        grid=(t // tm,),
        in_specs=[pl.BlockSpec((tm * NCH, LANES), lambda i: (i, 0)),
                  pl.BlockSpec(memory_space=pl.ANY)],
        out_specs=pl.BlockSpec(memory_space=pl.ANY),
        scratch_shapes=[pltpu.SMEM((2 * n,), jnp.int32),
                        pltpu.SemaphoreType.DMA((2,)),
                        pltpu.SemaphoreType.DMA(())],
        out_shape=jax.ShapeDtypeStruct((t * TOP_K * NCH, LANES), F32),
        compiler_params=_cparams(("arbitrary",)),
        name="dispatch",
    )(h2, dst_flat)


def _experts_kernel(vt_ref, vg_ref, offs_ref, nv_ref, xs_ref, wg_ref, wu_ref, wd_ref, ys_ref,
                    wgb_ref, wub_ref, wdb_ref, *, tm):
    v = pl.program_id(0)

    @pl.when(v < nv_ref[0])
    def _():
        g = vg_ref[v]
        tile = vt_ref[v]
        prev = jnp.maximum(v - 1, 0)
        first = jnp.logical_or(v == 0, vt_ref[prev] != tile)

        @pl.when(jnp.logical_or(v == 0, vg_ref[prev] != g))
        def _():
            wgb_ref[...] = wg_ref[0].astype(BF16)
            wub_ref[...] = wu_ref[0].astype(BF16)
            wdb_ref[...] = wd_ref[0].astype(BF16)

        def swiglu_rows(r0, nr, masked):
            x = jnp.concatenate([xs_ref[pl.ds(r0 * NCH + c, nr, stride=NCH), :] for c in range(NCH)],
                                axis=1).astype(BF16)
            hg = jnp.dot(x, wgb_ref[...], preferred_element_type=F32)
            hu = jnp.dot(x, wub_ref[...], preferred_element_type=F32)
            hm = (hg * jax.nn.sigmoid(hg) * hu).astype(BF16)
            if masked:
                rowid = tile * tm + r0 + lax.broadcasted_iota(jnp.int32, (nr, 1), 0)
                mine = jnp.logical_and(rowid >= offs_ref[g], rowid < offs_ref[g + 1])
            for c2 in range(NCH // 2):
                y2 = jnp.dot(hm, wdb_ref[:, c2 * 2 * LANES:(c2 + 1) * 2 * LANES], preferred_element_type=F32)
                for j in range(2):
                    dst = pl.ds(r0 * NCH + 2 * c2 + j, nr, stride=NCH)
                    yc = y2[:, j * LANES:(j + 1) * LANES]
                    ys_ref[dst, :] = jnp.where(mine, yc, ys_ref[dst, :]) if masked else yc

        def visit(masked):
            for r0 in range(0, tm, tm // 2):
                swiglu_rows(r0, tm // 2, masked)

        pl.when(first)(functools.partial(visit, False))
        pl.when(jnp.logical_not(first))(functools.partial(visit, True))


def _experts(vt, vg, offs, nv, xs, w_gate, w_up, w_down, *, tm):
    m, d = xs.shape[0] // NCH, D_MODEL
    nvis = vt.shape[0]
    de = w_gate.shape[2]
    xmap = lambda v, vt, vg, offs, nv: (vt[v], 0)
    wmap = lambda v, vt, vg, offs, nv: (vg[v], 0, 0)
    grid_spec = pltpu.PrefetchScalarGridSpec(
        num_scalar_prefetch=4,
        grid=(nvis,),
        in_specs=[pl.BlockSpec((tm * NCH, LANES), xmap),
                  pl.BlockSpec((1, d, de), wmap),
                  pl.BlockSpec((1, d, de), wmap),
                  pl.BlockSpec((1, de, d), wmap)],
        out_specs=pl.BlockSpec((tm * NCH, LANES), xmap),
        scratch_shapes=[pltpu.VMEM((d, de), BF16), pltpu.VMEM((d, de), BF16), pltpu.VMEM((de, d), BF16)],
    )
    return pl.pallas_call(
        functools.partial(_experts_kernel, tm=tm),
        grid_spec=grid_spec,
        out_shape=jax.ShapeDtypeStruct((m * NCH, LANES), F32),
        compiler_params=_cparams(("arbitrary",)),
        name="experts",
    )(vt, vg, offs, nv, xs, w_gate, w_up, w_down)


def _combine_kernel(x1_ref, h2_ref, gw_ref, gt2_ref, wsg_ref, wsu_ref, wsd_ref, dst_hbm, ys_hbm,
                    y_ref, idx_smem, buf_ref, acc_ref, idx_sem, row_sem, *, tm, step0, per_token_gate):
    i = pl.program_id(0)
    nsteps = pl.num_programs(0)
    n = tm * TOP_K

    def issue(b, r0, half):
        for tt in range(8):
            for k in range(TOP_K):
                src = pl.multiple_of(idx_smem[b + tt * TOP_K + k], NCH)
                pltpu.make_async_copy(ys_hbm.at[pl.ds(src, NCH)],
                                      buf_ref.at[half, k, pl.ds(r0 + tt * NCH, NCH)],
                                      row_sem.at[half]).start(priority=k % 2)

    def finish8(t8, cur):
        t0 = pl.multiple_of(t8 * 8, 8)
        gw = gw_ref[pl.ds(t0, 8), :]
        for c in range(NCH):
            lanes = slice(c * LANES, (c + 1) * LANES)
            ffn = acc_ref[pl.ds(t0, 8), lanes]
            for k in range(TOP_K):
                ffn = ffn + gw[:, k:k + 1] * buf_ref[cur, k, pl.ds(t0 * NCH + c, 8, stride=NCH), :]
            gate = gt2_ref[pl.ds(t0, 8), lanes] if per_token_gate else gt2_ref[0][:, lanes]
            y_ref[pl.ds(t0, 8), lanes] = x1_ref[pl.ds(t0, 8), lanes] + gate * ffn

    @pl.when(i == 0)
    def _():
        _idx_copy(dst_hbm, idx_smem, idx_sem, step0, n, 0).start()
        _idx_copy(dst_hbm, idx_smem, idx_sem, step0, n, 0).wait()

        def body(t8, carry):
            issue(t8 * (8 * TOP_K), pl.multiple_of(t8 * (8 * NCH), 8 * NCH), 0)
            return carry

        lax.fori_loop(0, tm // 8, body, 0)

        @pl.when(nsteps > 1)
        def _():
            _idx_copy(dst_hbm, idx_smem, idx_sem, step0 + 1, n, 1).start()

    h2 = _load_rows(h2_ref, tm).astype(BF16)
    sg = jnp.dot(h2, wsg_ref[...], preferred_element_type=F32)
    su = jnp.dot(h2, wsu_ref[...], preferred_element_type=F32)
    acc_ref[...] = jnp.dot((sg * jax.nn.sigmoid(sg) * su).astype(BF16), wsd_ref[...], preferred_element_type=F32)

    def step(cur, has_next):
        pltpu.make_async_copy(ys_hbm.at[pl.ds(0, n * NCH)], ys_hbm.at[pl.ds(0, n * NCH)], row_sem.at[cur]).wait()
        if has_next:
            _idx_copy(dst_hbm, idx_smem, idx_sem, step0 + i + 1, n, 1 - cur).wait()

        def body(t8, carry):
            if has_next:
                issue((1 - cur) * n + t8 * (8 * TOP_K), pl.multiple_of(t8 * (8 * NCH), 8 * NCH), 1 - cur)
            finish8(t8, cur)
            return carry

        lax.fori_loop(0, tm // 8, body, 0)

        if has_next:
            @pl.when(i + 2 < nsteps)
            def _():
                _idx_copy(dst_hbm, idx_smem, idx_sem, step0 + i + 2, n, cur).start()

    for cur in range(2):
        pl.when(jnp.logical_and(i % 2 == cur, i + 1 < nsteps))(functools.partial(step, cur, True))
        pl.when(jnp.logical_and(i % 2 == cur, i + 1 == nsteps))(functools.partial(step, cur, False))


def _combine(x1, h2, gw, gt2, wsg, wsu, wsd, dst_flat, ys, *, tm, batch, seq, row_offset):
    d = x1.shape[1]
    n = tm * TOP_K
    per_token_gate = gt2.ndim == 2
    assert per_token_gate or seq % tm == 0
    per_b = max(seq // tm, 1)
    step0 = row_offset // tm
    tok = lambda i: (step0 + i, 0)
    c2 = lambda i: (0, 0)
    gate_spec = (pl.BlockSpec((tm, d), lambda i: (i, 0)) if per_token_gate
                 else pl.BlockSpec((1, 1, d), lambda i: (i // per_b, 0, 0)))
    return pl.pallas_call(
        functools.partial(_combine_kernel, tm=tm, step0=step0, per_token_gate=per_token_gate),
        grid=(batch * seq // tm,),
        in_specs=[pl.BlockSpec((tm, d), tok),
                  pl.BlockSpec((tm * NCH, LANES), tok),
                  pl.BlockSpec((tm, LANES), tok),
                  gate_spec,
                  pl.BlockSpec(wsg.shape, c2),
                  pl.BlockSpec(wsu.shape, c2),
                  pl.BlockSpec(wsd.shape, c2),
                  pl.BlockSpec(memory_space=pl.ANY),
                  pl.BlockSpec(memory_space=pl.ANY)],
        out_specs=pl.BlockSpec((tm, d), lambda i: (i, 0)),
        scratch_shapes=[pltpu.SMEM((2 * n,), jnp.int32),
                        pltpu.VMEM((2, TOP_K, tm * NCH, LANES), F32),
                        pltpu.VMEM((tm, d), F32),
                        pltpu.SemaphoreType.DMA((2,)),
                        pltpu.SemaphoreType.DMA((2,))],
        out_shape=jax.ShapeDtypeStruct((batch * seq, d), F32),
        compiler_params=_cparams(("arbitrary",)),
        name="combine",
    )(x1, h2, gw, gt2, wsg, wsu, wsd, dst_flat, ys)


def _bias_table(rel_bias, sb, kw):
    h, n_rel = rel_bias.shape
    lo = (kw - 1) - ATT_PAST_WINDOW - (CHUNK - 1)
    ntot = sb + kw - 1
    s = jnp.concatenate([jnp.broadcast_to(rel_bias[:, :1], (h, lo)), rel_bias,
                         jnp.broadcast_to(rel_bias[:, -1:], (h, ntot - lo - n_rel))], axis=1).astype(F32)
    s_rev = jnp.pad(s[:, ::-1], ((0, 0), (0, 1)))
    skew = jnp.broadcast_to(s_rev[:, None, :], (h, sb, ntot + 1)).reshape(h, -1)[:, :sb * ntot]
    toep = skew.reshape(h, sb, ntot)[:, :, sb - 1:sb - 1 + kw]
    r = np.arange(sb)[:, None]
    kk = np.arange(kw)[None, :]
    qc, kc = r // CHUNK, kk // CHUNK
    band = (kc >= qc) & (kc <= qc + LEFT_CHUNKS)
    return jnp.where(band[None], toep, NEG_INF)


def _visit_plan(offs, m, tm, nvis):
    cnt = offs[1:] - offs[:-1]
    first = offs[:-1] // tm
    last = jnp.maximum(offs[1:] - 1, 0) // tm
    per = jnp.where(cnt > 0, last - first + 1, 0)
    vend = jnp.cumsum(per)
    vstart = vend - per
    total = vend[-1]
    v = jnp.arange(nvis, dtype=jnp.int32)
    g = jnp.minimum(jnp.sum(vend[None, :] <= v[:, None], axis=1), N_EXPERTS - 1).astype(jnp.int32)
    tile = (first[g] + v - vstart[g]).astype(jnp.int32)
    live = v < total
    last_tile = m // tm - 1
    tile = jnp.where(live, tile, last_tile)
    g = jnp.where(live, g, N_EXPERTS - 1)
    return tile, g, total.reshape(1).astype(jnp.int32)


def kernel(x_prompt, x_sample, c_prompt, c_sample, cache_k, cache_v, w_ada, b_ada, g_mix, w_in, g_gmlp_v,
           w_spatial, b_spatial, g_q, g_k, rel_bias, g_out_a, g_out_b, w_out, g_ffn, w_router, b_router,
           w_exp_gate, w_exp_up, w_exp_down, w_sh_gate, w_sh_up, w_sh_down):
    depth = w_ada.shape[0]
    assert depth == 1, "single layer"
    bp, sp, d = x_prompt.shape
    bs, ss, _ = x_sample.shape
    tp, ts = bp * sp, bs * ss
    t_all = tp + ts
    l = 0

    c_all = jnp.concatenate([c_prompt, c_sample], axis=0)
    nb = c_all.shape[0]
    nb_pad = -(-nb // 8) * 8
    c_all = jnp.pad(c_all, ((0, nb_pad - nb), (0, 0)))
    mod = _ada(c_all, w_ada[l], b_ada[l])[:nb].reshape(nb, 6, 1, d)
    sh1, sc1, gt1, sh2, sc2, gt2 = [mod[:, i] for i in range(6)]

    w_in_bf = w_in[l].astype(BF16)
    w_out_bf = w_out[l].astype(BF16)
    gv = g_gmlp_v[l].reshape(1, W_A)
    gq = jnp.tile(g_q[l], B_HEADS).reshape(1, W_B)
    gk = jnp.tile(g_k[l], B_HEADS).reshape(1, W_B)
    hid = np.arange(W_B) // HEAD_DIM
    hsum = jnp.asarray(hid[:, None] == hid[None, :], BF16)
    gmix = g_mix[l].reshape(1, d)
    goa = g_out_a[l].reshape(1, W_A)
    gob = g_out_b[l].reshape(1, W_B)
    gffn = g_ffn[l].reshape(1, d)
    bias_p = _bias_table(rel_bias[l], GMLP_BLOCK, ATT_PAST_WINDOW + GMLP_BLOCK)
    bsp_p = jnp.broadcast_to(b_spatial[l][:, :, None], (A_GROUPS, GMLP_BLOCK, A_GROUP_DIM))

    cl = cache_k.shape[2]
    us, vns, qs, ks_bf, vs_bf, k32s, v32s = _in_proj(
        x_sample, sh1[bp:], sc1[bp:], gmix, w_in_bf, gv, gq, gk, hsum,
        tm=ss, pad_rows=0, keep=ss, vn_dtype=F32)
    kcat = jnp.concatenate([cache_k[l].reshape(bs, cl, W_B).astype(BF16), ks_bf], axis=1)
    vcat = jnp.concatenate([cache_v[l].reshape(bs, cl, W_B).astype(BF16), vs_bf], axis=1)
    bias_s = bias_p[:, :ss, ATT_PAST_WINDOW - cl:ATT_PAST_WINDOW + ss]
    x1_s, h2_s = _mixer(x_sample, us, vns, qs, kcat, vcat, bias_s, w_spatial[l][:, :ss, :ss], bsp_p[:, :ss],
                        goa, gob, w_out_bf, gt1[bp:], gffn, sh2[bp:], sc2[bp:],
                        sb=ss, nsb=1, padded=False)
    new_k_sample = k32s.reshape(1, bs, ss, B_HEADS, HEAD_DIM)
    new_v_sample = v32s.reshape(1, bs, ss, B_HEADS, HEAD_DIM)
    new_gmlp_v_sample = vns.reshape(1, bs, ss, A_GROUPS, A_GROUP_DIM)

    keep = min(ATT_PAST_WINDOW, sp)
    u, vn, q, kpad, vpad, k32, v32 = _in_proj(
        x_prompt, sh1[:bp], sc1[:bp], gmix, w_in_bf, gv, gq, gk, hsum,
        tm=512, pad_rows=ATT_PAST_WINDOW, keep=keep, vn_dtype=BF16)
    x1, h2 = _mixer(x_prompt, u, vn, q, kpad, vpad, bias_p, w_spatial[l], bsp_p, goa, gob, w_out_bf,
                    gt1[:bp], gffn, sh2[:bp], sc2[:bp],
                    sb=GMLP_BLOCK, nsb=4 if sp % 512 == 0 and ts % 512 == 0 else 2, padded=True,
                    tail=(x1_s, h2_s))
    assert x1.shape[0] == t_all
    new_k_prompt = k32.reshape(1, bp, keep, B_HEADS, HEAD_DIM)
    new_v_prompt = v32.reshape(1, bp, keep, B_HEADS, HEAD_DIM)

    tmr = 512 if t_all % 512 == 0 else 256
    tmd = 512 if t_all % 512 == 0 else 256
    tme = 1024 if (t_all * TOP_K) % 1024 == 0 else 512
    assert t_all % tmr == 0 and (t_all * TOP_K) % tme == 0 and sp % 256 == 0 and ts % (2 * ss) == 0
    assert t_all * TOP_K < (1 << RANK_BITS)
    pk, gw_t, cnt = _route(h2, w_router[l].T.astype(BF16), b_router[l], tm=tmr)
    offs = jnp.concatenate([jnp.zeros((1,), jnp.int32), jnp.cumsum(cnt[:, 0]).astype(jnp.int32)])
    tms = next(c for c in (2048, 1536, 1024, 512, 256) if t_all % c == 0)
    dst_flat = _slots(pk, offs, tm=tms).T.reshape(-1)
    gw = jnp.pad(gw_t.T, ((0, 0), (0, LANES - TOP_K)))
    xs = _dispatch(h2, dst_flat, tm=tmd)
    m = t_all * TOP_K
    nvis = m // tme + N_EXPERTS - 1
    vt, vg, nv = _visit_plan(offs, m, tme, nvis)
    ys = _experts(vt, vg, offs, nv, xs, w_exp_gate[l], w_exp_up[l], w_exp_down[l], tm=tme)

    wsg, wsu, wsd = w_sh_gate[l].astype(BF16), w_sh_up[l].astype(BF16), w_sh_down[l].astype(BF16)
    y_prompt = _combine(x1, h2, gw, gt2[:bp], wsg, wsu, wsd, dst_flat, ys,
                        tm=512 if sp % 512 == 0 else 256, batch=bp, seq=sp, row_offset=0).reshape(bp, sp, d)
    gt2_tok = jnp.broadcast_to(gt2[bp:], (bs, ss, d)).reshape(ts, d)
    y_sample = _combine(x1, h2, gw, gt2_tok, wsg, wsu, wsd, dst_flat, ys,
                        tm=2 * ss, batch=bs, seq=ss, row_offset=tp).reshape(bs, ss, d)
    return (y_prompt, y_sample, new_k_prompt, new_v_prompt, new_k_sample, new_v_sample, new_gmlp_v_sample)
```

```python
import functools

import jax
import jax.numpy as jnp
import numpy as np
from jax import lax
from jax.experimental import pallas as pl
from jax.experimental.pallas import tpu as pltpu

F32 = jnp.float32
BF16 = jnp.bfloat16

D_MODEL = 1024
CHUNK = 64
GMLP_BLOCK = 128
W_A = 512
A_GROUPS = 4
A_GROUP_DIM = 128
W_B = 512
B_HEADS = 8
HEAD_DIM = 64
LEFT_CHUNKS = 8
ATT_PAST_WINDOW = LEFT_CHUNKS * CHUNK
REL_CLIP = 128
D_IN = 2 * W_A + 3 * W_B
N_EXPERTS = 256
TOP_K = 8
D_EXPERT = 256
ROUTED_SCALE = 2.5
EPS = 1e-6
NEG_INF = -1e30

RANK_BITS = 20
VMEM_LIMIT = 56 * 1024 * 1024


def _cparams(sem, vmem=VMEM_LIMIT):
    return pltpu.CompilerParams(dimension_semantics=sem, vmem_limit_bytes=vmem)


def _rms(x, axis=-1):
    return lax.rsqrt(jnp.mean(x * x, axis=axis, keepdims=True) + EPS)


LANES = 128
NCH = D_MODEL // LANES


def _load_rows(ref, n, lead=()):
    return jnp.concatenate([ref[(*lead, pl.ds(c, n, stride=NCH), slice(None))] for c in range(NCH)], axis=1)


def _store_rows(ref, val):
    n = val.shape[0]
    for c in range(NCH):
        ref[pl.ds(c, n, stride=NCH), :] = val[:, c * LANES:(c + 1) * LANES]


def _ada_kernel(c_ref, w_ref, b_ref, o_ref):
    c = c_ref[...]
    s = c * jax.nn.sigmoid(c)
    o_ref[...] = jnp.dot(s.astype(BF16), w_ref[...].astype(BF16), preferred_element_type=F32) + b_ref[...]


def _ada(c, w_ada, b_ada):
    n, d = c.shape
    nout = w_ada.shape[1]
    tn = 1024
    return pl.pallas_call(
        _ada_kernel,
        grid=(nout // tn,),
        in_specs=[pl.BlockSpec((n, d), lambda j: (0, 0)),
                  pl.BlockSpec((d, tn), lambda j: (0, j)),
                  pl.BlockSpec((1, tn), lambda j: (0, j))],
        out_specs=pl.BlockSpec((n, tn), lambda j: (0, j)),
        out_shape=jax.ShapeDtypeStruct((n, nout), F32),
        compiler_params=_cparams(("arbitrary",)),
        name="ada",
    )(c, w_ada, b_ada.reshape(1, nout))


def _in_proj_kernel(x_ref, sh_ref, sc_ref, gmix_ref, win_ref, gv_ref, gq_ref, gk_ref, hsum_ref,
                    u_ref, vn_ref, q_ref, kp_ref, vp_ref, k32_ref, v32_ref, *, npad, first_kept):
    i = pl.program_id(1)

    @pl.when(i < npad)
    def _():
        kp_ref[...] = jnp.zeros_like(kp_ref)
        vp_ref[...] = jnp.zeros_like(vp_ref)

    @pl.when(i >= npad)
    def _():
        x = x_ref[0]
        h = x * _rms(x) * gmix_ref[...]
        h = h * (1.0 + sc_ref[0]) + sh_ref[0]
        proj = jnp.dot(h.astype(BF16), win_ref[...], preferred_element_type=F32)
        u = jax.nn.gelu(proj[:, 0:W_A])
        u_ref[0] = u.astype(u_ref.dtype)
        v = jax.nn.gelu(proj[:, W_A:2 * W_A])
        for g in range(A_GROUPS):
            vg = v[:, g * A_GROUP_DIM:(g + 1) * A_GROUP_DIM]
            vn = vg * _rms(vg) * gv_ref[:, g * A_GROUP_DIM:(g + 1) * A_GROUP_DIM]
            vn_ref[0, :, g * A_GROUP_DIM:(g + 1) * A_GROUP_DIM] = vn.astype(vn_ref.dtype)
        q = proj[:, 2 * W_A:2 * W_A + W_B]
        k = proj[:, 2 * W_A + W_B:2 * W_A + 2 * W_B]
        va = proj[:, 2 * W_A + 2 * W_B:]
        q2 = jnp.dot((q * q).astype(BF16), hsum_ref[...], preferred_element_type=F32) * (1.0 / HEAD_DIM)
        k2 = jnp.dot((k * k).astype(BF16), hsum_ref[...], preferred_element_type=F32) * (1.0 / HEAD_DIM)
        qn = q * lax.rsqrt(q2 + EPS) * gq_ref[...]
        kn = k * lax.rsqrt(k2 + EPS) * gk_ref[...]
        q_ref[0] = (qn * (HEAD_DIM ** -0.5)).astype(q_ref.dtype)
        kp_ref[0] = kn.astype(kp_ref.dtype)
        vp_ref[0] = va.astype(vp_ref.dtype)

        @pl.when(i >= first_kept)
        def _():
            k32_ref[0] = kn
            v32_ref[0] = va


def _in_proj(x, sh1, sc1, g_mix, w_in_bf, gv, gq, gk, hsum, *, tm, pad_rows, keep, vn_dtype):
    b, l, d = x.shape
    assert l % tm == 0 and pad_rows % tm == 0 and keep % tm == 0
    npad = pad_rows // tm
    nt = l // tm
    first_kept = npad + nt - keep // tm
    row = lambda bi, i: (bi, jnp.maximum(i - npad, 0), 0)
    kept = lambda bi, i: (bi, jnp.maximum(i - first_kept, 0), 0)
    bvec = lambda bi, i: (bi, 0, 0)
    const2 = lambda bi, i: (0, 0)
    out_shapes = (
        jax.ShapeDtypeStruct((b, l, W_A), BF16),
        jax.ShapeDtypeStruct((b, l, W_A), vn_dtype),
        jax.ShapeDtypeStruct((b, l, W_B), BF16),
        jax.ShapeDtypeStruct((b, pad_rows + l, W_B), BF16),
        jax.ShapeDtypeStruct((b, pad_rows + l, W_B), BF16),
        jax.ShapeDtypeStruct((b, keep, W_B), F32),
        jax.ShapeDtypeStruct((b, keep, W_B), F32),
    )
    padrow = lambda bi, i: (bi, i, 0)
    return pl.pallas_call(
        functools.partial(_in_proj_kernel, npad=npad, first_kept=first_kept),
        grid=(b, nt + npad),
        in_specs=[pl.BlockSpec((1, tm, d), row),
                  pl.BlockSpec((1, 1, d), bvec),
                  pl.BlockSpec((1, 1, d), bvec),
                  pl.BlockSpec((1, d), const2),
                  pl.BlockSpec((d, D_IN), const2),
                  pl.BlockSpec((1, W_A), const2),
                  pl.BlockSpec((1, W_B), const2),
                  pl.BlockSpec((1, W_B), const2),
                  pl.BlockSpec((W_B, W_B), const2)],
        out_specs=(pl.BlockSpec((1, tm, W_A), row),
                   pl.BlockSpec((1, tm, W_A), row),
                   pl.BlockSpec((1, tm, W_B), row),
                   pl.BlockSpec((1, tm, W_B), padrow),
                   pl.BlockSpec((1, tm, W_B), padrow),
                   pl.BlockSpec((1, tm, W_B), kept),
                   pl.BlockSpec((1, tm, W_B), kept)),
        out_shape=out_shapes,
        compiler_params=_cparams(("arbitrary", "arbitrary")),
        name="in_proj",
    )(x, sh1, sc1, g_mix, w_in_bf, gv, gq, gk, hsum)


def _mixer_kernel(x_ref, u_ref, vn_ref, q_ref, kp_ref, vp_ref, bias_ref, wsp_ref, bsp_ref,
                  goa_ref, gob_ref, wout_ref, gt1_ref, gffn_ref, sh2_ref, sc2_ref, *rest,
                  sb, nsb, kw, padded, nj, nmain):
    x1_ref, h2_ref, merged_ref = rest[-3:]
    i = pl.program_id(0)
    if len(rest) == 5:
        tx1_ref, th2_ref = rest[:2]

        @pl.when(i >= nmain)
        def _():
            x1_ref[...] = tx1_ref[...]
            h2_ref[...] = th2_ref[...]

    @pl.when(i < nmain)
    def _():
        _mixer_body(x_ref, u_ref, vn_ref, q_ref, kp_ref, vp_ref, bias_ref, wsp_ref, bsp_ref,
                    goa_ref, gob_ref, wout_ref, gt1_ref, gffn_ref, sh2_ref, sc2_ref,
                    x1_ref, h2_ref, merged_ref, i % nj, sb=sb, nsb=nsb, kw=kw, padded=padded)


def _mixer_body(x_ref, u_ref, vn_ref, q_ref, kp_ref, vp_ref, bias_ref, wsp_ref, bsp_ref,
                goa_ref, gob_ref, wout_ref, gt1_ref, gffn_ref, sh2_ref, sc2_ref,
                x1_ref, h2_ref, merged_ref, j, *, sb, nsb, kw, padded):
    rows = sb * nsb
    ri = lax.broadcasted_iota(jnp.int32, (sb, sb), 0)
    ci = lax.broadcasted_iota(jnp.int32, (sb, sb), 1)
    gmask = (ci // CHUNK) <= (ri // CHUNK)
    lane = lax.broadcasted_iota(jnp.int32, (sb, 2 * HEAD_DIM), 1)
    lo_half = lane < HEAD_DIM
    kidx = lax.broadcasted_iota(jnp.int32, (sb, kw), 1)

    for s in range(nsb):
        r0 = s * sb
        vn = vn_ref[0, r0:r0 + sb, :].astype(BF16)
        u = u_ref[0, r0:r0 + sb, :].astype(F32)
        zs = []
        for g in range(A_GROUPS):
            w = jnp.where(gmask, wsp_ref[g], 0.0).astype(BF16)
            z = jnp.dot(w, vn[:, g * A_GROUP_DIM:(g + 1) * A_GROUP_DIM], preferred_element_type=F32)
            zs.append(z + bsp_ref[g])
        a = u * jnp.concatenate(zs, axis=1)
        a = a * _rms(a) * goa_ref[...]
        merged_ref[r0:r0 + sb, 0:W_A] = a.astype(BF16)

        if padded:
            blk0 = j * rows + r0
            kstart = pl.multiple_of(blk0, sb)
            kb = kp_ref[0, pl.ds(kstart, kw), :]
            vb = vp_ref[0, pl.ds(kstart, kw), :]
            valid = kidx >= (ATT_PAST_WINDOW - blk0)
        else:
            kb = kp_ref[0]
            vb = vp_ref[0]
            valid = None
        qb = q_ref[0, r0:r0 + sb, :]
        scs = []
        for h in range(B_HEADS):
            c0 = (h // 2) * 2 * HEAD_DIM
            qp = qb[:, c0:c0 + 2 * HEAD_DIM]
            qm = jnp.where(lo_half if h % 2 == 0 else jnp.logical_not(lo_half), qp, jnp.zeros_like(qp))
            sc = lax.dot_general(qm, kb[:, c0:c0 + 2 * HEAD_DIM], (((1,), (1,)), ((), ())),
                                 preferred_element_type=F32)
            scs.append(sc + bias_ref[h])
        ps, ls = [], []
        for sc in scs:
            if valid is not None:
                sc = jnp.where(valid, sc, NEG_INF)
            p = jnp.exp(sc - jnp.max(sc, axis=-1, keepdims=True))
            ls.append(jnp.sum(p, axis=-1, keepdims=True))
            ps.append(p.astype(BF16))
        os_ = []
        for h in range(B_HEADS):
            c0 = (h // 2) * 2 * HEAD_DIM
            os_.append(jnp.dot(ps[h], vb[:, c0:c0 + 2 * HEAD_DIM], preferred_element_type=F32) / ls[h])
        outs = [jnp.where(lo_half, os_[2 * hp], os_[2 * hp + 1]) for hp in range(B_HEADS // 2)]
        bo = jnp.concatenate(outs, axis=1)
        bo = bo * _rms(bo) * gob_ref[...]
        merged_ref[r0:r0 + sb, W_A:W_A + W_B] = bo.astype(BF16)

    mix = jnp.dot(merged_ref[...], wout_ref[...], preferred_element_type=F32)
    x1 = x_ref[0] + gt1_ref[0] * mix
    x1_ref[...] = x1
    h2 = x1 * _rms(x1) * gffn_ref[...]
    _store_rows(h2_ref, h2 * (1.0 + sc2_ref[0]) + sh2_ref[0])


def _mixer(x, u, vn, q, kp, vp, bias, wsp, bsp, goa, gob, wout_bf, gt1, gffn, sh2, sc2,
           *, sb, nsb, padded, tail=None):
    b, l, d = x.shape
    rows = sb * nsb
    assert l % rows == 0
    nj = l // rows
    nmain = b * nj
    ntail = 0
    if tail is not None:
        assert tail[0].shape[0] % rows == 0
        ntail = tail[0].shape[0] // rows
    total_rows = (nmain + ntail) * rows
    kw = bias.shape[2]
    kl = kp.shape[1]
    bidx = lambda i: jnp.minimum(i // nj, b - 1)
    row = lambda i: (bidx(i), i % nj, 0)
    bvec = lambda i: (bidx(i), 0, 0)
    c2 = lambda i: (0, 0)
    c3 = lambda i: (0, 0, 0)
    pool = lambda i: (i, 0)
    in_specs = [pl.BlockSpec((1, rows, d), row),
                pl.BlockSpec((1, rows, W_A), row),
                pl.BlockSpec((1, rows, W_A), row),
                pl.BlockSpec((1, rows, W_B), row),
                pl.BlockSpec((1, kl, W_B), bvec),
                pl.BlockSpec((1, kl, W_B), bvec),
                pl.BlockSpec(bias.shape, c3),
                pl.BlockSpec(wsp.shape, c3),
                pl.BlockSpec(bsp.shape, c3),
                pl.BlockSpec((1, W_A), c2),
                pl.BlockSpec((1, W_B), c2),
                pl.BlockSpec((d, d), c2),
                pl.BlockSpec((1, 1, d), bvec),
                pl.BlockSpec((1, d), c2),
                pl.BlockSpec((1, 1, d), bvec),
                pl.BlockSpec((1, 1, d), bvec)]
    args = [x, u, vn, q, kp, vp, bias, wsp, bsp, goa, gob, wout_bf, gt1, gffn, sh2, sc2]
    if tail is not None:
        tmap = lambda i: (jnp.maximum(i - nmain, 0), 0)
        in_specs += [pl.BlockSpec((rows, d), tmap), pl.BlockSpec((rows * NCH, LANES), tmap)]
        args += list(tail)
    return pl.pallas_call(
        functools.partial(_mixer_kernel, sb=sb, nsb=nsb, kw=kw, padded=padded, nj=nj, nmain=nmain),
        grid=(nmain + ntail,),
        in_specs=in_specs,
        out_specs=(pl.BlockSpec((rows, d), pool), pl.BlockSpec((rows * NCH, LANES), pool)),
        out_shape=(jax.ShapeDtypeStruct((total_rows, d), F32),
                   jax.ShapeDtypeStruct((total_rows * NCH, LANES), F32)),
        scratch_shapes=[pltpu.VMEM((rows, d), BF16)],
        compiler_params=_cparams(("arbitrary",)),
        name="mixer",
    )(*args)


def _route_kernel(h2_ref, wrt_ref, br_ref, pk_ref, gw_ref, cnt_ref, run_ref, *, tm):
    i = pl.program_id(0)

    @pl.when(i == 0)
    def _():
        run_ref[...] = jnp.zeros_like(run_ref)

    x = _load_rows(h2_ref, tm).astype(BF16)
    logits = lax.dot_general(wrt_ref[...], x, (((1,), (1,)), ((), ())), preferred_element_type=F32)
    aff = jax.nn.sigmoid(logits)
    sel = aff + br_ref[...]
    row = lax.broadcasted_iota(jnp.int32, (N_EXPERTS, tm), 0)
    picks = []
    onehot = jnp.zeros((N_EXPERTS, tm), F32)
    for _ in range(TOP_K):
        m = jnp.max(sel, axis=0, keepdims=True)
        ik = jnp.min(jnp.where(sel == m, row, N_EXPERTS), axis=0, keepdims=True)
        oh = row == ik
        sel = jnp.where(oh, -jnp.inf, sel)
        onehot = jnp.where(oh, 1.0, onehot)
        picks.append((ik, oh))
    ri = lax.broadcasted_iota(jnp.int32, (tm, tm), 0)
    ci = lax.broadcasted_iota(jnp.int32, (tm, tm), 1)
    earlier = (ri < ci).astype(BF16)
    before = jnp.dot(onehot.astype(BF16), earlier, preferred_element_type=F32) + run_ref[...]
    affs = [jnp.sum(jnp.where(oh, aff, 0.0), axis=0, keepdims=True) for _, oh in picks]
    tot = affs[0]
    for a in affs[1:]:
        tot = tot + a
    for k, (ik, oh) in enumerate(picks):
        rank = jnp.sum(jnp.where(oh, before, 0.0), axis=0, keepdims=True).astype(jnp.int32)
        pk_ref[k:k + 1, :] = ik * (1 << RANK_BITS) + rank
        gw_ref[k:k + 1, :] = affs[k] / tot * ROUTED_SCALE
    run = run_ref[...] + jnp.sum(onehot, axis=1, keepdims=True)
    run_ref[...] = run
    cnt_ref[...] = run.astype(jnp.int32)


def _route(h2, wrt_bf, b_router, *, tm):
    t, d = h2.shape[0] // NCH, D_MODEL
    return pl.pallas_call(
        functools.partial(_route_kernel, tm=tm),
        grid=(t // tm,),
        in_specs=[pl.BlockSpec((tm * NCH, LANES), lambda i: (i, 0)),
                  pl.BlockSpec((N_EXPERTS, d), lambda i: (0, 0)),
                  pl.BlockSpec((N_EXPERTS, 1), lambda i: (0, 0))],
        out_specs=(pl.BlockSpec((TOP_K, tm), lambda i: (0, i)),
                   pl.BlockSpec((TOP_K, tm), lambda i: (0, i)),
                   pl.BlockSpec((N_EXPERTS, 1), lambda i: (0, 0))),
        out_shape=(jax.ShapeDtypeStruct((TOP_K, t), jnp.int32),
                   jax.ShapeDtypeStruct((TOP_K, t), F32),
                   jax.ShapeDtypeStruct((N_EXPERTS, 1), jnp.int32)),
        scratch_shapes=[pltpu.VMEM((N_EXPERTS, 1), F32)],
        compiler_params=_cparams(("arbitrary",)),
        name="route",
    )(h2, wrt_bf, b_router.reshape(N_EXPERTS, 1))


def _idx_copy(idx_hbm, idx_smem, sem, step, n, half):
    return pltpu.make_async_copy(idx_hbm.at[pl.ds(pl.multiple_of(step * n, n), n)],
                                 idx_smem.at[pl.ds(half * n, n)], sem.at[half])


def _slots_kernel(pk_ref, offs_ref, dst_ref):
    half = N_EXPERTS // 2
    shape = (TOP_K, LANES)
    lo_tab = jnp.broadcast_to(offs_ref[:, 0:half], shape)
    hi_tab = jnp.broadcast_to(offs_ref[:, half:N_EXPERTS], shape)
    for j in range(pk_ref.shape[1] // LANES):
        lanes = slice(j * LANES, (j + 1) * LANES)
        p = pk_ref[:, lanes]
        e = p >> RANK_BITS
        start = jnp.where(e >= half, jnp.take_along_axis(hi_tab, e & (half - 1), axis=1),
                          jnp.take_along_axis(lo_tab, e & (half - 1), axis=1))
        dst_ref[:, lanes] = ((p & ((1 << RANK_BITS) - 1)) + start) * NCH


def _slots(pk, offs, *, tm):
    t = pk.shape[1]
    return pl.pallas_call(
        _slots_kernel,
        grid=(t // tm,),
        in_specs=[pl.BlockSpec((TOP_K, tm), lambda i: (0, i)),
                  pl.BlockSpec((1, N_EXPERTS), lambda i: (0, 0))],
        out_specs=pl.BlockSpec((TOP_K, tm), lambda i: (0, i)),
        out_shape=jax.ShapeDtypeStruct((TOP_K, t), jnp.int32),
        compiler_params=_cparams(("arbitrary",)),
        name="slots",
    )(pk, offs[:N_EXPERTS].reshape(1, N_EXPERTS))


def _dispatch_kernel(h2_ref, dst_hbm, xs_hbm, idx_smem, idx_sem, row_sem, *, tm):
    i = pl.program_id(0)
    n = tm * TOP_K

    @pl.when(i == 0)
    def _():
        _idx_copy(dst_hbm, idx_smem, idx_sem, i, n, 0).start()

    def scatter(half):
        _idx_copy(dst_hbm, idx_smem, idx_sem, i, n, half).wait()

        @pl.when(i + 1 < pl.num_programs(0))
        def _():
            _idx_copy(dst_hbm, idx_smem, idx_sem, i + 1, n, 1 - half).start()

        def body(t8, carry):
            r0 = pl.multiple_of(t8 * (8 * NCH), 8 * NCH)
            b = half * n + t8 * (8 * TOP_K)
            for tt in range(8):
                src = h2_ref.at[pl.ds(r0 + tt * NCH, NCH)]
                for k in range(TOP_K):
                    dst = pl.multiple_of(idx_smem[b + tt * TOP_K + k], NCH)
                    pltpu.make_async_copy(src, xs_hbm.at[pl.ds(dst, NCH)], row_sem).start(priority=k % 2)
            return carry

        lax.fori_loop(0, tm // 8, body, 0)

    for half in range(2):
        pl.when(i % 2 == half)(functools.partial(scatter, half))
    pltpu.make_async_copy(xs_hbm.at[pl.ds(0, n * NCH)], xs_hbm.at[pl.ds(0, n * NCH)], row_sem).wait()


def _dispatch(h2, dst_flat, *, tm):
    t = h2.shape[0] // NCH
    n = tm * TOP_K
    return pl.pallas_call(
        functools.partial(_dispatch_kernel, tm=tm),
        grid=(t // tm,),
        in_specs=[pl.BlockSpec((tm * NCH, LANES), lambda i: (i, 0)),
                  pl.BlockSpec(memory_space=pl.ANY)],
        out_specs=pl.BlockSpec(memory_space=pl.ANY),
        scratch_shapes=[pltpu.SMEM((2 * n,), jnp.int32),
                        pltpu.SemaphoreType.DMA((2,)),
                        pltpu.SemaphoreType.DMA(())],
        out_shape=jax.ShapeDtypeStruct((t * TOP_K * NCH, LANES), F32),
        compiler_params=_cparams(("arbitrary",)),
        name="dispatch",
    )(h2, dst_flat)


def _experts_kernel(vt_ref, vg_ref, offs_ref, nv_ref, xs_ref, wg_ref, wu_ref, wd_ref, ys_ref,
                    wgb_ref, wub_ref, wdb_ref, *, tm):
    v = pl.program_id(0)

    @pl.when(v < nv_ref[0])
    def _():
        g = vg_ref[v]
        tile = vt_ref[v]
        prev = jnp.maximum(v - 1, 0)
        first = jnp.logical_or(v == 0, vt_ref[prev] != tile)

        @pl.when(jnp.logical_or(v == 0, vg_ref[prev] != g))
        def _():
            wgb_ref[...] = wg_ref[0].astype(BF16)
            wub_ref[...] = wu_ref[0].astype(BF16)
            wdb_ref[...] = wd_ref[0].astype(BF16)

        def swiglu_rows(r0, nr, masked):
            x = jnp.concatenate([xs_ref[pl.ds(r0 * NCH + c, nr, stride=NCH), :] for c in range(NCH)],
                                axis=1).astype(BF16)
            hg = jnp.dot(x, wgb_ref[...], preferred_element_type=F32)
            hu = jnp.dot(x, wub_ref[...], preferred_element_type=F32)
            hm = (hg * jax.nn.sigmoid(hg) * hu).astype(BF16)
            if masked:
                rowid = tile * tm + r0 + lax.broadcasted_iota(jnp.int32, (nr, 1), 0)
                mine = jnp.logical_and(rowid >= offs_ref[g], rowid < offs_ref[g + 1])
            for c2 in range(NCH // 2):
                y2 = jnp.dot(hm, wdb_ref[:, c2 * 2 * LANES:(c2 + 1) * 2 * LANES], preferred_element_type=F32)
                for j in range(2):
                    dst = pl.ds(r0 * NCH + 2 * c2 + j, nr, stride=NCH)
                    yc = y2[:, j * LANES:(j + 1) * LANES]
                    ys_ref[dst, :] = jnp.where(mine, yc, ys_ref[dst, :]) if masked else yc

        def visit(masked):
            for r0 in range(0, tm, tm // 2):
                swiglu_rows(r0, tm // 2, masked)

        pl.when(first)(functools.partial(visit, False))
        pl.when(jnp.logical_not(first))(functools.partial(visit, True))


def _experts(vt, vg, offs, nv, xs, w_gate, w_up, w_down, *, tm):
    m, d = xs.shape[0] // NCH, D_MODEL
    nvis = vt.shape[0]
    de = w_gate.shape[2]
    xmap = lambda v, vt, vg, offs, nv: (vt[v], 0)
    wmap = lambda v, vt, vg, offs, nv: (vg[v], 0, 0)
    grid_spec = pltpu.PrefetchScalarGridSpec(
        num_scalar_prefetch=4,
        grid=(nvis,),
        in_specs=[pl.BlockSpec((tm * NCH, LANES), xmap),
                  pl.BlockSpec((1, d, de), wmap),
                  pl.BlockSpec((1, d, de), wmap),
                  pl.BlockSpec((1, de, d), wmap)],
        out_specs=pl.BlockSpec((tm * NCH, LANES), xmap),
        scratch_shapes=[pltpu.VMEM((d, de), BF16), pltpu.VMEM((d, de), BF16), pltpu.VMEM((de, d), BF16)],
    )
    return pl.pallas_call(
        functools.partial(_experts_kernel, tm=tm),
        grid_spec=grid_spec,
        out_shape=jax.ShapeDtypeStruct((m * NCH, LANES), F32),
        compiler_params=_cparams(("arbitrary",)),
        name="experts",
    )(vt, vg, offs, nv, xs, w_gate, w_up, w_down)


def _combine_kernel(x1_ref, h2_ref, gw_ref, gt2_ref, wsg_ref, wsu_ref, wsd_ref, dst_hbm, ys_hbm,
                    y_ref, idx_smem, buf_ref, acc_ref, idx_sem, row_sem, *, tm, step0, per_token_gate):
    i = pl.program_id(0)
    nsteps = pl.num_programs(0)
    n = tm * TOP_K

    def issue(b, r0, half):
        for tt in range(8):
            for k in range(TOP_K):
                src = pl.multiple_of(idx_smem[b + tt * TOP_K + k], NCH)
                pltpu.make_async_copy(ys_hbm.at[pl.ds(src, NCH)],
                                      buf_ref.at[half, k, pl.ds(r0 + tt * NCH, NCH)],
                                      row_sem.at[half]).start(priority=k % 2)

    def finish8(t8, cur):
        t0 = pl.multiple_of(t8 * 8, 8)
        gw = gw_ref[pl.ds(t0, 8), :]
        for c in range(NCH):
            lanes = slice(c * LANES, (c + 1) * LANES)
            ffn = acc_ref[pl.ds(t0, 8), lanes]
            for k in range(TOP_K):
                ffn = ffn + gw[:, k:k + 1] * buf_ref[cur, k, pl.ds(t0 * NCH + c, 8, stride=NCH), :]
            gate = gt2_ref[pl.ds(t0, 8), lanes] if per_token_gate else gt2_ref[0][:, lanes]
            y_ref[pl.ds(t0, 8), lanes] = x1_ref[pl.ds(t0, 8), lanes] + gate * ffn

    @pl.when(i == 0)
    def _():
        _idx_copy(dst_hbm, idx_smem, idx_sem, step0, n, 0).start()
        _idx_copy(dst_hbm, idx_smem, idx_sem, step0, n, 0).wait()

        def body(t8, carry):
            issue(t8 * (8 * TOP_K), pl.multiple_of(t8 * (8 * NCH), 8 * NCH), 0)
            return carry

        lax.fori_loop(0, tm // 8, body, 0)

        @pl.when(nsteps > 1)
        def _():
            _idx_copy(dst_hbm, idx_smem, idx_sem, step0 + 1, n, 1).start()

    n_lead = min(8, tm // 8)

    def step(cur, has_next):
        def issue_group(t8):
            issue((1 - cur) * n + t8 * (8 * TOP_K), pl.multiple_of(t8 * (8 * NCH), 8 * NCH), 1 - cur)

        def issue_only(t8, carry):
            issue_group(t8)
            return carry

        def finish_only(t8, carry):
            finish8(t8, cur)
            return carry

        def both(t8, carry):
            issue_group(t8)
            finish8(t8, cur)
            return carry

        if has_next:
            _idx_copy(dst_hbm, idx_smem, idx_sem, step0 + i + 1, n, 1 - cur).wait()
            lax.fori_loop(0, n_lead, issue_only, 0)

        h2 = _load_rows(h2_ref, tm).astype(BF16)
        sg = jnp.dot(h2, wsg_ref[...], preferred_element_type=F32)
        su = jnp.dot(h2, wsu_ref[...], preferred_element_type=F32)
        acc_ref[...] = jnp.dot((sg * jax.nn.sigmoid(sg) * su).astype(BF16), wsd_ref[...],
                               preferred_element_type=F32)

        pltpu.make_async_copy(ys_hbm.at[pl.ds(0, n * NCH)], ys_hbm.at[pl.ds(0, n * NCH)], row_sem.at[cur]).wait()
        if has_next:
            lax.fori_loop(0, n_lead, finish_only, 0)
            lax.fori_loop(n_lead, tm // 8, both, 0)
        else:
            lax.fori_loop(0, tm // 8, finish_only, 0)

        if has_next:
            @pl.when(i + 2 < nsteps)
            def _():
                _idx_copy(dst_hbm, idx_smem, idx_sem, step0 + i + 2, n, cur).start()

    for cur in range(2):
        pl.when(jnp.logical_and(i % 2 == cur, i + 1 < nsteps))(functools.partial(step, cur, True))
        pl.when(jnp.logical_and(i % 2 == cur, i + 1 == nsteps))(functools.partial(step, cur, False))


def _combine(x1, h2, gw, gt2, wsg, wsu, wsd, dst_flat, ys, *, tm, batch, seq, row_offset):
    d = x1.shape[1]
    n = tm * TOP_K
    per_token_gate = gt2.ndim == 2
    assert per_token_gate or seq % tm == 0
    per_b = max(seq // tm, 1)
    step0 = row_offset // tm
    tok = lambda i: (step0 + i, 0)
    c2 = lambda i: (0, 0)
    gate_spec = (pl.BlockSpec((tm, d), lambda i: (i, 0)) if per_token_gate
                 else pl.BlockSpec((1, 1, d), lambda i: (i // per_b, 0, 0)))
    return pl.pallas_call(
        functools.partial(_combine_kernel, tm=tm, step0=step0, per_token_gate=per_token_gate),
        grid=(batch * seq // tm,),
        in_specs=[pl.BlockSpec((tm, d), tok),
                  pl.BlockSpec((tm * NCH, LANES), tok),
                  pl.BlockSpec((tm, LANES), tok),
                  gate_spec,
                  pl.BlockSpec(wsg.shape, c2),
                  pl.BlockSpec(wsu.shape, c2),
                  pl.BlockSpec(wsd.shape, c2),
                  pl.BlockSpec(memory_space=pl.ANY),
                  pl.BlockSpec(memory_space=pl.ANY)],
        out_specs=pl.BlockSpec((tm, d), lambda i: (i, 0)),
        scratch_shapes=[pltpu.SMEM((2 * n,), jnp.int32),
                        pltpu.VMEM((2, TOP_K, tm * NCH, LANES), F32),
                        pltpu.VMEM((tm, d), F32),
                        pltpu.SemaphoreType.DMA((2,)),
                        pltpu.SemaphoreType.DMA((2,))],
        out_shape=jax.ShapeDtypeStruct((batch * seq, d), F32),
        compiler_params=_cparams(("arbitrary",)),
        name="combine",
    )(x1, h2, gw, gt2, wsg, wsu, wsd, dst_flat, ys)


def _bias_table(rel_bias, sb, kw):
    h, n_rel = rel_bias.shape
    lo = (kw - 1) - ATT_PAST_WINDOW - (CHUNK - 1)
    ntot = sb + kw - 1
    s = jnp.concatenate([jnp.broadcast_to(rel_bias[:, :1], (h, lo)), rel_bias,
                         jnp.broadcast_to(rel_bias[:, -1:], (h, ntot - lo - n_rel))], axis=1).astype(F32)
    s_rev = jnp.pad(s[:, ::-1], ((0, 0), (0, 1)))
    skew = jnp.broadcast_to(s_rev[:, None, :], (h, sb, ntot + 1)).reshape(h, -1)[:, :sb * ntot]
    toep = skew.reshape(h, sb, ntot)[:, :, sb - 1:sb - 1 + kw]
    r = np.arange(sb)[:, None]
    kk = np.arange(kw)[None, :]
    qc, kc = r // CHUNK, kk // CHUNK
    band = (kc >= qc) & (kc <= qc + LEFT_CHUNKS)
    return jnp.where(band[None], toep, NEG_INF)


def _visit_plan(offs, m, tm, nvis):
    cnt = offs[1:] - offs[:-1]
    first = offs[:-1] // tm
    last = jnp.maximum(offs[1:] - 1, 0) // tm
    per = jnp.where(cnt > 0, last - first + 1, 0)
    vend = jnp.cumsum(per)
    vstart = vend - per
    total = vend[-1]
    v = jnp.arange(nvis, dtype=jnp.int32)
    g = jnp.minimum(jnp.sum(vend[None, :] <= v[:, None], axis=1), N_EXPERTS - 1).astype(jnp.int32)
    tile = (first[g] + v - vstart[g]).astype(jnp.int32)
    live = v < total
    last_tile = m // tm - 1
    tile = jnp.where(live, tile, last_tile)
    g = jnp.where(live, g, N_EXPERTS - 1)
    return tile, g, total.reshape(1).astype(jnp.int32)


def kernel(x_prompt, x_sample, c_prompt, c_sample, cache_k, cache_v, w_ada, b_ada, g_mix, w_in, g_gmlp_v,
           w_spatial, b_spatial, g_q, g_k, rel_bias, g_out_a, g_out_b, w_out, g_ffn, w_router, b_router,
           w_exp_gate, w_exp_up, w_exp_down, w_sh_gate, w_sh_up, w_sh_down):
    depth = w_ada.shape[0]
    assert depth == 1, "single layer"
    bp, sp, d = x_prompt.shape
    bs, ss, _ = x_sample.shape
    tp, ts = bp * sp, bs * ss
    t_all = tp + ts
    l = 0

    c_all = jnp.concatenate([c_prompt, c_sample], axis=0)
    nb = c_all.shape[0]
    nb_pad = -(-nb // 8) * 8
    c_all = jnp.pad(c_all, ((0, nb_pad - nb), (0, 0)))
    mod = _ada(c_all, w_ada[l], b_ada[l])[:nb].reshape(nb, 6, 1, d)
    sh1, sc1, gt1, sh2, sc2, gt2 = [mod[:, i] for i in range(6)]

    w_in_bf = w_in[l].astype(BF16)
    w_out_bf = w_out[l].astype(BF16)
    gv = g_gmlp_v[l].reshape(1, W_A)
    gq = jnp.tile(g_q[l], B_HEADS).reshape(1, W_B)
    gk = jnp.tile(g_k[l], B_HEADS).reshape(1, W_B)
    hid = np.arange(W_B) // HEAD_DIM
    hsum = jnp.asarray(hid[:, None] == hid[None, :], BF16)
    gmix = g_mix[l].reshape(1, d)
    goa = g_out_a[l].reshape(1, W_A)
    gob = g_out_b[l].reshape(1, W_B)
    gffn = g_ffn[l].reshape(1, d)
    bias_p = _bias_table(rel_bias[l], GMLP_BLOCK, ATT_PAST_WINDOW + GMLP_BLOCK)
    bsp_p = jnp.broadcast_to(b_spatial[l][:, :, None], (A_GROUPS, GMLP_BLOCK, A_GROUP_DIM))

    cl = cache_k.shape[2]
    us, vns, qs, ks_bf, vs_bf, k32s, v32s = _in_proj(
        x_sample, sh1[bp:], sc1[bp:], gmix, w_in_bf, gv, gq, gk, hsum,
        tm=ss, pad_rows=0, keep=ss, vn_dtype=F32)
    kcat = jnp.concatenate([cache_k[l].reshape(bs, cl, W_B).astype(BF16), ks_bf], axis=1)
    vcat = jnp.concatenate([cache_v[l].reshape(bs, cl, W_B).astype(BF16), vs_bf], axis=1)
    bias_s = bias_p[:, :ss, ATT_PAST_WINDOW - cl:ATT_PAST_WINDOW + ss]
    x1_s, h2_s = _mixer(x_sample, us, vns, qs, kcat, vcat, bias_s, w_spatial[l][:, :ss, :ss], bsp_p[:, :ss],
                        goa, gob, w_out_bf, gt1[bp:], gffn, sh2[bp:], sc2[bp:],
                        sb=ss, nsb=1, padded=False)
    new_k_sample = k32s.reshape(1, bs, ss, B_HEADS, HEAD_DIM)
    new_v_sample = v32s.reshape(1, bs, ss, B_HEADS, HEAD_DIM)
    new_gmlp_v_sample = vns.reshape(1, bs, ss, A_GROUPS, A_GROUP_DIM)

    keep = min(ATT_PAST_WINDOW, sp)
    u, vn, q, kpad, vpad, k32, v32 = _in_proj(
        x_prompt, sh1[:bp], sc1[:bp], gmix, w_in_bf, gv, gq, gk, hsum,
        tm=512, pad_rows=ATT_PAST_WINDOW, keep=keep, vn_dtype=BF16)
    x1, h2 = _mixer(x_prompt, u, vn, q, kpad, vpad, bias_p, w_spatial[l], bsp_p, goa, gob, w_out_bf,
                    gt1[:bp], gffn, sh2[:bp], sc2[:bp],
                    sb=GMLP_BLOCK, nsb=4 if sp % 512 == 0 and ts % 512 == 0 else 2, padded=True,
                    tail=(x1_s, h2_s))
    assert x1.shape[0] == t_all
    new_k_prompt = k32.reshape(1, bp, keep, B_HEADS, HEAD_DIM)
    new_v_prompt = v32.reshape(1, bp, keep, B_HEADS, HEAD_DIM)

    tmr = 512 if t_all % 512 == 0 else 256
    tmd = 512 if t_all % 512 == 0 else 256
    tme = 1024 if (t_all * TOP_K) % 1024 == 0 else 512
    assert t_all % tmr == 0 and (t_all * TOP_K) % tme == 0 and sp % 256 == 0 and ts % (2 * ss) == 0
    assert t_all * TOP_K < (1 << RANK_BITS)
    pk, gw_t, cnt = _route(h2, w_router[l].T.astype(BF16), b_router[l], tm=tmr)
    offs = jnp.concatenate([jnp.zeros((1,), jnp.int32), jnp.cumsum(cnt[:, 0]).astype(jnp.int32)])
    tms = next(c for c in (2048, 1536, 1024, 512, 256) if t_all % c == 0)
    dst_flat = _slots(pk, offs, tm=tms).T.reshape(-1)
    gw = jnp.pad(gw_t.T, ((0, 0), (0, LANES - TOP_K)))
    xs = _dispatch(h2, dst_flat, tm=tmd)
    m = t_all * TOP_K
    nvis = m // tme + N_EXPERTS - 1
    vt, vg, nv = _visit_plan(offs, m, tme, nvis)
    ys = _experts(vt, vg, offs, nv, xs, w_exp_gate[l], w_exp_up[l], w_exp_down[l], tm=tme)

    wsg, wsu, wsd = w_sh_gate[l].astype(BF16), w_sh_up[l].astype(BF16), w_sh_down[l].astype(BF16)
    y_prompt = _combine(x1, h2, gw, gt2[:bp], wsg, wsu, wsd, dst_flat, ys,
                        tm=512 if sp % 512 == 0 else 256, batch=bp, seq=sp, row_offset=0).reshape(bp, sp, d)
    gt2_tok = jnp.broadcast_to(gt2[bp:], (bs, ss, d)).reshape(ts, d)
    y_sample = _combine(x1, h2, gw, gt2_tok, wsg, wsu, wsd, dst_flat, ys,
                        tm=2 * ss, batch=bs, seq=ss, row_offset=tp).reshape(bs, ss, d)
    return (y_prompt, y_sample, new_k_prompt, new_v_prompt, new_k_sample, new_v_sample, new_gmlp_v_sample)
```

```python
import functools

import jax
import jax.numpy as jnp
import numpy as np
from jax import lax
from jax.experimental import pallas as pl
from jax.experimental.pallas import tpu as pltpu

F32 = jnp.float32
BF16 = jnp.bfloat16

D_MODEL = 1024
CHUNK = 64
GMLP_BLOCK = 128
W_A = 512
A_GROUPS = 4
A_GROUP_DIM = 128
W_B = 512
B_HEADS = 8
HEAD_DIM = 64
LEFT_CHUNKS = 8
ATT_PAST_WINDOW = LEFT_CHUNKS * CHUNK
REL_CLIP = 128
D_IN = 2 * W_A + 3 * W_B
N_EXPERTS = 256
TOP_K = 8
D_EXPERT = 256
ROUTED_SCALE = 2.5
EPS = 1e-6
NEG_INF = -1e30

RANK_BITS = 20
VMEM_LIMIT = 56 * 1024 * 1024


def _cparams(sem, vmem=VMEM_LIMIT):
    return pltpu.CompilerParams(dimension_semantics=sem, vmem_limit_bytes=vmem)


def _rms(x, axis=-1):
    return lax.rsqrt(jnp.mean(x * x, axis=axis, keepdims=True) + EPS)


LANES = 128
NCH = D_MODEL // LANES


def _load_rows(ref, n, lead=()):
    return jnp.concatenate([ref[(*lead, pl.ds(c, n, stride=NCH), slice(None))] for c in range(NCH)], axis=1)


def _store_rows(ref, val):
    n = val.shape[0]
    for c in range(NCH):
        ref[pl.ds(c, n, stride=NCH), :] = val[:, c * LANES:(c + 1) * LANES]


def _ada_kernel(c_ref, w_ref, b_ref, o_ref):
    c = c_ref[...]
    s = c * jax.nn.sigmoid(c)
    o_ref[...] = jnp.dot(s.astype(BF16), w_ref[...].astype(BF16), preferred_element_type=F32) + b_ref[...]


def _ada(c, w_ada, b_ada):
    n, d = c.shape
    nout = w_ada.shape[1]
    tn = 1024
    return pl.pallas_call(
        _ada_kernel,
        grid=(nout // tn,),
        in_specs=[pl.BlockSpec((n, d), lambda j: (0, 0)),
                  pl.BlockSpec((d, tn), lambda j: (0, j)),
                  pl.BlockSpec((1, tn), lambda j: (0, j))],
        out_specs=pl.BlockSpec((n, tn), lambda j: (0, j)),
        out_shape=jax.ShapeDtypeStruct((n, nout), F32),
        compiler_params=_cparams(("arbitrary",)),
        name="ada",
    )(c, w_ada, b_ada.reshape(1, nout))


def _in_proj_kernel(x_ref, sh_ref, sc_ref, gmix_ref, win_ref, gv_ref, gq_ref, gk_ref, hsum_ref,
                    u_ref, vn_ref, q_ref, kp_ref, vp_ref, k32_ref, v32_ref, *, npad, first_kept):
    i = pl.program_id(1)

    @pl.when(i < npad)
    def _():
        kp_ref[...] = jnp.zeros_like(kp_ref)
        vp_ref[...] = jnp.zeros_like(vp_ref)

    @pl.when(i >= npad)
    def _():
        x = x_ref[0]
        h = x * _rms(x) * gmix_ref[...]
        h = h * (1.0 + sc_ref[0]) + sh_ref[0]
        proj = jnp.dot(h.astype(BF16), win_ref[...], preferred_element_type=F32)
        u = jax.nn.gelu(proj[:, 0:W_A])
        u_ref[0] = u.astype(u_ref.dtype)
        v = jax.nn.gelu(proj[:, W_A:2 * W_A])
        for g in range(A_GROUPS):
            vg = v[:, g * A_GROUP_DIM:(g + 1) * A_GROUP_DIM]
            vn = vg * _rms(vg) * gv_ref[:, g * A_GROUP_DIM:(g + 1) * A_GROUP_DIM]
            vn_ref[0, :, g * A_GROUP_DIM:(g + 1) * A_GROUP_DIM] = vn.astype(vn_ref.dtype)
        q = proj[:, 2 * W_A:2 * W_A + W_B]
        k = proj[:, 2 * W_A + W_B:2 * W_A + 2 * W_B]
        va = proj[:, 2 * W_A + 2 * W_B:]
        q2 = jnp.dot((q * q).astype(BF16), hsum_ref[...], preferred_element_type=F32) * (1.0 / HEAD_DIM)
        k2 = jnp.dot((k * k).astype(BF16), hsum_ref[...], preferred_element_type=F32) * (1.0 / HEAD_DIM)
        qn = q * lax.rsqrt(q2 + EPS) * gq_ref[...]
        kn = k * lax.rsqrt(k2 + EPS) * gk_ref[...]
        q_ref[0] = (qn * (HEAD_DIM ** -0.5)).astype(q_ref.dtype)
        kp_ref[0] = kn.astype(kp_ref.dtype)
        vp_ref[0] = va.astype(vp_ref.dtype)

        @pl.when(i >= first_kept)
        def _():
            k32_ref[0] = kn
            v32_ref[0] = va


def _in_proj(x, sh1, sc1, g_mix, w_in_bf, gv, gq, gk, hsum, *, tm, pad_rows, keep, vn_dtype):
    b, l, d = x.shape
    assert l % tm == 0 and pad_rows % tm == 0 and keep % tm == 0
    npad = pad_rows // tm
    nt = l // tm
    first_kept = npad + nt - keep // tm
    row = lambda bi, i: (bi, jnp.maximum(i - npad, 0), 0)
    kept = lambda bi, i: (bi, jnp.maximum(i - first_kept, 0), 0)
    bvec = lambda bi, i: (bi, 0, 0)
    const2 = lambda bi, i: (0, 0)
    out_shapes = (
        jax.ShapeDtypeStruct((b, l, W_A), BF16),
        jax.ShapeDtypeStruct((b, l, W_A), vn_dtype),
        jax.ShapeDtypeStruct((b, l, W_B), BF16),
        jax.ShapeDtypeStruct((b, pad_rows + l, W_B), BF16),
        jax.ShapeDtypeStruct((b, pad_rows + l, W_B), BF16),
        jax.ShapeDtypeStruct((b, keep, W_B), F32),
        jax.ShapeDtypeStruct((b, keep, W_B), F32),
    )
    padrow = lambda bi, i: (bi, i, 0)
    return pl.pallas_call(
        functools.partial(_in_proj_kernel, npad=npad, first_kept=first_kept),
        grid=(b, nt + npad),
        in_specs=[pl.BlockSpec((1, tm, d), row),
                  pl.BlockSpec((1, 1, d), bvec),
                  pl.BlockSpec((1, 1, d), bvec),
                  pl.BlockSpec((1, d), const2),
                  pl.BlockSpec((d, D_IN), const2),
                  pl.BlockSpec((1, W_A), const2),
                  pl.BlockSpec((1, W_B), const2),
                  pl.BlockSpec((1, W_B), const2),
                  pl.BlockSpec((W_B, W_B), const2)],
        out_specs=(pl.BlockSpec((1, tm, W_A), row),
                   pl.BlockSpec((1, tm, W_A), row),
                   pl.BlockSpec((1, tm, W_B), row),
                   pl.BlockSpec((1, tm, W_B), padrow),
                   pl.BlockSpec((1, tm, W_B), padrow),
                   pl.BlockSpec((1, tm, W_B), kept),
                   pl.BlockSpec((1, tm, W_B), kept)),
        out_shape=out_shapes,
        compiler_params=_cparams(("arbitrary", "arbitrary")),
        name="in_proj",
    )(x, sh1, sc1, g_mix, w_in_bf, gv, gq, gk, hsum)


def _mixer_kernel(x_ref, u_ref, vn_ref, q_ref, kp_ref, vp_ref, bias_ref, wsp_ref, bsp_ref,
                  goa_ref, gob_ref, wout_ref, gt1_ref, gffn_ref, sh2_ref, sc2_ref, *rest,
                  sb, nsb, kw, padded, nj, nmain):
    x1_ref, h2_ref, merged_ref = rest[-3:]
    i = pl.program_id(0)
    if len(rest) == 5:
        tx1_ref, th2_ref = rest[:2]

        @pl.when(i >= nmain)
        def _():
            x1_ref[...] = tx1_ref[...]
            h2_ref[...] = th2_ref[...]

    @pl.when(i < nmain)
    def _():
        _mixer_body(x_ref, u_ref, vn_ref, q_ref, kp_ref, vp_ref, bias_ref, wsp_ref, bsp_ref,
                    goa_ref, gob_ref, wout_ref, gt1_ref, gffn_ref, sh2_ref, sc2_ref,
                    x1_ref, h2_ref, merged_ref, i % nj, sb=sb, nsb=nsb, kw=kw, padded=padded)


def _mixer_body(x_ref, u_ref, vn_ref, q_ref, kp_ref, vp_ref, bias_ref, wsp_ref, bsp_ref,
                goa_ref, gob_ref, wout_ref, gt1_ref, gffn_ref, sh2_ref, sc2_ref,
                x1_ref, h2_ref, merged_ref, j, *, sb, nsb, kw, padded):
    rows = sb * nsb
    ri = lax.broadcasted_iota(jnp.int32, (sb, sb), 0)
    ci = lax.broadcasted_iota(jnp.int32, (sb, sb), 1)
    gmask = (ci // CHUNK) <= (ri // CHUNK)
    lane = lax.broadcasted_iota(jnp.int32, (sb, 2 * HEAD_DIM), 1)
    lo_half = lane < HEAD_DIM
    kidx = lax.broadcasted_iota(jnp.int32, (sb, kw), 1)

    for s in range(nsb):
        r0 = s * sb
        vn = vn_ref[0, r0:r0 + sb, :].astype(BF16)
        u = u_ref[0, r0:r0 + sb, :].astype(F32)
        zs = []
        for g in range(A_GROUPS):
            w = jnp.where(gmask, wsp_ref[g], 0.0).astype(BF16)
            z = jnp.dot(w, vn[:, g * A_GROUP_DIM:(g + 1) * A_GROUP_DIM], preferred_element_type=F32)
            zs.append(z + bsp_ref[g])
        a = u * jnp.concatenate(zs, axis=1)
        a = a * _rms(a) * goa_ref[...]
        merged_ref[r0:r0 + sb, 0:W_A] = a.astype(BF16)

        if padded:
            blk0 = j * rows + r0
            kstart = pl.multiple_of(blk0, sb)
            kb = kp_ref[0, pl.ds(kstart, kw), :]
            vb = vp_ref[0, pl.ds(kstart, kw), :]
            valid = kidx >= (ATT_PAST_WINDOW - blk0)
        else:
            kb = kp_ref[0]
            vb = vp_ref[0]
            valid = None
        qb = q_ref[0, r0:r0 + sb, :]
        scs = []
        for h in range(B_HEADS):
            c0 = (h // 2) * 2 * HEAD_DIM
            qp = qb[:, c0:c0 + 2 * HEAD_DIM]
            qm = jnp.where(lo_half if h % 2 == 0 else jnp.logical_not(lo_half), qp, jnp.zeros_like(qp))
            sc = lax.dot_general(qm, kb[:, c0:c0 + 2 * HEAD_DIM], (((1,), (1,)), ((), ())),
                                 preferred_element_type=F32)
            scs.append(sc + bias_ref[h])
        ps, ls = [], []
        for sc in scs:
            if valid is not None:
                sc = jnp.where(valid, sc, NEG_INF)
            p = jnp.exp(sc - jnp.max(sc, axis=-1, keepdims=True))
            ls.append(jnp.sum(p, axis=-1, keepdims=True))
            ps.append(p.astype(BF16))
        os_ = []
        for h in range(B_HEADS):
            c0 = (h // 2) * 2 * HEAD_DIM
            os_.append(jnp.dot(ps[h], vb[:, c0:c0 + 2 * HEAD_DIM], preferred_element_type=F32) / ls[h])
        outs = [jnp.where(lo_half, os_[2 * hp], os_[2 * hp + 1]) for hp in range(B_HEADS // 2)]
        bo = jnp.concatenate(outs, axis=1)
        bo = bo * _rms(bo) * gob_ref[...]
        merged_ref[r0:r0 + sb, W_A:W_A + W_B] = bo.astype(BF16)

    mix = jnp.dot(merged_ref[...], wout_ref[...], preferred_element_type=F32)
    x1 = x_ref[0] + gt1_ref[0] * mix
    x1_ref[...] = x1
    h2 = x1 * _rms(x1) * gffn_ref[...]
    _store_rows(h2_ref, h2 * (1.0 + sc2_ref[0]) + sh2_ref[0])


def _mixer(x, u, vn, q, kp, vp, bias, wsp, bsp, goa, gob, wout_bf, gt1, gffn, sh2, sc2,
           *, sb, nsb, padded, tail=None):
    b, l, d = x.shape
    rows = sb * nsb
    assert l % rows == 0
    nj = l // rows
    nmain = b * nj
    ntail = 0
    if tail is not None:
        assert tail[0].shape[0] % rows == 0
        ntail = tail[0].shape[0] // rows
    total_rows = (nmain + ntail) * rows
    kw = bias.shape[2]
    kl = kp.shape[1]
    bidx = lambda i: jnp.minimum(i // nj, b - 1)
    row = lambda i: (bidx(i), i % nj, 0)
    bvec = lambda i: (bidx(i), 0, 0)
    c2 = lambda i: (0, 0)
    c3 = lambda i: (0, 0, 0)
    pool = lambda i: (i, 0)
    in_specs = [pl.BlockSpec((1, rows, d), row),
                pl.BlockSpec((1, rows, W_A), row),
                pl.BlockSpec((1, rows, W_A), row),
                pl.BlockSpec((1, rows, W_B), row),
                pl.BlockSpec((1, kl, W_B), bvec),
                pl.BlockSpec((1, kl, W_B), bvec),
                pl.BlockSpec(bias.shape, c3),
                pl.BlockSpec(wsp.shape, c3),
                pl.BlockSpec(bsp.shape, c3),
                pl.BlockSpec((1, W_A), c2),
                pl.BlockSpec((1, W_B), c2),
                pl.BlockSpec((d, d), c2),
                pl.BlockSpec((1, 1, d), bvec),
                pl.BlockSpec((1, d), c2),
                pl.BlockSpec((1, 1, d), bvec),
                pl.BlockSpec((1, 1, d), bvec)]
    args = [x, u, vn, q, kp, vp, bias, wsp, bsp, goa, gob, wout_bf, gt1, gffn, sh2, sc2]
    if tail is not None:
        tmap = lambda i: (jnp.maximum(i - nmain, 0), 0)
        in_specs += [pl.BlockSpec((rows, d), tmap), pl.BlockSpec((rows * NCH, LANES), tmap)]
        args += list(tail)
    return pl.pallas_call(
        functools.partial(_mixer_kernel, sb=sb, nsb=nsb, kw=kw, padded=padded, nj=nj, nmain=nmain),
        grid=(nmain + ntail,),
        in_specs=in_specs,
        out_specs=(pl.BlockSpec((rows, d), pool), pl.BlockSpec((rows * NCH, LANES), pool)),
        out_shape=(jax.ShapeDtypeStruct((total_rows, d), F32),
                   jax.ShapeDtypeStruct((total_rows * NCH, LANES), F32)),
        scratch_shapes=[pltpu.VMEM((rows, d), BF16)],
        compiler_params=_cparams(("arbitrary",)),
        name="mixer",
    )(*args)


def _route_kernel(h2_ref, wrt_ref, br_ref, pk_ref, gw_ref, cnt_ref, run_ref, *, tm):
    i = pl.program_id(0)

    @pl.when(i == 0)
    def _():
        run_ref[...] = jnp.zeros_like(run_ref)

    x = _load_rows(h2_ref, tm).astype(BF16)
    logits = lax.dot_general(wrt_ref[...], x, (((1,), (1,)), ((), ())), preferred_element_type=F32)
    aff = jax.nn.sigmoid(logits)
    sel = aff + br_ref[...]
    row = lax.broadcasted_iota(jnp.int32, (N_EXPERTS, tm), 0)
    picks = []
    onehot = jnp.zeros((N_EXPERTS, tm), F32)
    for _ in range(TOP_K):
        m = jnp.max(sel, axis=0, keepdims=True)
        ik = jnp.min(jnp.where(sel == m, row, N_EXPERTS), axis=0, keepdims=True)
        oh = row == ik
        sel = jnp.where(oh, -jnp.inf, sel)
        onehot = jnp.where(oh, 1.0, onehot)
        picks.append((ik, oh))
    ri = lax.broadcasted_iota(jnp.int32, (tm, tm), 0)
    ci = lax.broadcasted_iota(jnp.int32, (tm, tm), 1)
    earlier = (ri < ci).astype(BF16)
    before = jnp.dot(onehot.astype(BF16), earlier, preferred_element_type=F32) + run_ref[...]
    affs = [jnp.sum(jnp.where(oh, aff, 0.0), axis=0, keepdims=True) for _, oh in picks]
    tot = affs[0]
    for a in affs[1:]:
        tot = tot + a
    for k, (ik, oh) in enumerate(picks):
        rank = jnp.sum(jnp.where(oh, before, 0.0), axis=0, keepdims=True).astype(jnp.int32)
        pk_ref[k:k + 1, :] = ik * (1 << RANK_BITS) + rank
        gw_ref[k:k + 1, :] = affs[k] / tot * ROUTED_SCALE
    run = run_ref[...] + jnp.sum(onehot, axis=1, keepdims=True)
    run_ref[...] = run
    cnt_ref[...] = run.astype(jnp.int32)


def _route(h2, wrt_bf, b_router, *, tm):
    t, d = h2.shape[0] // NCH, D_MODEL
    return pl.pallas_call(
        functools.partial(_route_kernel, tm=tm),
        grid=(t // tm,),
        in_specs=[pl.BlockSpec((tm * NCH, LANES), lambda i: (i, 0)),
                  pl.BlockSpec((N_EXPERTS, d), lambda i: (0, 0)),
                  pl.BlockSpec((N_EXPERTS, 1), lambda i: (0, 0))],
        out_specs=(pl.BlockSpec((TOP_K, tm), lambda i: (0, i)),
                   pl.BlockSpec((TOP_K, tm), lambda i: (0, i)),
                   pl.BlockSpec((N_EXPERTS, 1), lambda i: (0, 0))),
        out_shape=(jax.ShapeDtypeStruct((TOP_K, t), jnp.int32),
                   jax.ShapeDtypeStruct((TOP_K, t), F32),
                   jax.ShapeDtypeStruct((N_EXPERTS, 1), jnp.int32)),
        scratch_shapes=[pltpu.VMEM((N_EXPERTS, 1), F32)],
        compiler_params=_cparams(("arbitrary",)),
        name="route",
    )(h2, wrt_bf, b_router.reshape(N_EXPERTS, 1))


def _idx_copy(idx_hbm, idx_smem, sem, step, n, half):
    return pltpu.make_async_copy(idx_hbm.at[pl.ds(pl.multiple_of(step * n, n), n)],
                                 idx_smem.at[pl.ds(half * n, n)], sem.at[half])


def _slots_kernel(pk_ref, offs_ref, dst_ref):
    half = N_EXPERTS // 2
    shape = (TOP_K, LANES)
    lo_tab = jnp.broadcast_to(offs_ref[:, 0:half], shape)
    hi_tab = jnp.broadcast_to(offs_ref[:, half:N_EXPERTS], shape)
    for j in range(pk_ref.shape[1] // LANES):
        lanes = slice(j * LANES, (j + 1) * LANES)
        p = pk_ref[:, lanes]
        e = p >> RANK_BITS
        start = jnp.where(e >= half, jnp.take_along_axis(hi_tab, e & (half - 1), axis=1),
                          jnp.take_along_axis(lo_tab, e & (half - 1), axis=1))
        dst_ref[:, lanes] = ((p & ((1 << RANK_BITS) - 1)) + start) * NCH


def _slots(pk, offs, *, tm):
    t = pk.shape[1]
    return pl.pallas_call(
        _slots_kernel,
        grid=(t // tm,),
        in_specs=[pl.BlockSpec((TOP_K, tm), lambda i: (0, i)),
                  pl.BlockSpec((1, N_EXPERTS), lambda i: (0, 0))],
        out_specs=pl.BlockSpec((TOP_K, tm), lambda i: (0, i)),
        out_shape=jax.ShapeDtypeStruct((TOP_K, t), jnp.int32),
        compiler_params=_cparams(("arbitrary",)),
        name="slots",
    )(pk, offs[:N_EXPERTS].reshape(1, N_EXPERTS))


def _dispatch_kernel(h2_ref, dst_hbm, xs_hbm, idx_smem, idx_sem, row_sem, *, tm):
    i = pl.program_id(0)
    n = tm * TOP_K

    @pl.when(i == 0)
    def _():
        _idx_copy(dst_hbm, idx_smem, idx_sem, i, n, 0).start()

    def scatter(half):
        _idx_copy(dst_hbm, idx_smem, idx_sem, i, n, half).wait()

        @pl.when(i + 1 < pl.num_programs(0))
        def _():
            _idx_copy(dst_hbm, idx_smem, idx_sem, i + 1, n, 1 - half).start()

        def body(t8, carry):
            r0 = pl.multiple_of(t8 * (8 * NCH), 8 * NCH)
            b = half * n + t8 * (8 * TOP_K)
            for tt in range(8):
                src = h2_ref.at[pl.ds(r0 + tt * NCH, NCH)]
                for k in range(TOP_K):
                    dst = pl.multiple_of(idx_smem[b + tt * TOP_K + k], NCH)
                    pltpu.make_async_copy(src, xs_hbm.at[pl.ds(dst, NCH)], row_sem).start(priority=k % 2)
            return carry

        lax.fori_loop(0, tm // 8, body, 0)

    for half in range(2):
        pl.when(i % 2 == half)(functools.partial(scatter, half))
    pltpu.make_async_copy(xs_hbm.at[pl.ds(0, n * NCH)], xs_hbm.at[pl.ds(0, n * NCH)], row_sem).wait()


def _dispatch(h2, dst_flat, *, tm):
    t = h2.shape[0] // NCH
    n = tm * TOP_K
    return pl.pallas_call(
        functools.partial(_dispatch_kernel, tm=tm),
        grid=(t // tm,),
        in_specs=[pl.BlockSpec((tm * NCH, LANES), lambda i: (i, 0)),
                  pl.BlockSpec(memory_space=pl.ANY)],
        out_specs=pl.BlockSpec(memory_space=pl.ANY),
        scratch_shapes=[pltpu.SMEM((2 * n,), jnp.int32),
                        pltpu.SemaphoreType.DMA((2,)),
                        pltpu.SemaphoreType.DMA(())],
        out_shape=jax.ShapeDtypeStruct((t * TOP_K * NCH, LANES), F32),
        compiler_params=_cparams(("arbitrary",)),
        name="dispatch",
    )(h2, dst_flat)


X_BUFS = 3


def _experts_kernel(vt_ref, vg_ref, offs_ref, nv_ref, xs_hbm, wg_ref, wu_ref, wd_ref, ys_ref,
                    wgb_ref, wub_ref, wdb_ref, xbuf_ref, xsem, *, tm, ntiles):
    v = pl.program_id(0)

    def x_copy(t):
        row0 = pl.multiple_of(t * (tm * NCH), tm * NCH)
        return pltpu.make_async_copy(xs_hbm.at[pl.ds(row0, tm * NCH)], xbuf_ref.at[t % X_BUFS], xsem.at[t % X_BUFS])

    @pl.when(v == 0)
    def _():
        for t in range(min(X_BUFS - 1, ntiles)):
            x_copy(t).start()

    @pl.when(v < nv_ref[0])
    def _():
        g = vg_ref[v]
        tile = vt_ref[v]
        prev = jnp.maximum(v - 1, 0)
        first = jnp.logical_or(v == 0, vt_ref[prev] != tile)

        @pl.when(first)
        def _():
            @pl.when(tile + (X_BUFS - 1) < ntiles)
            def _():
                x_copy(tile + (X_BUFS - 1)).start()

            x_copy(tile).wait()

        xs_ref = xbuf_ref.at[tile % X_BUFS]

        @pl.when(jnp.logical_or(v == 0, vg_ref[prev] != g))
        def _():
            wgb_ref[...] = wg_ref[0].astype(BF16)
            wub_ref[...] = wu_ref[0].astype(BF16)
            wdb_ref[...] = wd_ref[0].astype(BF16)

        def swiglu_rows(r0, nr, masked):
            x = jnp.concatenate([xs_ref[pl.ds(r0 * NCH + c, nr, stride=NCH), :] for c in range(NCH)],
                                axis=1).astype(BF16)
            hg = jnp.dot(x, wgb_ref[...], preferred_element_type=F32)
            hu = jnp.dot(x, wub_ref[...], preferred_element_type=F32)
            hm = (hg * jax.nn.sigmoid(hg) * hu).astype(BF16)
            if masked:
                rowid = tile * tm + r0 + lax.broadcasted_iota(jnp.int32, (nr, 1), 0)
                mine = jnp.logical_and(rowid >= offs_ref[g], rowid < offs_ref[g + 1])
            for c2 in range(NCH // 2):
                y2 = jnp.dot(hm, wdb_ref[:, c2 * 2 * LANES:(c2 + 1) * 2 * LANES], preferred_element_type=F32)
                for j in range(2):
                    dst = pl.ds(r0 * NCH + 2 * c2 + j, nr, stride=NCH)
                    yc = y2[:, j * LANES:(j + 1) * LANES]
                    ys_ref[dst, :] = jnp.where(mine, yc, ys_ref[dst, :]) if masked else yc

        def visit(masked):
            for r0 in range(0, tm, tm // 2):
                swiglu_rows(r0, tm // 2, masked)

        pl.when(first)(functools.partial(visit, False))
        pl.when(jnp.logical_not(first))(functools.partial(visit, True))


def _experts(vt, vg, offs, nv, xs, w_gate, w_up, w_down, *, tm):
    m, d = xs.shape[0] // NCH, D_MODEL
    nvis = vt.shape[0]
    de = w_gate.shape[2]
    xmap = lambda v, vt, vg, offs, nv: (vt[v], 0)
    wmap = lambda v, vt, vg, offs, nv: (vg[v], 0, 0)
    grid_spec = pltpu.PrefetchScalarGridSpec(
        num_scalar_prefetch=4,
        grid=(nvis,),
        in_specs=[pl.BlockSpec(memory_space=pl.ANY),
                  pl.BlockSpec((1, d, de), wmap),
                  pl.BlockSpec((1, d, de), wmap),
                  pl.BlockSpec((1, de, d), wmap)],
        out_specs=pl.BlockSpec((tm * NCH, LANES), xmap),
        scratch_shapes=[pltpu.VMEM((d, de), BF16), pltpu.VMEM((d, de), BF16), pltpu.VMEM((de, d), BF16),
                        pltpu.VMEM((X_BUFS, tm * NCH, LANES), F32), pltpu.SemaphoreType.DMA((X_BUFS,))],
    )
    return pl.pallas_call(
        functools.partial(_experts_kernel, tm=tm, ntiles=m // tm),
        grid_spec=grid_spec,
        out_shape=jax.ShapeDtypeStruct((m * NCH, LANES), F32),
        compiler_params=_cparams(("arbitrary",)),
        name="experts",
    )(vt, vg, offs, nv, xs, w_gate, w_up, w_down)


def _combine_kernel(x1_ref, h2_ref, gw_ref, gt2_ref, wsg_ref, wsu_ref, wsd_ref, dst_hbm, ys_hbm,
                    y_ref, idx_smem, buf_ref, acc_ref, idx_sem, row_sem, *, tm, step0, per_token_gate):
    i = pl.program_id(0)
    nsteps = pl.num_programs(0)
    n = tm * TOP_K

    def issue(b, r0, half):
        for tt in range(8):
            for k in range(TOP_K):
                src = pl.multiple_of(idx_smem[b + tt * TOP_K + k], NCH)
                pltpu.make_async_copy(ys_hbm.at[pl.ds(src, NCH)],
                                      buf_ref.at[half, k, pl.ds(r0 + tt * NCH, NCH)],
                                      row_sem.at[half]).start(priority=k % 2)

    def finish8(t8, cur):
        t0 = pl.multiple_of(t8 * 8, 8)
        gw = gw_ref[pl.ds(t0, 8), :]
        for c in range(NCH):
            lanes = slice(c * LANES, (c + 1) * LANES)
            ffn = acc_ref[pl.ds(t0, 8), lanes]
            for k in range(TOP_K):
                ffn = ffn + gw[:, k:k + 1] * buf_ref[cur, k, pl.ds(t0 * NCH + c, 8, stride=NCH), :]
            gate = gt2_ref[pl.ds(t0, 8), lanes] if per_token_gate else gt2_ref[0][:, lanes]
            y_ref[pl.ds(t0, 8), lanes] = x1_ref[pl.ds(t0, 8), lanes] + gate * ffn

    @pl.when(i == 0)
    def _():
        _idx_copy(dst_hbm, idx_smem, idx_sem, step0, n, 0).start()
        _idx_copy(dst_hbm, idx_smem, idx_sem, step0, n, 0).wait()

        def body(t8, carry):
            issue(t8 * (8 * TOP_K), pl.multiple_of(t8 * (8 * NCH), 8 * NCH), 0)
            return carry

        lax.fori_loop(0, tm // 8, body, 0)

        @pl.when(nsteps > 1)
        def _():
            _idx_copy(dst_hbm, idx_smem, idx_sem, step0 + 1, n, 1).start()

    h2 = _load_rows(h2_ref, tm).astype(BF16)
    sg = jnp.dot(h2, wsg_ref[...], preferred_element_type=F32)
    su = jnp.dot(h2, wsu_ref[...], preferred_element_type=F32)
    acc_ref[...] = jnp.dot((sg * jax.nn.sigmoid(sg) * su).astype(BF16), wsd_ref[...], preferred_element_type=F32)

    def step(cur, has_next):
        pltpu.make_async_copy(ys_hbm.at[pl.ds(0, n * NCH)], ys_hbm.at[pl.ds(0, n * NCH)], row_sem.at[cur]).wait()
        if has_next:
            _idx_copy(dst_hbm, idx_smem, idx_sem, step0 + i + 1, n, 1 - cur).wait()

        def body(t8, carry):
            if has_next:
                issue((1 - cur) * n + t8 * (8 * TOP_K), pl.multiple_of(t8 * (8 * NCH), 8 * NCH), 1 - cur)
            finish8(t8, cur)
            return carry

        lax.fori_loop(0, tm // 8, body, 0)

        if has_next:
            @pl.when(i + 2 < nsteps)
            def _():
                _idx_copy(dst_hbm, idx_smem, idx_sem, step0 + i + 2, n, cur).start()

    for cur in range(2):
        pl.when(jnp.logical_and(i % 2 == cur, i + 1 < nsteps))(functools.partial(step, cur, True))
        pl.when(jnp.logical_and(i % 2 == cur, i + 1 == nsteps))(functools.partial(step, cur, False))


def _combine(x1, h2, gw, gt2, wsg, wsu, wsd, dst_flat, ys, *, tm, batch, seq, row_offset):
    d = x1.shape[1]
    n = tm * TOP_K
    per_token_gate = gt2.ndim == 2
    assert per_token_gate or seq % tm == 0
    per_b = max(seq // tm, 1)
    step0 = row_offset // tm
    tok = lambda i: (step0 + i, 0)
    c2 = lambda i: (0, 0)
    gate_spec = (pl.BlockSpec((tm, d), lambda i: (i, 0)) if per_token_gate
                 else pl.BlockSpec((1, 1, d), lambda i: (i // per_b, 0, 0)))
    return pl.pallas_call(
        functools.partial(_combine_kernel, tm=tm, step0=step0, per_token_gate=per_token_gate),
        grid=(batch * seq // tm,),
        in_specs=[pl.BlockSpec((tm, d), tok),
                  pl.BlockSpec((tm * NCH, LANES), tok),
                  pl.BlockSpec((tm, LANES), tok),
                  gate_spec,
                  pl.BlockSpec(wsg.shape, c2),
                  pl.BlockSpec(wsu.shape, c2),
                  pl.BlockSpec(wsd.shape, c2),
                  pl.BlockSpec(memory_space=pl.ANY),
                  pl.BlockSpec(memory_space=pl.ANY)],
        out_specs=pl.BlockSpec((tm, d), lambda i: (i, 0)),
        scratch_shapes=[pltpu.SMEM((2 * n,), jnp.int32),
                        pltpu.VMEM((2, TOP_K, tm * NCH, LANES), F32),
                        pltpu.VMEM((tm, d), F32),
                        pltpu.SemaphoreType.DMA((2,)),
                        pltpu.SemaphoreType.DMA((2,))],
        out_shape=jax.ShapeDtypeStruct((batch * seq, d), F32),
        compiler_params=_cparams(("arbitrary",)),
        name="combine",
    )(x1, h2, gw, gt2, wsg, wsu, wsd, dst_flat, ys)


def _bias_table(rel_bias, sb, kw):
    h, n_rel = rel_bias.shape
    lo = (kw - 1) - ATT_PAST_WINDOW - (CHUNK - 1)
    ntot = sb + kw - 1
    s = jnp.concatenate([jnp.broadcast_to(rel_bias[:, :1], (h, lo)), rel_bias,
                         jnp.broadcast_to(rel_bias[:, -1:], (h, ntot - lo - n_rel))], axis=1).astype(F32)
    s_rev = jnp.pad(s[:, ::-1], ((0, 0), (0, 1)))
    skew = jnp.broadcast_to(s_rev[:, None, :], (h, sb, ntot + 1)).reshape(h, -1)[:, :sb * ntot]
    toep = skew.reshape(h, sb, ntot)[:, :, sb - 1:sb - 1 + kw]
    r = np.arange(sb)[:, None]
    kk = np.arange(kw)[None, :]
    qc, kc = r // CHUNK, kk // CHUNK
    band = (kc >= qc) & (kc <= qc + LEFT_CHUNKS)
    return jnp.where(band[None], toep, NEG_INF)


def _visit_plan(offs, m, tm, nvis):
    cnt = offs[1:] - offs[:-1]
    first = offs[:-1] // tm
    last = jnp.maximum(offs[1:] - 1, 0) // tm
    per = jnp.where(cnt > 0, last - first + 1, 0)
    vend = jnp.cumsum(per)
    vstart = vend - per
    total = vend[-1]
    v = jnp.arange(nvis, dtype=jnp.int32)
    g = jnp.minimum(jnp.sum(vend[None, :] <= v[:, None], axis=1), N_EXPERTS - 1).astype(jnp.int32)
    tile = (first[g] + v - vstart[g]).astype(jnp.int32)
    live = v < total
    last_tile = m // tm - 1
    tile = jnp.where(live, tile, last_tile)
    g = jnp.where(live, g, N_EXPERTS - 1)
    return tile, g, total.reshape(1).astype(jnp.int32)


def kernel(x_prompt, x_sample, c_prompt, c_sample, cache_k, cache_v, w_ada, b_ada, g_mix, w_in, g_gmlp_v,
           w_spatial, b_spatial, g_q, g_k, rel_bias, g_out_a, g_out_b, w_out, g_ffn, w_router, b_router,
           w_exp_gate, w_exp_up, w_exp_down, w_sh_gate, w_sh_up, w_sh_down):
    depth = w_ada.shape[0]
    assert depth == 1, "single layer"
    bp, sp, d = x_prompt.shape
    bs, ss, _ = x_sample.shape
    tp, ts = bp * sp, bs * ss
    t_all = tp + ts
    l = 0

    c_all = jnp.concatenate([c_prompt, c_sample], axis=0)
    nb = c_all.shape[0]
    nb_pad = -(-nb // 8) * 8
    c_all = jnp.pad(c_all, ((0, nb_pad - nb), (0, 0)))
    mod = _ada(c_all, w_ada[l], b_ada[l])[:nb].reshape(nb, 6, 1, d)
    sh1, sc1, gt1, sh2, sc2, gt2 = [mod[:, i] for i in range(6)]

    w_in_bf = w_in[l].astype(BF16)
    w_out_bf = w_out[l].astype(BF16)
    gv = g_gmlp_v[l].reshape(1, W_A)
    gq = jnp.tile(g_q[l], B_HEADS).reshape(1, W_B)
    gk = jnp.tile(g_k[l], B_HEADS).reshape(1, W_B)
    hid = np.arange(W_B) // HEAD_DIM
    hsum = jnp.asarray(hid[:, None] == hid[None, :], BF16)
    gmix = g_mix[l].reshape(1, d)
    goa = g_out_a[l].reshape(1, W_A)
    gob = g_out_b[l].reshape(1, W_B)
    gffn = g_ffn[l].reshape(1, d)
    bias_p = _bias_table(rel_bias[l], GMLP_BLOCK, ATT_PAST_WINDOW + GMLP_BLOCK)
    bsp_p = jnp.broadcast_to(b_spatial[l][:, :, None], (A_GROUPS, GMLP_BLOCK, A_GROUP_DIM))

    cl = cache_k.shape[2]
    us, vns, qs, ks_bf, vs_bf, k32s, v32s = _in_proj(
        x_sample, sh1[bp:], sc1[bp:], gmix, w_in_bf, gv, gq, gk, hsum,
        tm=ss, pad_rows=0, keep=ss, vn_dtype=F32)
    kcat = jnp.concatenate([cache_k[l].reshape(bs, cl, W_B).astype(BF16), ks_bf], axis=1)
    vcat = jnp.concatenate([cache_v[l].reshape(bs, cl, W_B).astype(BF16), vs_bf], axis=1)
    bias_s = bias_p[:, :ss, ATT_PAST_WINDOW - cl:ATT_PAST_WINDOW + ss]
    x1_s, h2_s = _mixer(x_sample, us, vns, qs, kcat, vcat, bias_s, w_spatial[l][:, :ss, :ss], bsp_p[:, :ss],
                        goa, gob, w_out_bf, gt1[bp:], gffn, sh2[bp:], sc2[bp:],
                        sb=ss, nsb=1, padded=False)
    new_k_sample = k32s.reshape(1, bs, ss, B_HEADS, HEAD_DIM)
    new_v_sample = v32s.reshape(1, bs, ss, B_HEADS, HEAD_DIM)
    new_gmlp_v_sample = vns.reshape(1, bs, ss, A_GROUPS, A_GROUP_DIM)

    keep = min(ATT_PAST_WINDOW, sp)
    u, vn, q, kpad, vpad, k32, v32 = _in_proj(
        x_prompt, sh1[:bp], sc1[:bp], gmix, w_in_bf, gv, gq, gk, hsum,
        tm=512, pad_rows=ATT_PAST_WINDOW, keep=keep, vn_dtype=BF16)
    x1, h2 = _mixer(x_prompt, u, vn, q, kpad, vpad, bias_p, w_spatial[l], bsp_p, goa, gob, w_out_bf,
                    gt1[:bp], gffn, sh2[:bp], sc2[:bp],
                    sb=GMLP_BLOCK, nsb=4 if sp % 512 == 0 and ts % 512 == 0 else 2, padded=True,
                    tail=(x1_s, h2_s))
    assert x1.shape[0] == t_all
    new_k_prompt = k32.reshape(1, bp, keep, B_HEADS, HEAD_DIM)
    new_v_prompt = v32.reshape(1, bp, keep, B_HEADS, HEAD_DIM)

    tmr = 512 if t_all % 512 == 0 else 256
    tmd = 512 if t_all % 512 == 0 else 256
    tme = 1024 if (t_all * TOP_K) % 1024 == 0 else 512
    assert t_all % tmr == 0 and (t_all * TOP_K) % tme == 0 and sp % 256 == 0 and ts % (2 * ss) == 0
    assert t_all * TOP_K < (1 << RANK_BITS)
    pk, gw_t, cnt = _route(h2, w_router[l].T.astype(BF16), b_router[l], tm=tmr)
    offs = jnp.concatenate([jnp.zeros((1,), jnp.int32), jnp.cumsum(cnt[:, 0]).astype(jnp.int32)])
    tms = next(c for c in (2048, 1536, 1024, 512, 256) if t_all % c == 0)
    dst_flat = _slots(pk, offs, tm=tms).T.reshape(-1)
    gw = jnp.pad(gw_t.T, ((0, 0), (0, LANES - TOP_K)))
    xs = _dispatch(h2, dst_flat, tm=tmd)
    m = t_all * TOP_K
    nvis = m // tme + N_EXPERTS - 1
    vt, vg, nv = _visit_plan(offs, m, tme, nvis)
    ys = _experts(vt, vg, offs, nv, xs, w_exp_gate[l], w_exp_up[l], w_exp_down[l], tm=tme)

    wsg, wsu, wsd = w_sh_gate[l].astype(BF16), w_sh_up[l].astype(BF16), w_sh_down[l].astype(BF16)
    y_prompt = _combine(x1, h2, gw, gt2[:bp], wsg, wsu, wsd, dst_flat, ys,
                        tm=512 if sp % 512 == 0 else 256, batch=bp, seq=sp, row_offset=0).reshape(bp, sp, d)
    gt2_tok = jnp.broadcast_to(gt2[bp:], (bs, ss, d)).reshape(ts, d)
    y_sample = _combine(x1, h2, gw, gt2_tok, wsg, wsu, wsd, dst_flat, ys,
                        tm=2 * ss, batch=bs, seq=ss, row_offset=tp).reshape(bs, ss, d)
    return (y_prompt, y_sample, new_k_prompt, new_v_prompt, new_k_sample, new_v_sample, new_gmlp_v_sample)
```
